```python
import math
import jax, jax.numpy as jnp
from jax import lax
import numpy as np

D_MODEL = 1024
BATCH = 16
SEQ = 256
DEPTH = 2
DEC_BATCH = 4
DEC_SEQ = 4096
PAST_LEN = 256

GRID_W = 64
ROPE_BASE = 10000.0
EPS = 1e-6
Q_BLOCK = 128
H_A = 4
DK_A = 64
DV_A = 128
GATE_RANK = 16
GATE_TAU = 16.0
CHUNK = 64
W_A = H_A * DV_A
H_B = 8
Q_RANK = 384
KV_RANK = 256
NOPE_B = 64
ROPE_B = 32
V_B = 64
QK_B = NOPE_B + ROPE_B
W_B = H_B * V_B
H_C = 4
D_C = 64
W_C = H_C * 2 * D_C
D_FF = 4 * D_MODEL
N_BRANCH = 3
IN_WIDTHS = (H_A * DK_A, H_A * DK_A, W_A, W_A, 2 * GATE_RANK, Q_RANK, KV_RANK, ROPE_B, H_C * 2 * D_C, H_C * 2 * D_C, W_C, N_BRANCH * D_MODEL)
IN_COLS = sum(IN_WIDTHS)

kernel_name = 'hybrid_gla_mla_diff_prefix_dit_step'


def rms_norm(x, g):
    xf = x.astype(jnp.float32)
    y = xf * lax.rsqrt(jnp.mean(xf * xf, axis=-1, keepdims=True) + EPS)
    return y.astype(x.dtype) * g


def grid_positions(n):
    rows = n // GRID_W
    t = jnp.arange(rows * GRID_W)
    return t // GRID_W, t % GRID_W


def _rotate(xh, pos):
    nf = xh.shape[-1] // 2
    inv = ROPE_BASE ** (-jnp.arange(nf, dtype=jnp.float32) / nf)
    ang = pos.astype(jnp.float32)[:, None] * inv[None, :]
    ang = ang.reshape((1, ang.shape[0]) + (1,) * (xh.ndim - 3) + (nf,))
    cos = jnp.cos(ang).astype(xh.dtype)
    sin = jnp.sin(ang).astype(xh.dtype)
    x1, x2 = xh[..., :nf], xh[..., nf:]
    return jnp.concatenate([x1 * cos - x2 * sin, x1 * sin + x2 * cos], axis=-1)


def axial_rope(x, row, col):
    half = x.shape[-1] // 2
    return jnp.concatenate([_rotate(x[..., :half], row), _rotate(x[..., half:], col)], axis=-1)


def rope_tail(x, row, col):
    return jnp.concatenate([x[..., :NOPE_B], axial_rope(x[..., NOPE_B:], row, col)], axis=-1)


def gla_chunk_scan(q, k, v, g, s0):
    B, S, H, _ = q.shape
    dv = v.shape[-1]
    n = S // CHUNK

    def chunks(t):
        return t.reshape(B, n, CHUNK, H, t.shape[-1]).transpose(1, 0, 3, 2, 4)

    causal = jnp.tril(jnp.ones((CHUNK, CHUNK), dtype=bool))[:, :, None]

    def step(state, inp):
        qc, kc, vc, gc = inp
        b = jnp.cumsum(gc, axis=2)
        inter = jnp.einsum('bhtd,bhdv->bhtv', qc * jnp.exp(b), state)
        rel = jnp.where(causal, b[:, :, :, None, :] - b[:, :, None, :, :], -jnp.inf)
        att = jnp.einsum('bhtd,bhsd,bhtsd->bhts', qc, kc, jnp.exp(rel))
        intra = jnp.einsum('bhts,bhsv->bhtv', att, vc)
        b_end = b[:, :, -1:, :]
        new_state = jnp.exp(b_end[:, :, 0, :, None]) * state + jnp.einsum('bhsd,bhsv->bhdv', kc * jnp.exp(b_end - b), vc)
        return new_state, inter + intra

    final, out = lax.scan(step, s0.astype(jnp.float32), (chunks(q), chunks(k), chunks(v), chunks(g)))
    out = out.transpose(1, 0, 3, 2, 4).reshape(B, S, H, dv)
    return out, final


def gla_branch(q, k, v, r, a_lr, w_a2, b_a, g_out, s0):
    B, S, _ = q.shape
    f32 = jnp.float32
    q = q.reshape(B, S, H_A, DK_A).astype(f32) * (DK_A ** -0.5)
    k = k.reshape(B, S, H_A, DK_A).astype(f32)
    v = v.reshape(B, S, H_A, DV_A).astype(f32)
    a = a_lr.reshape(B, S, 2, GATE_RANK).astype(f32)
    logits = jnp.einsum('bsjr,jrk->bsjk', a, w_a2.astype(f32)) + b_a.astype(f32)
    g = (jax.nn.log_sigmoid(logits) / GATE_TAU).reshape(B, S, 2, H_A, DK_A)
    o_f, s_f = gla_chunk_scan(q, k, v, g[:, :, 0], s0[:, 0])
    flip = lambda t: jnp.flip(t, axis=1)
    o_b, s_b = gla_chunk_scan(flip(q), flip(k), flip(v), flip(g[:, :, 1]), s0[:, 1])
    o = rms_norm(o_f + flip(o_b), g_out.astype(f32))
    y = o.reshape(B, S, W_A).astype(r.dtype) * jax.nn.silu(r)
    return y, jnp.stack([s_f, s_b], axis=1)


def mla_queries(qd, g_qa, w_uq, g_q):
    B, S, _ = qd.shape
    q = (rms_norm(qd, g_qa) @ w_uq).reshape(B, S, H_B, QK_B)
    return rms_norm(q, g_q)


def mla_keys_values(ckv, krope, w_uk, w_uv, g_k):
    B, T, _ = ckv.shape
    k_nope = (ckv @ w_uk).reshape(B, T, H_B, NOPE_B)
    k_rope = jnp.broadcast_to(krope[:, :, None, :], (B, T, H_B, ROPE_B))
    k = rms_norm(jnp.concatenate([k_nope, k_rope], axis=-1), g_k)
    v = (ckv @ w_uv).reshape(B, T, H_B, V_B)
    return k, v


def _blocks(q):
    B, S = q.shape[:2]
    return q.reshape((B, S // Q_BLOCK, Q_BLOCK) + q.shape[2:]).swapaxes(0, 1)


def _unblocks(o):
    n, B, qb = o.shape[:3]
    return o.swapaxes(0, 1).reshape((B, n * qb) + o.shape[3:])


def softmax_attend(q, k, v):
    scale = q.shape[-1] ** -0.5

    def block(qb):
        s = jnp.einsum('bqhd,bkhd->bhqk', qb, k).astype(jnp.float32) * scale
        p = jax.nn.softmax(s, axis=-1).astype(v.dtype)
        return jnp.einsum('bhqk,bkhe->bqhe', p, v)

    return _unblocks(lax.map(block, _blocks(q)))


def diff_attend(q, k, v, lam):
    scale = q.shape[-1] ** -0.5

    def block(qb):
        s = jnp.einsum('bqhcd,bkhcd->bchqk', qb, k).astype(jnp.float32) * scale
        p = jax.nn.softmax(s, axis=-1)
        a = (p[:, 0] - lam * p[:, 1]).astype(v.dtype)
        return jnp.einsum('bhqk,bkhe->bqhe', a, v)

    return _unblocks(lax.map(block, _blocks(q)))


def layer(x, cond, lp, lam_init, cached):
    B, S, _ = x.shape
    mod = jax.nn.silu(cond) @ lp['w_mod'] + lp['b_mod']
    sh1, sc1, gt1, sh2, sc2, gt2 = jnp.split(mod, 6, axis=-1)
    h = rms_norm(x, lp['g_norm1']) * (1 + sc1) + sh1
    offsets = [int(o) for o in np.cumsum(IN_WIDTHS)[:-1]]
    (aq, ak, av, ar, aa, qd, kvd, kr, dq, dk, dv, gates) = jnp.split(h @ lp['w_in'], offsets, axis=-1)
    is_latent = cached is not None

    s0 = cached[0] if is_latent else jnp.zeros((B, 2, H_A, DK_A, DV_A), jnp.float32)
    y_a, gla_state = gla_branch(aq, ak, av, ar, aa, lp['w_gla_a2'], lp['b_gla_a'], lp['g_gla_out'], s0)

    q_b = mla_queries(qd, lp['g_mla_qa'], lp['w_mla_uq'], lp['g_mla_q'])
    ckv = rms_norm(kvd, lp['g_mla_kva'])
    k_b, v_b = mla_keys_values(ckv, kr, lp['w_mla_uk'], lp['w_mla_uv'], lp['g_mla_k'])

    q_c = rms_norm(dq.reshape(B, S, H_C, 2, D_C), lp['g_diff_q'])
    k_c = rms_norm(dk.reshape(B, S, H_C, 2, D_C), lp['g_diff_k'])
    v_c = dv.reshape(B, S, H_C, 2 * D_C)

    if is_latent:
        row, col = grid_positions(S)
        q_b = rope_tail(q_b, row, col)
        k_b = rope_tail(k_b, row, col)
        k_bc, v_bc = mla_keys_values(cached[1], cached[2], lp['w_mla_uk'], lp['w_mla_uv'], lp['g_mla_k'])
        k_b = jnp.concatenate([k_bc, k_b], axis=1)
        v_b = jnp.concatenate([v_bc, v_b], axis=1)
        q_c = axial_rope(q_c, row, col)
        k_c = jnp.concatenate([cached[3], axial_rope(k_c, row, col)], axis=1)
        v_c = jnp.concatenate([cached[4], v_c], axis=1)
        new_state = None
    else:
        new_state = (gla_state.astype(x.dtype), ckv, kr, k_c, v_c)

    y_b = softmax_attend(q_b, k_b, v_b).reshape(B, S, W_B)
    lam_p = lp['lam_qk'].astype(jnp.float32)
    lam = jnp.exp(jnp.sum(lam_p[0] * lam_p[1])) - jnp.exp(jnp.sum(lam_p[2] * lam_p[3])) + lam_init
    y_c = rms_norm(diff_attend(q_c, k_c, v_c, lam), lp['g_diff_sub']) * (1.0 - lam_init)
    y_c = y_c.reshape(B, S, W_C)

    g_a, g_b, g_c = jnp.split(jax.nn.sigmoid(gates), N_BRANCH, axis=-1)
    merged = g_a * (y_a @ lp['w_o_gla']) + g_b * (y_b @ lp['w_o_mla']) + g_c * (y_c @ lp['w_o_diff'])
    x = x + gt1 * (merged @ lp['w_out'])

    h2 = rms_norm(x, lp['g_norm2']) * (1 + sc2) + sh2
    x = x + gt2 * (jnp.square(jax.nn.relu(h2 @ lp['w_mlp1'])) @ lp['w_mlp2'])
    return x, new_state


def setup_inputs(seed: int = 0) -> dict:
    key = jax.random.key(seed)
    ks = iter(jax.random.split(key, 48))
    f32 = jnp.float32

    def nrm(shape, scale=1.0):
        return jax.random.normal(next(ks), shape, f32) * scale

    def gain(shape):
        return 1.0 + nrm(shape, 0.02)

    D = D_MODEL
    return {
        'x_prompt': nrm((BATCH, SEQ, D)),
        'x_sample': nrm((DEC_BATCH, DEC_SEQ, D)),
        'state_gla': nrm((DEC_BATCH, DEPTH, 2, H_A, DK_A, DV_A), 0.5),
        'cache_mla_ckv': nrm((DEC_BATCH, DEPTH, PAST_LEN, KV_RANK)),
        'cache_mla_krope': nrm((DEC_BATCH, DEPTH, PAST_LEN, ROPE_B)),
        'cache_diff_k': nrm((DEC_BATCH, DEPTH, PAST_LEN, H_C, 2, D_C)),
        'cache_diff_v': nrm((DEC_BATCH, DEPTH, PAST_LEN, H_C, 2 * D_C)),
        'c': nrm((DEC_BATCH, D)),
        'c_ctx': nrm((D,)),
        'w_mod': nrm((DEPTH, D, 6 * D), 0.5 * D ** -0.5),
        'b_mod': nrm((DEPTH, 6 * D), 0.02),
        'g_norm1': gain((DEPTH, D)),
        'g_norm2': gain((DEPTH, D)),
        'w_in': nrm((DEPTH, D, IN_COLS), D ** -0.5),
        'w_gla_a2': nrm((DEPTH, 2, GATE_RANK, H_A * DK_A), GATE_RANK ** -0.5),
        'b_gla_a': nrm((DEPTH, 2, H_A * DK_A), 0.1),
        'g_gla_out': gain((DEPTH, DV_A)),
        'g_mla_qa': gain((DEPTH, Q_RANK)),
        'g_mla_kva': gain((DEPTH, KV_RANK)),
        'w_mla_uq': nrm((DEPTH, Q_RANK, H_B * QK_B), Q_RANK ** -0.5),
        'w_mla_uk': nrm((DEPTH, KV_RANK, H_B * NOPE_B), KV_RANK ** -0.5),
        'w_mla_uv': nrm((DEPTH, KV_RANK, H_B * V_B), KV_RANK ** -0.5),
        'g_mla_q': gain((DEPTH, QK_B)),
        'g_mla_k': gain((DEPTH, QK_B)),
        'g_diff_q': gain((DEPTH, D_C)),
        'g_diff_k': gain((DEPTH, D_C)),
        'lam_qk': nrm((DEPTH, 4, D_C), 0.1),
        'g_diff_sub': gain((DEPTH, 2 * D_C)),
        'w_o_gla': nrm((DEPTH, W_A, D), W_A ** -0.5),
        'w_o_mla': nrm((DEPTH, W_B, D), W_B ** -0.5),
        'w_o_diff': nrm((DEPTH, W_C, D), W_C ** -0.5),
        'w_out': nrm((DEPTH, D, D), D ** -0.5),
        'w_mlp1': nrm((DEPTH, D, D_FF), D ** -0.5),
        'w_mlp2': nrm((DEPTH, D_FF, D), D_FF ** -0.5),
    }


def reference(x_prompt, x_sample, state_gla, cache_mla_ckv, cache_mla_krope, cache_diff_k, cache_diff_v, c, c_ctx, w_mod, b_mod, g_norm1, g_norm2, w_in, w_gla_a2, b_gla_a, g_gla_out, g_mla_qa, g_mla_kva, w_mla_uq, w_mla_uk, w_mla_uv, g_mla_q, g_mla_k, g_diff_q, g_diff_k, lam_qk, g_diff_sub, w_o_gla, w_o_mla, w_o_diff, w_out, w_mlp1, w_mlp2):
    y_prompt = x_prompt
    y_sample = x_sample
    ctx_cond = c_ctx[None, None, :]
    lat_cond = c[:, None, :]
    st_gla, st_ckv, st_krope, st_dk, st_dv = [], [], [], [], []
    for l in range(DEPTH):
        lp = dict(w_mod=w_mod[l], b_mod=b_mod[l], g_norm1=g_norm1[l], g_norm2=g_norm2[l], w_in=w_in[l], w_gla_a2=w_gla_a2[l], b_gla_a=b_gla_a[l], g_gla_out=g_gla_out[l], g_mla_qa=g_mla_qa[l], g_mla_kva=g_mla_kva[l], w_mla_uq=w_mla_uq[l], w_mla_uk=w_mla_uk[l], w_mla_uv=w_mla_uv[l], g_mla_q=g_mla_q[l], g_mla_k=g_mla_k[l], g_diff_q=g_diff_q[l], g_diff_k=g_diff_k[l], lam_qk=lam_qk[l], g_diff_sub=g_diff_sub[l], w_o_gla=w_o_gla[l], w_o_mla=w_o_mla[l], w_o_diff=w_o_diff[l], w_out=w_out[l], w_mlp1=w_mlp1[l], w_mlp2=w_mlp2[l])
        lam_init = 0.8 - 0.6 * math.exp(-0.3 * l)
        y_prompt, st = layer(y_prompt, ctx_cond, lp, lam_init, None)
        st_gla.append(st[0])
        st_ckv.append(st[1])
        st_krope.append(st[2])
        st_dk.append(st[3])
        st_dv.append(st[4])
        cached = (state_gla[:, l], cache_mla_ckv[:, l], cache_mla_krope[:, l], cache_diff_k[:, l], cache_diff_v[:, l])
        y_sample, _ = layer(y_sample, lat_cond, lp, lam_init, cached)
    new_state_gla = jnp.stack(st_gla, axis=1)
    new_mla_ckv = jnp.stack(st_ckv, axis=1)
    new_mla_krope = jnp.stack(st_krope, axis=1)
    new_diff_k = jnp.stack(st_dk, axis=1)
    new_diff_v = jnp.stack(st_dv, axis=1)
    return (y_prompt, y_sample, new_state_gla, new_mla_ckv, new_mla_krope, new_diff_k, new_diff_v)
```

```python
import functools
import math

import numpy as np
import jax
import jax.numpy as jnp
from jax import lax
from jax.experimental import pallas as pl
from jax.experimental.pallas import tpu as pltpu

F32 = jnp.float32
BF16 = jnp.bfloat16

D_MODEL = 1024
DEPTH = 2
GRID_W = 64
ROPE_BASE = 10000.0
EPS = 1e-6
H_A, DK_A, DV_A = 4, 64, 128
GATE_RANK = 16
GATE_TAU = 16.0
CHUNK = 64
W_A = H_A * DV_A
H_B, Q_RANK, KV_RANK = 8, 384, 256
NOPE_B, ROPE_B, V_B = 64, 32, 64
QK_B = NOPE_B + ROPE_B
W_B = H_B * V_B
H_C, D_C = 4, 64
W_C = H_C * 2 * D_C
D_FF = 4 * D_MODEL
GATE_COLS = 3 * D_MODEL
LAT_SEQ = 4096

LANES = 128
HEAD_PAD = LANES
VMEM_LIMIT = 56 * 1024 * 1024

TOKEN_TILE = 256
Q_TILE = 256
KEY_TILE = 512
GLA_TILE = 256

C_AQ, C_AK, C_AV, C_AR = 0, 256, 512, 1024
C_QD, C_KVD = 1536, 1920
C_DQ, C_DK, C_DV = 2176, 2688, 3200
C_MISC, C_KRP = 3712, 3840
IN_COLS_A = 3968

NEG_BIG = -1e30


def _dot(a, b):
    return jnp.dot(a, b, preferred_element_type=F32)


def _dot_nt(a, b):
    return lax.dot_general(a, b, (((1,), (1,)), ((), ())), preferred_element_type=F32)


def _sigmoid(x):
    return 1.0 / (1.0 + jnp.exp(-x))


def _rms(x, g):
    ms = jnp.mean(x * x, axis=-1, keepdims=True)
    return x * lax.rsqrt(ms + EPS) * g


def _lane_iota(shape):
    return lax.broadcasted_iota(jnp.int32, shape, len(shape) - 1)


def _mod_kernel(c_ref, w_ref, b_ref, o_ref):
    c = c_ref[...]
    s = c * _sigmoid(c)
    o_ref[0] = _dot(s.astype(BF16), w_ref[0].astype(BF16)) + b_ref[0]


def _modulation(cond8, w_mod, b_mod):
    nt = 768
    return pl.pallas_call(
        _mod_kernel,
        grid=(DEPTH, 6 * D_MODEL // nt),
        in_specs=[
            pl.BlockSpec((8, D_MODEL), lambda l, j: (0, 0)),
            pl.BlockSpec((1, D_MODEL, nt), lambda l, j: (l, 0, j)),
            pl.BlockSpec((1, 1, nt), lambda l, j: (l, 0, j)),
        ],
        out_specs=pl.BlockSpec((1, 8, nt), lambda l, j: (l, 0, j)),
        out_shape=jax.ShapeDtypeStruct((DEPTH, 8, 6 * D_MODEL), F32),
        name="modulation",
    )(cond8, w_mod, b_mod.reshape(DEPTH, 1, 6 * D_MODEL))


def _rope_tables(n_rope, lane0, width):
    half = n_rope // 2
    nf = half // 2
    inv = ROPE_BASE ** (-np.arange(nf, dtype=np.float64) / nf)
    pos = np.arange(GRID_W, dtype=np.float64)[:, None]
    cr = np.zeros((GRID_W, LANES)); cc = np.ones((GRID_W, LANES))
    sr = np.zeros((GRID_W, LANES)); sc = np.zeros((GRID_W, LANES))
    for base in range(0, LANES, width):
        for d in range(n_rope):
            lane = base + lane0 + d
            within = d % half
            ang = pos[:, 0] * inv[within % nf]
            sign = -1.0 if within < nf else 1.0
            if d < half:
                cr[:, lane] = np.cos(ang); cc[:, lane] = 0.0
                sr[:, lane] = sign * np.sin(ang)
            else:
                cc[:, lane] = np.cos(ang)
                sc[:, lane] = sign * np.sin(ang)
    return tuple(jnp.asarray(t, F32) for t in (cr, cc, sr, sc))


def _tile_tables(tabs, row0, n_sub):
    cr_ref, cc_ref, sr_ref, sc_ref = tabs
    cs, ss = [], []
    for j in range(n_sub):
        r = (row0 + j) % GRID_W
        cs.append(cr_ref[pl.ds(r, 1), :] + cc_ref[...])
        ss.append(sr_ref[pl.ds(r, 1), :] + sc_ref[...])
    return jnp.concatenate(cs, axis=0), jnp.concatenate(ss, axis=0)


def _rotate(x, cos_t, sin_t, nf):
    lane = _lane_iota(x.shape)
    up = pltpu.roll(x, LANES - nf, 1)
    dn = pltpu.roll(x, nf, 1)
    partner = jnp.where((lane & nf) == 0, up, dn)
    return x * cos_t + partner * sin_t


def _inproj_kernel(*refs, rope, emit_state, tiles_per_seq):
    it = iter(refs)
    x_ref, mod_ref, g1_ref, w_ref, wa2_ref, ba_ref = (next(it) for _ in range(6))
    gqa_ref, gkva_ref, wuq_ref, wuk_ref, wuv_ref, gq_ref, gk_ref = (next(it) for _ in range(7))
    gdq_ref, gdk_ref = next(it), next(it)
    if rope:
        tab_m = tuple(next(it) for _ in range(4))
        tab_d = tuple(next(it) for _ in range(4))
    o_gq, o_gk, o_gv, o_sr, o_gf, o_gb = (next(it) for _ in range(6))
    o_mq, o_mk, o_mv, o_dq, o_dk, o_dv = (next(it) for _ in range(6))
    if emit_state:
        o_ckv, o_misc, o_dk32, o_dv32 = (next(it) for _ in range(4))

    tm = x_ref.shape[0]
    x = x_ref[...]
    m = mod_ref[0]
    h = _rms(x, g1_ref[...]) * (1.0 + m[1:2]) + m[0:1]
    y = _dot(h.astype(BF16), w_ref[...])

    o_gq[...] = y[:, C_AQ:C_AQ + 256] * (DK_A ** -0.5)
    o_gk[...] = y[:, C_AK:C_AK + 256]
    o_gv[...] = y[:, C_AV:C_AV + 512].astype(BF16)
    r = y[:, C_AR:C_AR + 512]
    o_sr[...] = r * _sigmoid(r)
    misc = y[:, C_MISC:C_MISC + LANES]
    logits = _dot(misc.astype(BF16), wa2_ref[...]) + ba_ref[...]
    logsig = jnp.minimum(logits, 0.0) - jnp.log(1.0 + jnp.exp(-jnp.abs(logits)))
    gate = logsig * (1.0 / GATE_TAU)
    o_gf[...] = gate[:, 0:256]
    o_gb[...] = gate[:, 256:512]

    if rope:
        row0 = (pl.program_id(0) % tiles_per_seq) * (tm // GRID_W)
        cos_m, sin_m = _tile_tables(tab_m, row0, tm // GRID_W)
        cos_d, sin_d = _tile_tables(tab_d, row0, tm // GRID_W)

    qn = _rms(y[:, C_QD:C_QD + Q_RANK], gqa_ref[...])
    q8 = _dot(qn.astype(BF16), wuq_ref[...])
    ckv = _rms(y[:, C_KVD:C_KVD + KV_RANK], gkva_ref[...])
    kn = _dot(ckv.astype(BF16), wuk_ref[...])
    o_mv[...] = _dot(ckv.astype(BF16), wuv_ref[...]).astype(BF16)
    krp = y[:, C_KRP:C_KRP + LANES]
    for hd in range(H_B):
        sl = slice(hd * HEAD_PAD, (hd + 1) * HEAD_PAD)
        qt = q8[:, sl]
        qt = qt * lax.rsqrt(jnp.sum(qt * qt, axis=-1, keepdims=True) * (1.0 / QK_B) + EPS) * gq_ref[...]
        kt = kn[:, sl] + krp
        kt = kt * lax.rsqrt(jnp.sum(kt * kt, axis=-1, keepdims=True) * (1.0 / QK_B) + EPS) * gk_ref[...]
        if rope:
            qt = _rotate(qt, cos_m, sin_m, ROPE_B // 4)
            kt = _rotate(kt, cos_m, sin_m, ROPE_B // 4)
        o_mq[:, sl] = (qt * (QK_B ** -0.5)).astype(BF16)
        o_mk[:, sl] = kt.astype(BF16)

    lane = _lane_iota((tm, LANES))
    lo = lane < D_C
    for hd in range(H_C):
        sl = slice(hd * LANES, (hd + 1) * LANES)
        for src, g_ref, dst, dst32, scale in (
            (C_DQ, gdq_ref, o_dq, None, D_C ** -0.5),
            (C_DK, gdk_ref, o_dk, o_dk32 if emit_state else None, 1.0),
        ):
            t = y[:, src + hd * LANES:src + (hd + 1) * LANES]
            sq = t * t
            s_lo = jnp.sum(jnp.where(lo, sq, 0.0), axis=-1, keepdims=True)
            s_hi = jnp.sum(jnp.where(lo, 0.0, sq), axis=-1, keepdims=True)
            inv = jnp.where(lo, lax.rsqrt(s_lo * (1.0 / D_C) + EPS), lax.rsqrt(s_hi * (1.0 / D_C) + EPS))
            t = t * inv * g_ref[...]
            if rope:
                t = _rotate(t, cos_d, sin_d, D_C // 4)
            if dst32 is not None:
                dst32[:, sl] = t
            dst[:, sl] = (t * scale).astype(BF16)
    dv = y[:, C_DV:C_DV + W_C]
    o_dv[...] = dv.astype(BF16)
    if emit_state:
        o_dv32[...] = dv
        o_ckv[...] = ckv
        o_misc[...] = misc


def _inproj(x2, mod8, group0, lw, tabs, *, rope, emit_state):
    t = x2.shape[0]
    tm = TOKEN_TILE
    tiles_per_seq = LAT_SEQ // tm
    const = lambda shape: pl.BlockSpec(shape, lambda i: (0,) * len(shape))
    row = lambda w: pl.BlockSpec((tm, w), lambda i: (i, 0))
    if rope:
        mod_spec = pl.BlockSpec((1, 8, D_MODEL), lambda i: (group0 + i // tiles_per_seq, 0, 0))
    else:
        mod_spec = pl.BlockSpec((1, 8, D_MODEL), lambda i: (group0, 0, 0))
    in_specs = [
        row(D_MODEL), mod_spec, const((1, D_MODEL)), const((D_MODEL, IN_COLS_A)),
        const((LANES, 512)), const((1, 512)),
        const((1, Q_RANK)), const((1, KV_RANK)), const((Q_RANK, H_B * HEAD_PAD)),
        const((KV_RANK, H_B * HEAD_PAD)), const((KV_RANK, W_B)), const((1, LANES)), const((1, LANES)),
        const((1, LANES)), const((1, LANES)),
    ]
    args = [x2, mod8, lw["g1"], lw["w_in_a"], lw["wa2"], lw["ba"], lw["gqa"], lw["gkva"], lw["wuq"],
            lw["wuk"], lw["wuv"], lw["gq"], lw["gk"], lw["gdq"], lw["gdk"]]
    if rope:
        in_specs += [const((GRID_W, LANES))] * 8
        args += list(tabs[0]) + list(tabs[1])
    outs = [(256, F32), (256, F32), (512, BF16), (512, F32), (256, F32), (256, F32),
            (H_B * HEAD_PAD, BF16), (H_B * HEAD_PAD, BF16), (W_B, BF16), (W_C, BF16), (W_C, BF16), (W_C, BF16)]
    if emit_state:
        outs += [(KV_RANK, F32), (LANES, F32), (W_C, F32), (W_C, F32)]
    return pl.pallas_call(
        functools.partial(_inproj_kernel, rope=rope, emit_state=emit_state, tiles_per_seq=tiles_per_seq),
        grid=(t // tm,),
        in_specs=in_specs,
        out_specs=[row(w) for w, _ in outs],
        out_shape=[jax.ShapeDtypeStruct((t, w), dt) for w, dt in outs],
        compiler_params=pltpu.CompilerParams(dimension_semantics=("arbitrary",), vmem_limit_bytes=VMEM_LIMIT),
        name="inproj_lat" if rope else "inproj_ctx",
    )(*args)


def _cache_kv_kernel(ckv_ref, krp_ref, wuk_ref, wuv_ref, gk_ref, k_ref, v_ref):
    c = ckv_ref[...].astype(BF16)
    kn = _dot(c, wuk_ref[...])
    v_ref[...] = _dot(c, wuv_ref[...]).astype(BF16)
    krp = krp_ref[...]
    for hd in range(H_B):
        sl = slice(hd * HEAD_PAD, (hd + 1) * HEAD_PAD)
        kt = kn[:, sl] + krp
        kt = kt * lax.rsqrt(jnp.sum(kt * kt, axis=-1, keepdims=True) * (1.0 / QK_B) + EPS) * gk_ref[...]
        k_ref[:, sl] = kt.astype(BF16)


def _cache_kv(ckv2, krp2, lw):
    t = ckv2.shape[0]
    tm = TOKEN_TILE
    const = lambda shape: pl.BlockSpec(shape, lambda i: (0,) * len(shape))
    row = lambda w: pl.BlockSpec((tm, w), lambda i: (i, 0))
    return pl.pallas_call(
        _cache_kv_kernel,
        grid=(t // tm,),
        in_specs=[row(KV_RANK), row(LANES), const((KV_RANK, H_B * HEAD_PAD)), const((KV_RANK, W_B)),
                  const((1, LANES))],
        out_specs=[row(H_B * HEAD_PAD), row(W_B)],
        out_shape=[jax.ShapeDtypeStruct((t, H_B * HEAD_PAD), BF16), jax.ShapeDtypeStruct((t, W_B), BF16)],
        name="cache_kv",
    )(ckv2, krp2, lw["wuk"], lw["wuv"], lw["gk"])


def _gla_constants(rev):
    c = CHUNK
    idx = np.arange(c)
    t, u = idx[:, None], idx[None, :]
    mats = [(u <= t)]
    masks = [np.eye(c, dtype=bool)]
    hs = c // 2
    while hs >= 1:
        blk = idx // (2 * hs)
        lower = (idx % (2 * hs)) < hs
        p = blk * 2 * hs + hs - 1
        m_low = lower[:, None] & (u > t) & (u <= p[:, None])
        m_up = (~lower)[:, None] & (u > p[:, None]) & (u <= t)
        mats.append(m_low | m_up)
        masks.append((blk[:, None] == blk[None, :]) & (~lower)[:, None] & lower[None, :])
        hs //= 2
    mats.append(u > t)
    mats = [m.astype(np.float32) for m in mats]
    masks = [m.astype(np.float32) for m in masks]
    if rev:
        mats = [m[::-1, ::-1] for m in mats]
        masks = [m[::-1, ::-1] for m in masks]
    cm = np.concatenate(mats, axis=0)
    cms = np.concatenate([cm, cm], axis=1)
    lm = np.stack([np.tile(m, (H_A, 1)) for m in masks])
    hm = (np.arange(H_A * c)[:, None] // c == np.arange(H_A * DK_A)[None, :] // DK_A).astype(np.float32)
    return jnp.asarray(cms, BF16), jnp.asarray(lm, F32), jnp.asarray(hm, F32)


N_LEVEL = 6


def _gla_kernel(q_ref, k_ref, v_ref, g_ref, s0_ref, cms_ref, lm_ref, hm_ref, o_ref, sf_ref, st_ref, *, rev):
    ti = pl.program_id(1)
    nt = pl.num_programs(1)

    @pl.when(ti == 0)
    def _():
        st_ref[...] = s0_ref[0]

    n_chunk = q_ref.shape[1] // CHUNK
    hm = hm_ref[...]
    order = range(n_chunk - 1, -1, -1) if rev else range(n_chunk)
    for ci in order:
        rows = slice(ci * CHUNK, (ci + 1) * CHUNK)
        q = q_ref[0, rows, :]
        k = k_ref[0, rows, :]
        v16 = v_ref[0, rows, :]
        g = g_ref[0, rows, :]
        g_hi = g.astype(BF16)
        g_lo = (g - g_hi.astype(F32)).astype(BF16)
        sums = _dot(cms_ref[...], jnp.concatenate([g_hi, g_lo], axis=0))
        b = sums[0:CHUNK]
        qm = jnp.concatenate([q] * H_A, axis=0) * hm
        att = lm_ref[0] * _dot_nt(qm.astype(BF16), k.astype(BF16))
        for lv in range(1, N_LEVEL + 1):
            e = jnp.exp(sums[lv * CHUNK:(lv + 1) * CHUNK])
            qs = (qm * jnp.concatenate([e] * H_A, axis=0)).astype(BF16)
            ks = (k * e).astype(BF16)
            att = att + lm_ref[lv] * _dot_nt(qs, ks)
        qe = (qm * jnp.concatenate([jnp.exp(b)] * H_A, axis=0)).astype(BF16)
        st = st_ref[...]
        st16 = st.astype(BF16)
        outs = []
        for hd in range(H_A):
            hr = slice(hd * CHUNK, (hd + 1) * CHUNK)
            vr = slice(hd * DV_A, (hd + 1) * DV_A)
            intra = _dot(att[hr].astype(BF16), v16[:, vr])
            inter = _dot_nt(qe[hr], st16[vr])
            outs.append(intra + inter)
        o_ref[0, rows, :] = jnp.concatenate(outs, axis=1)
        ke = (k * jnp.exp(sums[(N_LEVEL + 1) * CHUNK:(N_LEVEL + 2) * CHUNK])).astype(BF16)
        upd = pl.dot(v16, ke, trans_a=True)
        b_end = b[0:1] if rev else b[CHUNK - 1:CHUNK]
        st_ref[...] = st * jnp.exp(b_end) + upd

    @pl.when(ti == nt - 1)
    def _():
        sf_ref[0] = st_ref[...]


def _gla_scan(q3, k3, v3, g3, s0, consts, *, rev):
    bsz, s, _ = q3.shape
    tt = GLA_TILE
    nt = s // tt
    tmap = (lambda b, t: (b, nt - 1 - t, 0)) if rev else (lambda b, t: (b, t, 0))
    bmap = lambda b, t: (b, 0, 0)
    const = lambda shape: pl.BlockSpec(shape, lambda b, t: (0,) * len(shape))
    cms, lm, hm = consts
    return pl.pallas_call(
        functools.partial(_gla_kernel, rev=rev),
        grid=(bsz, nt),
        in_specs=[
            pl.BlockSpec((1, tt, 256), tmap), pl.BlockSpec((1, tt, 256), tmap),
            pl.BlockSpec((1, tt, 512), tmap), pl.BlockSpec((1, tt, 256), tmap),
            pl.BlockSpec((1, 512, 256), bmap),
            const(cms.shape), const(lm.shape), const(hm.shape),
        ],
        out_specs=[pl.BlockSpec((1, tt, 512), tmap), pl.BlockSpec((1, 512, 256), bmap)],
        out_shape=[jax.ShapeDtypeStruct((bsz, s, W_A), F32), jax.ShapeDtypeStruct((bsz, 512, 256), F32)],
        scratch_shapes=[pltpu.VMEM((512, 256), F32)],
        compiler_params=pltpu.CompilerParams(dimension_semantics=("arbitrary", "arbitrary"),
                                             vmem_limit_bytes=VMEM_LIMIT),
        name="gla_bwd" if rev else "gla_fwd",
    )(q3, k3, v3, g3, s0, cms, lm, hm)


def _state_to_blocks(s0):
    st = jnp.swapaxes(s0, -1, -2)
    eye = jnp.eye(H_A, dtype=s0.dtype)
    full = st[:, :, :, None, :] * eye[None, :, None, :, None]
    return full.reshape(s0.shape[0], H_A * DV_A, H_A * DK_A)


def _blocks_to_state(sf):
    bsz = sf.shape[0]
    full = sf.reshape(bsz, H_A, DV_A, H_A, DK_A)
    diag = jnp.stack([full[:, h, :, h, :] for h in range(H_A)], axis=1)
    return jnp.swapaxes(diag, -1, -2)


def _online_step(carry, q, kt, vt):
    m, l, acc = carry
    s = _dot_nt(q, kt)
    m_new = jnp.maximum(m, jnp.max(s, axis=-1, keepdims=True))
    a = jnp.exp(m - m_new)
    p = jnp.exp(s - m_new)
    l = a * l + jnp.sum(p, axis=-1, keepdims=True)
    acc = a * acc + _dot(p.astype(BF16), vt)
    return m_new, l, acc


def _attend(q, k_ref, v_ref, ksl, cache):
    tq = q.shape[0]
    s_len = k_ref.shape[1]
    carry = (jnp.full((tq, 1), NEG_BIG, F32), jnp.zeros((tq, 1), F32), jnp.zeros((tq, LANES), F32))
    if cache is not None:
        carry = _online_step(carry, q, cache[0], cache[1])
    tk = min(KEY_TILE, s_len)
    if s_len == tk:
        carry = _online_step(carry, q, k_ref[0, :, ksl], v_ref[0])
    else:
        def body(j, c):
            rows = pl.ds(pl.multiple_of(j * tk, tk), tk)
            return _online_step(c, q, k_ref[0, rows, ksl], v_ref[0, rows, :])
        carry = lax.fori_loop(0, s_len // tk, body, carry)
    return carry[2], carry[1]


def _mla_kernel(*refs, has_cache):
    if has_cache:
        q_ref, k_ref, v_ref, kc_ref, vc_ref, o_ref = refs
    else:
        q_ref, k_ref, v_ref, o_ref = refs
    outs = []
    for hh in range(2):
        sl = slice(hh * HEAD_PAD, (hh + 1) * HEAD_PAD)
        cache = (kc_ref[0, :, sl], vc_ref[0]) if has_cache else None
        acc, l = _attend(q_ref[0, :, sl], k_ref, v_ref, sl, cache)
        outs.append(acc / l)
    lane = _lane_iota(outs[0].shape)
    o_ref[0] = jnp.where(lane < V_B, outs[0], outs[1]).astype(o_ref.dtype)


def _mla_attention(q3, k3, v3, cache):
    bsz, s, _ = q3.shape
    tq = Q_TILE
    pair = 2 * HEAD_PAD
    in_specs = [
        pl.BlockSpec((1, tq, pair), lambda b, h, i: (b, i, h)),
        pl.BlockSpec((1, s, pair), lambda b, h, i: (b, 0, h)),
        pl.BlockSpec((1, s, LANES), lambda b, h, i: (b, 0, h)),
    ]
    args = [q3, k3, v3]
    if cache is not None:
        p = cache[0].shape[1]
        in_specs += [pl.BlockSpec((1, p, pair), lambda b, h, i: (b, 0, h)),
                     pl.BlockSpec((1, p, LANES), lambda b, h, i: (b, 0, h))]
        args += list(cache)
    return pl.pallas_call(
        functools.partial(_mla_kernel, has_cache=cache is not None),
        grid=(bsz, H_B // 2, s // tq),
        in_specs=in_specs,
        out_specs=pl.BlockSpec((1, tq, LANES), lambda b, h, i: (b, i, h)),
        out_shape=jax.ShapeDtypeStruct((bsz, s, W_B), BF16),
        compiler_params=pltpu.CompilerParams(dimension_semantics=("arbitrary",) * 3,
                                             vmem_limit_bytes=VMEM_LIMIT),
        name="mla_lat" if cache is not None else "mla_ctx",
    )(*args)


def _diff_kernel(*refs, has_cache, lam_init):
    if has_cache:
        lam_ref, q_ref, k_ref, v_ref, kc_ref, vc_ref, o_ref = refs
    else:
        lam_ref, q_ref, k_ref, v_ref, o_ref = refs
    lq = lam_ref[...]
    lam = (jnp.exp(jnp.sum(lq[0:1] * lq[1:2], axis=-1, keepdims=True))
           - jnp.exp(jnp.sum(lq[2:3] * lq[3:4], axis=-1, keepdims=True)) + lam_init)
    q = q_ref[0].astype(F32)
    lane = _lane_iota(q.shape)
    cache = (kc_ref[0].astype(BF16), vc_ref[0].astype(BF16)) if has_cache else None
    full = slice(0, LANES)
    res = []
    for comp in range(2):
        keep = (lane < D_C) if comp == 0 else (lane >= D_C)
        qc = jnp.where(keep, q, 0.0).astype(BF16)
        acc, l = _attend(qc, k_ref, v_ref, full, cache)
        res.append(acc / l)
    o_ref[0] = res[0] - lam * res[1]


def _diff_attention(lam_qk, q3, k3, v3, cache, lam_init):
    bsz, s, _ = q3.shape
    tq = Q_TILE
    in_specs = [
        pl.BlockSpec((4, D_C), lambda b, h, i: (0, 0)),
        pl.BlockSpec((1, tq, LANES), lambda b, h, i: (b, i, h)),
        pl.BlockSpec((1, s, LANES), lambda b, h, i: (b, 0, h)),
        pl.BlockSpec((1, s, LANES), lambda b, h, i: (b, 0, h)),
    ]
    args = [lam_qk, q3, k3, v3]
    if cache is not None:
        p = cache[0].shape[1]
        in_specs += [pl.BlockSpec((1, p, LANES), lambda b, h, i: (b, 0, h))] * 2
        args += list(cache)
    return pl.pallas_call(
        functools.partial(_diff_kernel, has_cache=cache is not None, lam_init=lam_init),
        grid=(bsz, H_C, s // tq),
        in_specs=in_specs,
        out_specs=pl.BlockSpec((1, tq, LANES), lambda b, h, i: (b, i, h)),
        out_shape=jax.ShapeDtypeStruct((bsz, s, W_C), F32),
        compiler_params=pltpu.CompilerParams(dimension_semantics=("arbitrary",) * 3,
                                             vmem_limit_bytes=VMEM_LIMIT),
        name="diff_lat" if cache is not None else "diff_ctx",
    )(*args)


def _merge_kernel(x_ref, mod_ref, g1_ref, wg_ref, of_ref, ob_ref, sr_ref, yb_ref, yc_ref, ggla_ref, gsub_ref,
                  woa_ref, wob_ref, woc_ref, wout_ref, o_ref, *, lam_init):
    x = x_ref[...]
    m = mod_ref[0]
    h = _rms(x, g1_ref[...]) * (1.0 + m[1:2]) + m[0:1]
    gates = _sigmoid(_dot(h.astype(BF16), wg_ref[...]))
    o = of_ref[...] + ob_ref[...]
    sr = sr_ref[...]
    yc = yc_ref[...]
    ya_t, yc_t = [], []
    for hd in range(H_A):
        sl = slice(hd * LANES, (hd + 1) * LANES)
        ya_t.append(_rms(o[:, sl], ggla_ref[...]) * sr[:, sl])
        yc_t.append(_rms(yc[:, sl], gsub_ref[...]) * (1.0 - lam_init))
    ya = jnp.concatenate(ya_t, axis=1).astype(BF16)
    ycn = jnp.concatenate(yc_t, axis=1).astype(BF16)
    merged = (gates[:, 0:D_MODEL] * _dot(ya, woa_ref[...])
              + gates[:, D_MODEL:2 * D_MODEL] * _dot(yb_ref[...], wob_ref[...])
              + gates[:, 2 * D_MODEL:3 * D_MODEL] * _dot(ycn, woc_ref[...]))
    o_ref[...] = x + m[2:3] * _dot(merged.astype(BF16), wout_ref[...])


def _merge(x2, mod8, group0, lw, of2, ob2, sr2, yb2, yc2, lam_init, per_seq_groups):
    t = x2.shape[0]
    tm = TOKEN_TILE
    tiles_per_seq = LAT_SEQ // tm
    const = lambda shape: pl.BlockSpec(shape, lambda i: (0,) * len(shape))
    row = lambda w: pl.BlockSpec((tm, w), lambda i: (i, 0))
    if per_seq_groups:
        mod_spec = pl.BlockSpec((1, 8, D_MODEL), lambda i: (group0 + i // tiles_per_seq, 0, 0))
    else:
        mod_spec = pl.BlockSpec((1, 8, D_MODEL), lambda i: (group0, 0, 0))
    return pl.pallas_call(
        functools.partial(_merge_kernel, lam_init=lam_init),
        grid=(t // tm,),
        in_specs=[row(D_MODEL), mod_spec, const((1, D_MODEL)), const((D_MODEL, GATE_COLS)),
                  row(W_A), row(W_A), row(W_A), row(W_B), row(W_C), const((1, LANES)), const((1, LANES)),
                  const((W_A, D_MODEL)), const((W_B, D_MODEL)), const((W_C, D_MODEL)),
                  const((D_MODEL, D_MODEL))],
        out_specs=row(D_MODEL),
        out_shape=jax.ShapeDtypeStruct((t, D_MODEL), F32),
        compiler_params=pltpu.CompilerParams(dimension_semantics=("arbitrary",), vmem_limit_bytes=VMEM_LIMIT),
        name="merge",
    )(x2, mod8, lw["g1"], lw["w_gates"], of2, ob2, sr2, yb2, yc2, lw["ggla"], lw["gsub"],
      lw["woa"], lw["wob"], lw["woc"], lw["wout"])


def _mlp_kernel(x_ref, mod_ref, g2_ref, w1_ref, w2_ref, o_ref):
    x = x_ref[...]
    m = mod_ref[0]
    h = _rms(x, g2_ref[...]) * (1.0 + m[4:5]) + m[3:4]
    a = jnp.maximum(_dot(h.astype(BF16), w1_ref[...]), 0.0)
    o_ref[...] = x + m[5:6] * _dot((a * a).astype(BF16), w2_ref[...])


def _mlp(x2, mod8, group0, lw, per_seq_groups):
    t = x2.shape[0]
    tm = TOKEN_TILE
    tiles_per_seq = LAT_SEQ // tm
    const = lambda shape: pl.BlockSpec(shape, lambda i: (0,) * len(shape))
    row = lambda w: pl.BlockSpec((tm, w), lambda i: (i, 0))
    if per_seq_groups:
        mod_spec = pl.BlockSpec((1, 8, D_MODEL), lambda i: (group0 + i // tiles_per_seq, 0, 0))
    else:
        mod_spec = pl.BlockSpec((1, 8, D_MODEL), lambda i: (group0, 0, 0))
    return pl.pallas_call(
        _mlp_kernel,
        grid=(t // tm,),
        in_specs=[row(D_MODEL), mod_spec, const((1, D_MODEL)), const((D_MODEL, D_FF)), const((D_FF, D_MODEL))],
        out_specs=row(D_MODEL),
        out_shape=jax.ShapeDtypeStruct((t, D_MODEL), F32),
        compiler_params=pltpu.CompilerParams(dimension_semantics=("arbitrary",), vmem_limit_bytes=VMEM_LIMIT),
        name="mlp",
    )(x2, mod8, lw["g2"], lw["w1"], lw["w2"])


def _pad_heads(w, real, pad):
    k = w.shape[0]
    w = w.reshape(k, -1, real)
    return jnp.pad(w, ((0, 0), (0, 0), (0, pad - real))).reshape(k, -1)


def _layer_weights(l, p):
    widths = (H_A * DK_A, H_A * DK_A, W_A, W_A, 2 * GATE_RANK, Q_RANK, KV_RANK, ROPE_B, W_C, W_C, W_C, GATE_COLS)
    offs = np.concatenate([[0], np.cumsum(widths)])
    w_in = p["w_in"][l]
    aq, ak, av, ar, aa, qd, kvd, kr, dq, dk, dv, gates = (w_in[:, offs[i]:offs[i + 1]] for i in range(12))
    z = lambda n: jnp.zeros((D_MODEL, n), F32)
    misc = jnp.concatenate([aa, kr, z(LANES - 2 * GATE_RANK - ROPE_B)], axis=1)
    krp = jnp.concatenate([z(NOPE_B), kr, z(LANES - QK_B)], axis=1)
    w_in_a = jnp.concatenate([aq, ak, av, ar, qd, kvd, dq, dk, dv, misc, krp], axis=1).astype(BF16)
    wa2 = jnp.zeros((LANES, 512), F32)
    wa2 = wa2.at[0:GATE_RANK, 0:256].set(p["w_gla_a2"][l, 0])
    wa2 = wa2.at[GATE_RANK:2 * GATE_RANK, 256:512].set(p["w_gla_a2"][l, 1])
    pad_gain = lambda g: jnp.pad(g, (0, HEAD_PAD - QK_B)).reshape(1, HEAD_PAD)
    twice = lambda g: jnp.concatenate([g, g]).reshape(1, LANES)
    return dict(
        g1=p["g_norm1"][l].reshape(1, D_MODEL), g2=p["g_norm2"][l].reshape(1, D_MODEL),
        w_in_a=w_in_a, w_gates=gates.astype(BF16),
        wa2=wa2.astype(BF16), ba=p["b_gla_a"][l].reshape(1, 512),
        gqa=p["g_mla_qa"][l].reshape(1, Q_RANK), gkva=p["g_mla_kva"][l].reshape(1, KV_RANK),
        wuq=_pad_heads(p["w_mla_uq"][l], QK_B, HEAD_PAD).astype(BF16),
        wuk=_pad_heads(p["w_mla_uk"][l], NOPE_B, HEAD_PAD).astype(BF16),
        wuv=p["w_mla_uv"][l].astype(BF16),
        gq=pad_gain(p["g_mla_q"][l]), gk=pad_gain(p["g_mla_k"][l]),
        gdq=twice(p["g_diff_q"][l]), gdk=twice(p["g_diff_k"][l]),
        ggla=p["g_gla_out"][l].reshape(1, DV_A), gsub=p["g_diff_sub"][l].reshape(1, 2 * D_C),
        woa=p["w_o_gla"][l].astype(BF16), wob=p["w_o_mla"][l].astype(BF16), woc=p["w_o_diff"][l].astype(BF16),
        wout=p["w_out"][l].astype(BF16), w1=p["w_mlp1"][l].astype(BF16), w2=p["w_mlp2"][l].astype(BF16),
        lam_qk=p["lam_qk"][l],
    )


def _layer(x2, bsz, seq, mod8, group0, lw, lam_init, consts, tabs, cached):
    is_latent = cached is not None
    outs = _inproj(x2, mod8, group0, lw, tabs, rope=is_latent, emit_state=not is_latent)
    gq, gk, gv, sr, gf, gb, mq, mk, mv, dq, dk, dv = outs[:12]
    r3 = lambda a: a.reshape(bsz, seq, a.shape[-1])

    if is_latent:
        s0, ckv_c, krope_c, dk_c, dv_c = cached
        s0f, s0b = _state_to_blocks(s0[:, 0]), _state_to_blocks(s0[:, 1])
    else:
        s0f = s0b = jnp.zeros((bsz, H_A * DV_A, H_A * DK_A), F32)
    of, sf = _gla_scan(r3(gq), r3(gk), r3(gv), r3(gf), s0f, consts[0], rev=False)
    ob, sb = _gla_scan(r3(gq), r3(gk), r3(gv), r3(gb), s0b, consts[1], rev=True)

    if is_latent:
        past = ckv_c.shape[1]
        krp_c = jnp.pad(krope_c, ((0, 0), (0, 0), (NOPE_B, LANES - QK_B)))
        kc, vc = _cache_kv(ckv_c.reshape(bsz * past, KV_RANK), krp_c.reshape(bsz * past, LANES), lw)
        mla_cache = (kc.reshape(bsz, past, -1), vc.reshape(bsz, past, -1))
        diff_cache = (dk_c.reshape(bsz, past, W_C), dv_c.reshape(bsz, past, W_C))
    else:
        mla_cache = diff_cache = None
    yb = _mla_attention(r3(mq), r3(mk), r3(mv), mla_cache)
    yc = _diff_attention(lw["lam_qk"], r3(dq), r3(dk), r3(dv), diff_cache, lam_init)

    x1 = _merge(x2, mod8, group0, lw, of.reshape(-1, W_A), ob.reshape(-1, W_A), sr, yb.reshape(-1, W_B),
                yc.reshape(-1, W_C), lam_init, is_latent)
    x_out = _mlp(x1, mod8, group0, lw, is_latent)

    new_state = None
    if not is_latent:
        ckv, misc, dk32, dv32 = outs[12:]
        gla_state = jnp.stack([_blocks_to_state(sf), _blocks_to_state(sb)], axis=1)
        new_state = (gla_state, ckv.reshape(bsz, seq, KV_RANK),
                     misc[:, 2 * GATE_RANK:2 * GATE_RANK + ROPE_B].reshape(bsz, seq, ROPE_B),
                     dk32.reshape(bsz, seq, H_C, 2, D_C), dv32.reshape(bsz, seq, H_C, 2 * D_C))
    return x_out, new_state


def kernel(x_prompt, x_sample, state_gla, cache_mla_ckv, cache_mla_krope, cache_diff_k, cache_diff_v, c, c_ctx, w_mod, b_mod, g_norm1, g_norm2, w_in, w_gla_a2, b_gla_a, g_gla_out, g_mla_qa, g_mla_kva, w_mla_uq, w_mla_uk, w_mla_uv, g_mla_q, g_mla_k, g_diff_q, g_diff_k, lam_qk, g_diff_sub, w_o_gla, w_o_mla, w_o_diff, w_out, w_mlp1, w_mlp2):
    params = dict(w_in=w_in, g_norm1=g_norm1, g_norm2=g_norm2, w_gla_a2=w_gla_a2, b_gla_a=b_gla_a,
                  g_gla_out=g_gla_out, g_mla_qa=g_mla_qa, g_mla_kva=g_mla_kva, w_mla_uq=w_mla_uq,
                  w_mla_uk=w_mla_uk, w_mla_uv=w_mla_uv, g_mla_q=g_mla_q, g_mla_k=g_mla_k, g_diff_q=g_diff_q,
                  g_diff_k=g_diff_k, lam_qk=lam_qk, g_diff_sub=g_diff_sub, w_o_gla=w_o_gla, w_o_mla=w_o_mla,
                  w_o_diff=w_o_diff, w_out=w_out, w_mlp1=w_mlp1, w_mlp2=w_mlp2)
    nb, ns, _ = x_prompt.shape
    db, ds, _ = x_sample.shape
    assert ds == LAT_SEQ and db + 1 <= 8

    cond8 = jnp.concatenate([c_ctx[None, :], c, jnp.zeros((8 - 1 - db, D_MODEL), F32)], axis=0)
    mod = _modulation(cond8, w_mod, b_mod).reshape(DEPTH, 8, 6, D_MODEL)
    mod = jnp.pad(mod, ((0, 0), (0, 0), (0, 2), (0, 0)))

    consts = (_gla_constants(False), _gla_constants(True))
    tabs = (_rope_tables(ROPE_B, NOPE_B, LANES), _rope_tables(D_C, 0, D_C))

    yp = x_prompt.reshape(nb * ns, D_MODEL)
    ys = x_sample.reshape(db * ds, D_MODEL)
    states = []
    for l in range(DEPTH):
        lw = _layer_weights(l, params)
        lam_init = 0.8 - 0.6 * math.exp(-0.3 * l)
        yp, st = _layer(yp, nb, ns, mod[l], 0, lw, lam_init, consts, tabs, None)
        states.append(st)
        cached = (state_gla[:, l], cache_mla_ckv[:, l], cache_mla_krope[:, l], cache_diff_k[:, l],
                  cache_diff_v[:, l])
        ys, _ = _layer(ys, db, ds, mod[l], 1, lw, lam_init, consts, tabs, cached)
    stack = lambda i: jnp.stack([s[i] for s in states], axis=1)
    return (yp.reshape(nb, ns, D_MODEL), ys.reshape(db, ds, D_MODEL), stack(0), stack(1), stack(2), stack(3),
            stack(4))
```

```python
import functools
import math

import numpy as np
import jax
import jax.numpy as jnp
from jax import lax
from jax.experimental import pallas as pl
from jax.experimental.pallas import tpu as pltpu

F32 = jnp.float32
BF16 = jnp.bfloat16

D_MODEL = 1024
DEPTH = 2
GRID_W = 64
ROPE_BASE = 10000.0
EPS = 1e-6
H_A, DK_A, DV_A = 4, 64, 128
GATE_RANK = 16
GATE_TAU = 16.0
CHUNK = 64
W_A = H_A * DV_A
H_B, Q_RANK, KV_RANK = 8, 384, 256
NOPE_B, ROPE_B, V_B = 64, 32, 64
QK_B = NOPE_B + ROPE_B
W_B = H_B * V_B
H_C, D_C = 4, 64
W_C = H_C * 2 * D_C
D_FF = 4 * D_MODEL
GATE_COLS = 3 * D_MODEL
LAT_SEQ = 4096

LANES = 128
HEAD_PAD = LANES
MXU_WIDTH = 256
VMEM_LIMIT = 56 * 1024 * 1024

TOKEN_TILE = 256
Q_TILE = 256
MLA_ROUNDS = 2
DIFF_ROUNDS = 2
GLA_TILE = 256

C_AQ, C_AK, C_AV, C_AR = 0, 256, 512, 1024
C_QD, C_KVD = 1536, 1920
C_DQ, C_DK, C_DV = 2176, 2688, 3200
C_MISC, C_KRP = 3712, 3840
IN_COLS_A = 3968

NEG_BIG = -1e30
LOG2E = math.log2(math.e)


def _dot(a, b):
    return jnp.dot(a, b, preferred_element_type=F32)


def _dot_nt(a, b):
    return lax.dot_general(a, b, (((1,), (1,)), ((), ())), preferred_element_type=F32)


def _sigmoid(x):
    return 1.0 / (1.0 + jnp.exp(-x))


def _rms(x, g):
    ms = jnp.mean(x * x, axis=-1, keepdims=True)
    return x * lax.rsqrt(ms + EPS) * g


def _lane_iota(shape):
    return lax.broadcasted_iota(jnp.int32, shape, len(shape) - 1)


def _ones_lane0(shape):
    return jnp.where(_lane_iota(shape) % LANES == 0, 1.0, 0.0).astype(F32)


def _mod_kernel(c_ref, w_ref, b_ref, o_ref):
    c = c_ref[...]
    s = c * _sigmoid(c)
    o_ref[0] = _dot(s.astype(BF16), w_ref[0].astype(BF16)) + b_ref[0]


def _modulation(cond8, w_mod, b_mod):
    nt = 768
    return pl.pallas_call(
        _mod_kernel,
        grid=(DEPTH, 6 * D_MODEL // nt),
        in_specs=[
            pl.BlockSpec((8, D_MODEL), lambda l, j: (0, 0)),
            pl.BlockSpec((1, D_MODEL, nt), lambda l, j: (l, 0, j)),
            pl.BlockSpec((1, 1, nt), lambda l, j: (l, 0, j)),
        ],
        out_specs=pl.BlockSpec((1, 8, nt), lambda l, j: (l, 0, j)),
        out_shape=jax.ShapeDtypeStruct((DEPTH, 8, 6 * D_MODEL), F32),
        name="modulation",
    )(cond8, w_mod, b_mod.reshape(DEPTH, 1, 6 * D_MODEL))


def _rope_tables(n_rope, lane0, width):
    half = n_rope // 2
    nf = half // 2
    inv = ROPE_BASE ** (-np.arange(nf, dtype=np.float64) / nf)
    pos = np.arange(GRID_W, dtype=np.float64)[:, None]
    cr = np.zeros((GRID_W, LANES)); cc = np.ones((GRID_W, LANES))
    sr = np.zeros((GRID_W, LANES)); sc = np.zeros((GRID_W, LANES))
    for base in range(0, LANES, width):
        for d in range(n_rope):
            lane = base + lane0 + d
            within = d % half
            ang = pos[:, 0] * inv[within % nf]
            sign = -1.0 if within < nf else 1.0
            if d < half:
                cr[:, lane] = np.cos(ang); cc[:, lane] = 0.0
                sr[:, lane] = sign * np.sin(ang)
            else:
                cc[:, lane] = np.cos(ang)
                sc[:, lane] = sign * np.sin(ang)
    return tuple(jnp.asarray(t, F32) for t in (cr, cc, sr, sc))


def _tile_tables(tabs, row0, n_sub):
    cr_ref, cc_ref, sr_ref, sc_ref = tabs
    cs, ss = [], []
    for j in range(n_sub):
        r = (row0 + j) % GRID_W
        cs.append(cr_ref[pl.ds(r, 1), :] + cc_ref[...])
        ss.append(sr_ref[pl.ds(r, 1), :] + sc_ref[...])
    return jnp.concatenate(cs, axis=0), jnp.concatenate(ss, axis=0)


def _rotate(x, cos_t, sin_t, nf):
    lane = _lane_iota(x.shape)
    up = pltpu.roll(x, LANES - nf, 1)
    dn = pltpu.roll(x, nf, 1)
    partner = jnp.where((lane & nf) == 0, up, dn)
    return x * cos_t + partner * sin_t


def _inproj_kernel(*refs, rope, emit_state, tiles_per_seq):
    it = iter(refs)
    x_ref, mod_ref, g1_ref, w_ref, wa2_ref, ba_ref = (next(it) for _ in range(6))
    gqa_ref, gkva_ref, wuq_ref, wuk_ref, wuv_ref, gq_ref, gk_ref = (next(it) for _ in range(7))
    gdq_ref, gdk_ref = next(it), next(it)
    if rope:
        tab_m = tuple(next(it) for _ in range(4))
        tab_d = tuple(next(it) for _ in range(4))
    o_gq, o_gk, o_gv, o_sr, o_gf, o_gb = (next(it) for _ in range(6))
    o_mq, o_mk, o_mv, o_dq, o_dk, o_dv = (next(it) for _ in range(6))
    if emit_state:
        o_ckv, o_misc, o_dk32, o_dv32 = (next(it) for _ in range(4))

    tm = x_ref.shape[0]
    x = x_ref[...]
    m = mod_ref[0]
    h = _rms(x, g1_ref[...]) * (1.0 + m[1:2]) + m[0:1]
    y = _dot(h.astype(BF16), w_ref[...])

    o_gq[...] = y[:, C_AQ:C_AQ + 256] * (DK_A ** -0.5)
    o_gk[...] = y[:, C_AK:C_AK + 256]
    o_gv[...] = y[:, C_AV:C_AV + 512].astype(BF16)
    r = y[:, C_AR:C_AR + 512]
    o_sr[...] = r * _sigmoid(r)
    misc = y[:, C_MISC:C_MISC + LANES]
    logits = _dot(misc.astype(BF16), wa2_ref[...]) + ba_ref[...]
    logsig = jnp.minimum(logits, 0.0) - jnp.log(1.0 + jnp.exp(-jnp.abs(logits)))
    gate = logsig * (1.0 / GATE_TAU)
    o_gf[...] = gate[:, 0:256]
    o_gb[...] = gate[:, 256:512]

    if rope:
        row0 = (pl.program_id(0) % tiles_per_seq) * (tm // GRID_W)
        cos_m, sin_m = _tile_tables(tab_m, row0, tm // GRID_W)
        cos_d, sin_d = _tile_tables(tab_d, row0, tm // GRID_W)

    qn = _rms(y[:, C_QD:C_QD + Q_RANK], gqa_ref[...])
    q8 = _dot(qn.astype(BF16), wuq_ref[...])
    ckv = _rms(y[:, C_KVD:C_KVD + KV_RANK], gkva_ref[...])
    kn = _dot(ckv.astype(BF16), wuk_ref[...])
    vb = _dot(ckv.astype(BF16), wuv_ref[...])
    o_mv[0] = jnp.where(_lane_iota(vb.shape) % HEAD_PAD == V_B, 1.0, vb).astype(BF16)
    krp = y[:, C_KRP:C_KRP + LANES]
    for hd in range(H_B):
        sl = slice(hd * HEAD_PAD, (hd + 1) * HEAD_PAD)
        qt = q8[:, sl]
        qt = qt * lax.rsqrt(jnp.sum(qt * qt, axis=-1, keepdims=True) * (1.0 / QK_B) + EPS) * gq_ref[...]
        kt = kn[:, sl] + krp
        kt = kt * lax.rsqrt(jnp.sum(kt * kt, axis=-1, keepdims=True) * (1.0 / QK_B) + EPS) * gk_ref[...]
        if rope:
            qt = _rotate(qt, cos_m, sin_m, ROPE_B // 4)
            kt = _rotate(kt, cos_m, sin_m, ROPE_B // 4)
        o_mq[:, sl] = (qt * (QK_B ** -0.5 * LOG2E)).astype(BF16)
        o_mk[0, :, sl] = kt.astype(BF16)

    lane = _lane_iota((tm, LANES))
    lo = lane < D_C
    for hd in range(H_C):
        sl = slice(hd * LANES, (hd + 1) * LANES)
        for src, g_ref, dst, dst32, scale in (
            (C_DQ, gdq_ref, o_dq, None, D_C ** -0.5 * LOG2E),
            (C_DK, gdk_ref, o_dk.at[0], o_dk32 if emit_state else None, 1.0),
        ):
            t = y[:, src + hd * LANES:src + (hd + 1) * LANES]
            sq = t * t
            s_lo = jnp.sum(jnp.where(lo, sq, 0.0), axis=-1, keepdims=True)
            s_hi = jnp.sum(jnp.where(lo, 0.0, sq), axis=-1, keepdims=True)
            inv = jnp.where(lo, lax.rsqrt(s_lo * (1.0 / D_C) + EPS), lax.rsqrt(s_hi * (1.0 / D_C) + EPS))
            t = t * inv * g_ref[...]
            if rope:
                t = _rotate(t, cos_d, sin_d, D_C // 4)
            if dst32 is not None:
                dst32[:, sl] = t
            dst[:, sl] = (t * scale).astype(BF16)
    dv = y[:, C_DV:C_DV + W_C]
    one_tile = _ones_lane0((tm, LANES))
    o_dv[0] = jnp.concatenate(
        [t for hd in range(H_C) for t in (dv[:, hd * LANES:(hd + 1) * LANES], one_tile)], axis=1).astype(BF16)
    if emit_state:
        o_dv32[...] = dv
        o_ckv[...] = ckv
        o_misc[...] = misc


def _inproj(x2, bsz, past, mod8, group0, lw, tabs, *, rope, emit_state):
    t = x2.shape[0]
    tm = TOKEN_TILE
    seq = t // bsz
    tiles_per_seq = LAT_SEQ // tm
    tps = seq // tm
    assert past % tm == 0
    kv_rows = lambda w: pl.BlockSpec((1, tm, w), lambda i: (i // tps, past // tm + i % tps, 0))
    const = lambda shape: pl.BlockSpec(shape, lambda i: (0,) * len(shape))
    row = lambda w: pl.BlockSpec((tm, w), lambda i: (i, 0))
    if rope:
        mod_spec = pl.BlockSpec((1, 8, D_MODEL), lambda i: (group0 + i // tiles_per_seq, 0, 0))
    else:
        mod_spec = pl.BlockSpec((1, 8, D_MODEL), lambda i: (group0, 0, 0))
    in_specs = [
        row(D_MODEL), mod_spec, const((1, D_MODEL)), const((D_MODEL, IN_COLS_A)),
        const((LANES, 512)), const((1, 512)),
        const((1, Q_RANK)), const((1, KV_RANK)), const((Q_RANK, H_B * HEAD_PAD)),
        const((KV_RANK, H_B * HEAD_PAD)), const((KV_RANK, H_B * HEAD_PAD)), const((1, LANES)), const((1, LANES)),
        const((1, LANES)), const((1, LANES)),
    ]
    args = [x2, mod8, lw["g1"], lw["w_in_a"], lw["wa2"], lw["ba"], lw["gqa"], lw["gkva"], lw["wuq"],
            lw["wuk"], lw["wuv"], lw["gq"], lw["gk"], lw["gdq"], lw["gdk"]]
    if rope:
        in_specs += [const((GRID_W, LANES))] * 8
        args += list(tabs[0]) + list(tabs[1])
    flat, keyed = "flat", "keyed"
    outs = [(256, F32, flat), (256, F32, flat), (512, BF16, flat), (512, F32, flat), (256, F32, flat),
            (256, F32, flat), (H_B * HEAD_PAD, BF16, flat), (H_B * HEAD_PAD, BF16, keyed),
            (H_B * HEAD_PAD, BF16, keyed), (W_C, BF16, flat), (W_C, BF16, keyed), (2 * W_C, BF16, keyed)]
    if emit_state:
        outs += [(KV_RANK, F32, flat), (LANES, F32, flat), (W_C, F32, flat), (W_C, F32, flat)]
    return pl.pallas_call(
        functools.partial(_inproj_kernel, rope=rope, emit_state=emit_state, tiles_per_seq=tiles_per_seq),
        grid=(t // tm,),
        in_specs=in_specs,
        out_specs=[row(w) if kind == flat else kv_rows(w) for w, _, kind in outs],
        out_shape=[jax.ShapeDtypeStruct((t, w) if kind == flat else (bsz, past + seq, w), dt)
                   for w, dt, kind in outs],
        compiler_params=pltpu.CompilerParams(dimension_semantics=("arbitrary",), vmem_limit_bytes=VMEM_LIMIT),
        name="inproj_lat" if rope else "inproj_ctx",
    )(*args)


def _cache_kv_kernel(ckv_ref, krp_ref, dkc_ref, dvc_ref, wuk_ref, wuv_ref, gk_ref, _mk, _mv, _dk, _dv,
                     mk_ref, mv_ref, dk_ref, dv_ref):
    c = ckv_ref[0].astype(BF16)
    kn = _dot(c, wuk_ref[...])
    vb = _dot(c, wuv_ref[...])
    mv_ref[0] = jnp.where(_lane_iota(vb.shape) % HEAD_PAD == V_B, 1.0, vb).astype(BF16)
    krp = krp_ref[0]
    for hd in range(H_B):
        sl = slice(hd * HEAD_PAD, (hd + 1) * HEAD_PAD)
        kt = kn[:, sl] + krp
        kt = kt * lax.rsqrt(jnp.sum(kt * kt, axis=-1, keepdims=True) * (1.0 / QK_B) + EPS) * gk_ref[...]
        mk_ref[0, :, sl] = kt.astype(BF16)
    dk_ref[0] = dkc_ref[0].astype(BF16)
    dvc = dvc_ref[0]
    one_tile = _ones_lane0((dvc.shape[0], LANES))
    dv_ref[0] = jnp.concatenate(
        [t for hd in range(H_C) for t in (dvc[:, hd * LANES:(hd + 1) * LANES], one_tile)], axis=1).astype(BF16)


def _cache_kv(ckv3, krp3, dkc3, dvc3, lw, bufs):
    bsz, past, _ = ckv3.shape
    const = lambda shape: pl.BlockSpec(shape, lambda b: (0,) * len(shape))
    head = lambda w: pl.BlockSpec((1, past, w), lambda b: (b, 0, 0))
    anyspec = pl.BlockSpec(memory_space=pl.ANY)
    return pl.pallas_call(
        _cache_kv_kernel,
        grid=(bsz,),
        in_specs=[head(KV_RANK), head(LANES), head(W_C), head(W_C), const((KV_RANK, H_B * HEAD_PAD)),
                  const((KV_RANK, H_B * HEAD_PAD)), const((1, LANES))] + [anyspec] * 4,
        out_specs=[head(b.shape[-1]) for b in bufs],
        out_shape=[jax.ShapeDtypeStruct(b.shape, b.dtype) for b in bufs],
        input_output_aliases={7: 0, 8: 1, 9: 2, 10: 3},
        name="cache_kv",
    )(ckv3, krp3, dkc3, dvc3, lw["wuk"], lw["wuv"], lw["gk"], *bufs)


def _gla_constants(rev):
    c = CHUNK
    idx = np.arange(c)
    t, u = idx[:, None], idx[None, :]
    mats = [(u <= t)]
    masks = [np.eye(c, dtype=bool)]
    hs = c // 2
    while hs >= 1:
        blk = idx // (2 * hs)
        lower = (idx % (2 * hs)) < hs
        p = blk * 2 * hs + hs - 1
        m_low = lower[:, None] & (u > t) & (u <= p[:, None])
        m_up = (~lower)[:, None] & (u > p[:, None]) & (u <= t)
        mats.append(m_low | m_up)
        masks.append((blk[:, None] == blk[None, :]) & (~lower)[:, None] & lower[None, :])
        hs //= 2
    mats.append(u > t)
    mats = [m.astype(np.float32) for m in mats]
    masks = [m.astype(np.float32) for m in masks]
    if rev:
        mats = [m[::-1, ::-1] for m in mats]
        masks = [m[::-1, ::-1] for m in masks]
    cm = np.concatenate(mats, axis=0)
    cms = np.concatenate([cm, cm], axis=1)
    lm = np.stack([np.tile(m, (H_A, 1)) for m in masks])
    hm = (np.arange(H_A * c)[:, None] // c == np.arange(H_A * DK_A)[None, :] // DK_A).astype(np.float32)
    return jnp.asarray(cms, BF16), jnp.asarray(lm, F32), jnp.asarray(hm, F32)


N_LEVEL = 6


def _gla_kernel(q_ref, k_ref, v_ref, g_ref, s0_ref, cms_ref, lm_ref, hm_ref, o_ref, sf_ref, st_ref, *, rev):
    ti = pl.program_id(1)
    nt = pl.num_programs(1)

    @pl.when(ti == 0)
    def _():
        st_ref[...] = s0_ref[0]

    n_chunk = q_ref.shape[1] // CHUNK
    hm = hm_ref[...]
    order = range(n_chunk - 1, -1, -1) if rev else range(n_chunk)
    for ci in order:
        rows = slice(ci * CHUNK, (ci + 1) * CHUNK)
        q = q_ref[0, rows, :]
        k = k_ref[0, rows, :]
        v16 = v_ref[0, rows, :]
        g = g_ref[0, rows, :]
        g_hi = g.astype(BF16)
        g_lo = (g - g_hi.astype(F32)).astype(BF16)
        sums = _dot(cms_ref[...], jnp.concatenate([g_hi, g_lo], axis=0))
        b = sums[0:CHUNK]
        qm = jnp.concatenate([q] * H_A, axis=0) * hm
        att = lm_ref[0] * _dot_nt(qm.astype(BF16), k.astype(BF16))
        for lv in range(1, N_LEVEL + 1):
            e = jnp.exp(sums[lv * CHUNK:(lv + 1) * CHUNK])
            qs = (qm * jnp.concatenate([e] * H_A, axis=0)).astype(BF16)
            ks = (k * e).astype(BF16)
            att = att + lm_ref[lv] * _dot_nt(qs, ks)
        qe = (qm * jnp.concatenate([jnp.exp(b)] * H_A, axis=0)).astype(BF16)
        st = st_ref[...]
        st16 = st.astype(BF16)
        outs = []
        for hd in range(H_A):
            hr = slice(hd * CHUNK, (hd + 1) * CHUNK)
            vr = slice(hd * DV_A, (hd + 1) * DV_A)
            intra = _dot(att[hr].astype(BF16), v16[:, vr])
            inter = _dot_nt(qe[hr], st16[vr])
            outs.append(intra + inter)
        o_ref[0, rows, :] = jnp.concatenate(outs, axis=1)
        ke = (k * jnp.exp(sums[(N_LEVEL + 1) * CHUNK:(N_LEVEL + 2) * CHUNK])).astype(BF16)
        upd = pl.dot(v16, ke, trans_a=True)
        b_end = b[0:1] if rev else b[CHUNK - 1:CHUNK]
        st_ref[...] = st * jnp.exp(b_end) + upd

    @pl.when(ti == nt - 1)
    def _():
        sf_ref[0] = st_ref[...]


def _gla_scan(q3, k3, v3, g3, s0, consts, *, rev):
    bsz, s, _ = q3.shape
    tt = GLA_TILE
    nt = s // tt
    tmap = (lambda b, t: (b, nt - 1 - t, 0)) if rev else (lambda b, t: (b, t, 0))
    bmap = lambda b, t: (b, 0, 0)
    const = lambda shape: pl.BlockSpec(shape, lambda b, t: (0,) * len(shape))
    cms, lm, hm = consts
    return pl.pallas_call(
        functools.partial(_gla_kernel, rev=rev),
        grid=(bsz, nt),
        in_specs=[
            pl.BlockSpec((1, tt, 256), tmap), pl.BlockSpec((1, tt, 256), tmap),
            pl.BlockSpec((1, tt, 512), tmap), pl.BlockSpec((1, tt, 256), tmap),
            pl.BlockSpec((1, 512, 256), bmap),
            const(cms.shape), const(lm.shape), const(hm.shape),
        ],
        out_specs=[pl.BlockSpec((1, tt, 512), tmap), pl.BlockSpec((1, 512, 256), bmap)],
        out_shape=[jax.ShapeDtypeStruct((bsz, s, W_A), F32), jax.ShapeDtypeStruct((bsz, 512, 256), F32)],
        scratch_shapes=[pltpu.VMEM((512, 256), F32)],
        compiler_params=pltpu.CompilerParams(dimension_semantics=("arbitrary", "arbitrary"),
                                             vmem_limit_bytes=VMEM_LIMIT),
        name="gla_bwd" if rev else "gla_fwd",
    )(q3, k3, v3, g3, s0, cms, lm, hm)


def _state_to_blocks(s0):
    st = jnp.swapaxes(s0, -1, -2)
    eye = jnp.eye(H_A, dtype=s0.dtype)
    full = st[:, :, :, None, :] * eye[None, :, None, :, None]
    return full.reshape(s0.shape[0], H_A * DV_A, H_A * DK_A)


def _blocks_to_state(sf):
    bsz = sf.shape[0]
    full = sf.reshape(bsz, H_A, DV_A, H_A, DK_A)
    diag = jnp.stack([full[:, h, :, h, :] for h in range(H_A)], axis=1)
    return jnp.swapaxes(diag, -1, -2)


def _online_step(carry, q, kt, vt):
    m, acc = carry
    s = _dot_nt(q, kt)
    m_new = jnp.maximum(m, jnp.max(s, axis=-1, keepdims=True))
    p = jnp.exp2(s - m_new)
    acc = jnp.exp2(m - m_new) * acc + _dot(p.astype(BF16), vt)
    return m_new, acc


def _key_rounds(n_keys, n_rounds, align):
    units = n_keys // align
    assert units * align == n_keys
    n_rounds = 1 if n_keys <= 2 * MXU_WIDTH else min(n_rounds, units)
    sizes = [(units // n_rounds + (1 if r < units % n_rounds else 0)) * align for r in range(n_rounds)]
    starts = np.concatenate([[0], np.cumsum(sizes)[:-1]])
    return [(int(s), int(z)) for s, z in zip(starts, sizes)]


def _attend(qs, k_ref, v_ref, ksls, vsls, n_rounds, align):
    tq = qs[0].shape[0]
    n = len(qs)
    width = vsls[0].stop - vsls[0].start
    init = (jnp.full((tq, 1), NEG_BIG, F32), jnp.zeros((tq, width), F32))
    carries = [init for _ in range(n)]
    for start, size in _key_rounds(k_ref.shape[1], n_rounds, align):
        rows = slice(start, start + size)
        carries = [_online_step(carries[i], qs[i], k_ref[0, rows, ksls[i]], v_ref[0, rows, vsls[i]])
                   for i in range(n)]
    return [c[1] for c in carries]


def _mla_kernel(q_ref, k_ref, v_ref, o_ref):
    sls = [slice(hh * HEAD_PAD, (hh + 1) * HEAD_PAD) for hh in range(2)]
    accs = _attend([q_ref[0, :, sl] for sl in sls], k_ref, v_ref, sls, sls, MLA_ROUNDS, MXU_WIDTH)
    outs = [acc / acc[:, V_B:V_B + 1] for acc in accs]
    lane = _lane_iota(outs[0].shape)
    o_ref[0] = jnp.where(lane < V_B, outs[0], pltpu.roll(outs[1], V_B, 1)).astype(o_ref.dtype)


def _mla_attention(q3, k3, v3):
    bsz, s, _ = q3.shape
    keys = k3.shape[1]
    tq = Q_TILE
    pair = 2 * HEAD_PAD
    return pl.pallas_call(
        _mla_kernel,
        grid=(bsz, H_B // 2, s // tq),
        in_specs=[
            pl.BlockSpec((1, tq, pair), lambda b, h, i: (b, i, h)),
            pl.BlockSpec((1, keys, pair), lambda b, h, i: (b, 0, h)),
            pl.BlockSpec((1, keys, pair), lambda b, h, i: (b, 0, h)),
        ],
        out_specs=pl.BlockSpec((1, tq, LANES), lambda b, h, i: (b, i, h)),
        out_shape=jax.ShapeDtypeStruct((bsz, s, W_B), BF16),
        compiler_params=pltpu.CompilerParams(dimension_semantics=("arbitrary",) * 3,
                                             vmem_limit_bytes=VMEM_LIMIT),
        name="mla_lat" if keys > s else "mla_ctx",
    )(q3, k3, v3)


def _diff_kernel(lam_ref, q_ref, k_ref, v_ref, o_ref, *, lam_init):
    lq = lam_ref[...]
    lam = (jnp.exp(jnp.sum(lq[0:1] * lq[1:2], axis=-1, keepdims=True))
           - jnp.exp(jnp.sum(lq[2:3] * lq[3:4], axis=-1, keepdims=True)) + lam_init)
    q = q_ref[0].astype(F32)
    lane = _lane_iota(q.shape)
    full, wide = slice(0, LANES), slice(0, 2 * LANES)
    qcs = [jnp.where(lane < D_C, q, 0.0).astype(BF16), jnp.where(lane >= D_C, q, 0.0).astype(BF16)]
    accs = _attend(qcs, k_ref, v_ref, [full, full], [wide, wide], DIFF_ROUNDS, LANES)
    res = [acc[:, 0:LANES] / acc[:, LANES:LANES + 1] for acc in accs]
    o_ref[0] = res[0] - lam * res[1]


def _diff_attention(lam_qk, q3, k3, v3, lam_init):
    bsz, s, _ = q3.shape
    keys = k3.shape[1]
    tq = Q_TILE
    return pl.pallas_call(
        functools.partial(_diff_kernel, lam_init=lam_init),
        grid=(bsz, H_C, s // tq),
        in_specs=[
            pl.BlockSpec((4, D_C), lambda b, h, i: (0, 0)),
            pl.BlockSpec((1, tq, LANES), lambda b, h, i: (b, i, h)),
            pl.BlockSpec((1, keys, LANES), lambda b, h, i: (b, 0, h)),
            pl.BlockSpec((1, keys, 2 * LANES), lambda b, h, i: (b, 0, h)),
        ],
        out_specs=pl.BlockSpec((1, tq, LANES), lambda b, h, i: (b, i, h)),
        out_shape=jax.ShapeDtypeStruct((bsz, s, W_C), F32),
        compiler_params=pltpu.CompilerParams(dimension_semantics=("arbitrary",) * 3,
                                             vmem_limit_bytes=VMEM_LIMIT),
        name="diff_lat" if keys > s else "diff_ctx",
    )(lam_qk, q3, k3, v3)


def _merge_kernel(x_ref, mod_ref, g1_ref, wg_ref, of_ref, ob_ref, sr_ref, yb_ref, yc_ref, ggla_ref, gsub_ref,
                  woa_ref, wob_ref, woc_ref, wout_ref, o_ref, *, lam_init):
    x = x_ref[...]
    m = mod_ref[0]
    h = _rms(x, g1_ref[...]) * (1.0 + m[1:2]) + m[0:1]
    gates = _sigmoid(_dot(h.astype(BF16), wg_ref[...]))
    o = of_ref[...] + ob_ref[...]
    sr = sr_ref[...]
    yc = yc_ref[...]
    ya_t, yc_t = [], []
    for hd in range(H_A):
        sl = slice(hd * LANES, (hd + 1) * LANES)
        ya_t.append(_rms(o[:, sl], ggla_ref[...]) * sr[:, sl])
        yc_t.append(_rms(yc[:, sl], gsub_ref[...]) * (1.0 - lam_init))
    ya = jnp.concatenate(ya_t, axis=1).astype(BF16)
    ycn = jnp.concatenate(yc_t, axis=1).astype(BF16)
    merged = (gates[:, 0:D_MODEL] * _dot(ya, woa_ref[...])
              + gates[:, D_MODEL:2 * D_MODEL] * _dot(yb_ref[...], wob_ref[...])
              + gates[:, 2 * D_MODEL:3 * D_MODEL] * _dot(ycn, woc_ref[...]))
    o_ref[...] = x + m[2:3] * _dot(merged.astype(BF16), wout_ref[...])


def _merge(x2, mod8, group0, lw, of2, ob2, sr2, yb2, yc2, lam_init, per_seq_groups):
    t = x2.shape[0]
    tm = TOKEN_TILE
    tiles_per_seq = LAT_SEQ // tm
    const = lambda shape: pl.BlockSpec(shape, lambda i: (0,) * len(shape))
    row = lambda w: pl.BlockSpec((tm, w), lambda i: (i, 0))
    if per_seq_groups:
        mod_spec = pl.BlockSpec((1, 8, D_MODEL), lambda i: (group0 + i // tiles_per_seq, 0, 0))
    else:
        mod_spec = pl.BlockSpec((1, 8, D_MODEL), lambda i: (group0, 0, 0))
    return pl.pallas_call(
        functools.partial(_merge_kernel, lam_init=lam_init),
        grid=(t // tm,),
        in_specs=[row(D_MODEL), mod_spec, const((1, D_MODEL)), const((D_MODEL, GATE_COLS)),
                  row(W_A), row(W_A), row(W_A), row(W_B), row(W_C), const((1, LANES)), const((1, LANES)),
                  const((W_A, D_MODEL)), const((W_B, D_MODEL)), const((W_C, D_MODEL)),
                  const((D_MODEL, D_MODEL))],
        out_specs=row(D_MODEL),
        out_shape=jax.ShapeDtypeStruct((t, D_MODEL), F32),
        compiler_params=pltpu.CompilerParams(dimension_semantics=("arbitrary",), vmem_limit_bytes=VMEM_LIMIT),
        name="merge",
    )(x2, mod8, lw["g1"], lw["w_gates"], of2, ob2, sr2, yb2, yc2, lw["ggla"], lw["gsub"],
      lw["woa"], lw["wob"], lw["woc"], lw["wout"])


def _mlp_kernel(x_ref, mod_ref, g2_ref, w1_ref, w2_ref, o_ref):
    x = x_ref[...]
    m = mod_ref[0]
    h = _rms(x, g2_ref[...]) * (1.0 + m[4:5]) + m[3:4]
    a = jnp.maximum(_dot(h.astype(BF16), w1_ref[...]), 0.0)
    o_ref[...] = x + m[5:6] * _dot((a * a).astype(BF16), w2_ref[...])


def _mlp(x2, mod8, group0, lw, per_seq_groups):
    t = x2.shape[0]
    tm = TOKEN_TILE
    tiles_per_seq = LAT_SEQ // tm
    const = lambda shape: pl.BlockSpec(shape, lambda i: (0,) * len(shape))
    row = lambda w: pl.BlockSpec((tm, w), lambda i: (i, 0))
    if per_seq_groups:
        mod_spec = pl.BlockSpec((1, 8, D_MODEL), lambda i: (group0 + i // tiles_per_seq, 0, 0))
    else:
        mod_spec = pl.BlockSpec((1, 8, D_MODEL), lambda i: (group0, 0, 0))
    return pl.pallas_call(
        _mlp_kernel,
        grid=(t // tm,),
        in_specs=[row(D_MODEL), mod_spec, const((1, D_MODEL)), const((D_MODEL, D_FF)), const((D_FF, D_MODEL))],
        out_specs=row(D_MODEL),
        out_shape=jax.ShapeDtypeStruct((t, D_MODEL), F32),
        compiler_params=pltpu.CompilerParams(dimension_semantics=("arbitrary",), vmem_limit_bytes=VMEM_LIMIT),
        name="mlp",
    )(x2, mod8, lw["g2"], lw["w1"], lw["w2"])


def _pad_heads(w, real, pad):
    k = w.shape[0]
    w = w.reshape(k, -1, real)
    return jnp.pad(w, ((0, 0), (0, 0), (0, pad - real))).reshape(k, -1)


def _layer_weights(l, p):
    widths = (H_A * DK_A, H_A * DK_A, W_A, W_A, 2 * GATE_RANK, Q_RANK, KV_RANK, ROPE_B, W_C, W_C, W_C, GATE_COLS)
    offs = np.concatenate([[0], np.cumsum(widths)])
    w_in = p["w_in"][l]
    aq, ak, av, ar, aa, qd, kvd, kr, dq, dk, dv, gates = (w_in[:, offs[i]:offs[i + 1]] for i in range(12))
    z = lambda n: jnp.zeros((D_MODEL, n), F32)
    misc = jnp.concatenate([aa, kr, z(LANES - 2 * GATE_RANK - ROPE_B)], axis=1)
    krp = jnp.concatenate([z(NOPE_B), kr, z(LANES - QK_B)], axis=1)
    w_in_a = jnp.concatenate([aq, ak, av, ar, qd, kvd, dq, dk, dv, misc, krp], axis=1).astype(BF16)
    wa2 = jnp.zeros((LANES, 512), F32)
    wa2 = wa2.at[0:GATE_RANK, 0:256].set(p["w_gla_a2"][l, 0])
    wa2 = wa2.at[GATE_RANK:2 * GATE_RANK, 256:512].set(p["w_gla_a2"][l, 1])
    pad_gain = lambda g: jnp.pad(g, (0, HEAD_PAD - QK_B)).reshape(1, HEAD_PAD)
    twice = lambda g: jnp.concatenate([g, g]).reshape(1, LANES)
    return dict(
        g1=p["g_norm1"][l].reshape(1, D_MODEL), g2=p["g_norm2"][l].reshape(1, D_MODEL),
        w_in_a=w_in_a, w_gates=gates.astype(BF16),
        wa2=wa2.astype(BF16), ba=p["b_gla_a"][l].reshape(1, 512),
        gqa=p["g_mla_qa"][l].reshape(1, Q_RANK), gkva=p["g_mla_kva"][l].reshape(1, KV_RANK),
        wuq=_pad_heads(p["w_mla_uq"][l], QK_B, HEAD_PAD).astype(BF16),
        wuk=_pad_heads(p["w_mla_uk"][l], NOPE_B, HEAD_PAD).astype(BF16),
        wuv=_pad_heads(p["w_mla_uv"][l], V_B, HEAD_PAD).astype(BF16),
        gq=pad_gain(p["g_mla_q"][l]), gk=pad_gain(p["g_mla_k"][l]),
        gdq=twice(p["g_diff_q"][l]), gdk=twice(p["g_diff_k"][l]),
        ggla=p["g_gla_out"][l].reshape(1, DV_A), gsub=p["g_diff_sub"][l].reshape(1, 2 * D_C),
        woa=p["w_o_gla"][l].astype(BF16), wob=p["w_o_mla"][l].astype(BF16), woc=p["w_o_diff"][l].astype(BF16),
        wout=p["w_out"][l].astype(BF16), w1=p["w_mlp1"][l].astype(BF16), w2=p["w_mlp2"][l].astype(BF16),
        lam_qk=p["lam_qk"][l],
    )


def _layer(x2, bsz, seq, mod8, group0, lw, lam_init, consts, tabs, cached):
    is_latent = cached is not None
    past = cached[1].shape[1] if is_latent else 0
    outs = _inproj(x2, bsz, past, mod8, group0, lw, tabs, rope=is_latent, emit_state=not is_latent)
    gq, gk, gv, sr, gf, gb, mq, mk, mv, dq, dk, dv = outs[:12]
    r3 = lambda a: a.reshape(bsz, seq, a.shape[-1])

    if is_latent:
        s0, ckv_c, krope_c, dk_c, dv_c = cached
        s0f, s0b = _state_to_blocks(s0[:, 0]), _state_to_blocks(s0[:, 1])
    else:
        s0f = s0b = jnp.zeros((bsz, H_A * DV_A, H_A * DK_A), F32)
    of, sf = _gla_scan(r3(gq), r3(gk), r3(gv), r3(gf), s0f, consts[0], rev=False)
    ob, sb = _gla_scan(r3(gq), r3(gk), r3(gv), r3(gb), s0b, consts[1], rev=True)

    if is_latent:
        krp_c = jnp.pad(krope_c, ((0, 0), (0, 0), (NOPE_B, LANES - QK_B)))
        mk, mv, dk, dv = _cache_kv(ckv_c, krp_c, dk_c.reshape(bsz, past, W_C), dv_c.reshape(bsz, past, W_C),
                                   lw, (mk, mv, dk, dv))
    yb = _mla_attention(r3(mq), mk, mv)
    yc = _diff_attention(lw["lam_qk"], r3(dq), dk, dv, lam_init)

    x1 = _merge(x2, mod8, group0, lw, of.reshape(-1, W_A), ob.reshape(-1, W_A), sr, yb.reshape(-1, W_B),
                yc.reshape(-1, W_C), lam_init, is_latent)
    x_out = _mlp(x1, mod8, group0, lw, is_latent)

    new_state = None
    if not is_latent:
        ckv, misc, dk32, dv32 = outs[12:]
        gla_state = jnp.stack([_blocks_to_state(sf), _blocks_to_state(sb)], axis=1)
        new_state = (gla_state, ckv.reshape(bsz, seq, KV_RANK),
                     misc[:, 2 * GATE_RANK:2 * GATE_RANK + ROPE_B].reshape(bsz, seq, ROPE_B),
                     dk32.reshape(bsz, seq, H_C, 2, D_C), dv32.reshape(bsz, seq, H_C, 2 * D_C))
    return x_out, new_state


def kernel(x_prompt, x_sample, state_gla, cache_mla_ckv, cache_mla_krope, cache_diff_k, cache_diff_v, c, c_ctx, w_mod, b_mod, g_norm1, g_norm2, w_in, w_gla_a2, b_gla_a, g_gla_out, g_mla_qa, g_mla_kva, w_mla_uq, w_mla_uk, w_mla_uv, g_mla_q, g_mla_k, g_diff_q, g_diff_k, lam_qk, g_diff_sub, w_o_gla, w_o_mla, w_o_diff, w_out, w_mlp1, w_mlp2):
    params = dict(w_in=w_in, g_norm1=g_norm1, g_norm2=g_norm2, w_gla_a2=w_gla_a2, b_gla_a=b_gla_a,
                  g_gla_out=g_gla_out, g_mla_qa=g_mla_qa, g_mla_kva=g_mla_kva, w_mla_uq=w_mla_uq,
                  w_mla_uk=w_mla_uk, w_mla_uv=w_mla_uv, g_mla_q=g_mla_q, g_mla_k=g_mla_k, g_diff_q=g_diff_q,
                  g_diff_k=g_diff_k, lam_qk=lam_qk, g_diff_sub=g_diff_sub, w_o_gla=w_o_gla, w_o_mla=w_o_mla,
                  w_o_diff=w_o_diff, w_out=w_out, w_mlp1=w_mlp1, w_mlp2=w_mlp2)
    nb, ns, _ = x_prompt.shape
    db, ds, _ = x_sample.shape
    assert ds == LAT_SEQ and db + 1 <= 8

    cond8 = jnp.concatenate([c_ctx[None, :], c, jnp.zeros((8 - 1 - db, D_MODEL), F32)], axis=0)
    mod = _modulation(cond8, w_mod, b_mod).reshape(DEPTH, 8, 6, D_MODEL)
    mod = jnp.pad(mod, ((0, 0), (0, 0), (0, 2), (0, 0)))

    consts = (_gla_constants(False), _gla_constants(True))
    tabs = (_rope_tables(ROPE_B, NOPE_B, LANES), _rope_tables(D_C, 0, D_C))

    yp = x_prompt.reshape(nb * ns, D_MODEL)
    ys = x_sample.reshape(db * ds, D_MODEL)
    states = []
    for l in range(DEPTH):
        lw = _layer_weights(l, params)
        lam_init = 0.8 - 0.6 * math.exp(-0.3 * l)
        yp, st = _layer(yp, nb, ns, mod[l], 0, lw, lam_init, consts, tabs, None)
        states.append(st)
        cached = (state_gla[:, l], cache_mla_ckv[:, l], cache_mla_krope[:, l], cache_diff_k[:, l],
                  cache_diff_v[:, l])
        ys, _ = _layer(ys, db, ds, mod[l], 1, lw, lam_init, consts, tabs, cached)
    stack = lambda i: jnp.stack([s[i] for s in states], axis=1)
    return (yp.reshape(nb, ns, D_MODEL), ys.reshape(db, ds, D_MODEL), stack(0), stack(1), stack(2), stack(3),
            stack(4))
```

```python
import functools
import math

import numpy as np
import jax
import jax.numpy as jnp
from jax import lax
from jax.experimental import pallas as pl
from jax.experimental.pallas import tpu as pltpu

F32 = jnp.float32
BF16 = jnp.bfloat16

D_MODEL = 1024
DEPTH = 2
GRID_W = 64
ROPE_BASE = 10000.0
EPS = 1e-6
H_A, DK_A, DV_A = 4, 64, 128
GATE_RANK = 16
GATE_TAU = 16.0
CHUNK = 64
W_A = H_A * DV_A
H_B, Q_RANK, KV_RANK = 8, 384, 256
NOPE_B, ROPE_B, V_B = 64, 32, 64
QK_B = NOPE_B + ROPE_B
W_B = H_B * V_B
H_C, D_C = 4, 64
W_C = H_C * 2 * D_C
D_FF = 4 * D_MODEL
GATE_COLS = 3 * D_MODEL
LAT_SEQ = 4096

LANES = 128
HEAD_PAD = LANES
MXU_WIDTH = 256
VMEM_LIMIT = 56 * 1024 * 1024

TOKEN_TILE = 256
Q_TILE = 1024
MLA_ROUNDS = 2
DIFF_ROUNDS = 2
GLA_TILE = 256

C_AQ, C_AK, C_AV, C_AR = 0, 256, 512, 1024
C_QD, C_KVD = 1536, 1920
C_DQ, C_DK, C_DV = 2176, 2688, 3200
C_MISC, C_KRP = 3712, 3840
IN_COLS_A = 3968

NEG_BIG = -1e30
LOG2E = math.log2(math.e)


def _dot(a, b):
    return jnp.dot(a, b, preferred_element_type=F32)


def _dot_nt(a, b):
    return lax.dot_general(a, b, (((1,), (1,)), ((), ())), preferred_element_type=F32)


def _sigmoid(x):
    return 1.0 / (1.0 + jnp.exp(-x))


def _rms(x, g):
    ms = jnp.mean(x * x, axis=-1, keepdims=True)
    return x * lax.rsqrt(ms + EPS) * g


def _lane_iota(shape):
    return lax.broadcasted_iota(jnp.int32, shape, len(shape) - 1)


def _ones_lane0(shape):
    return jnp.where(_lane_iota(shape) % LANES == 0, 1.0, 0.0).astype(F32)


def _mod_kernel(c_ref, w_ref, b_ref, o_ref):
    c = c_ref[...]
    s = c * _sigmoid(c)
    o_ref[0] = _dot(s.astype(BF16), w_ref[0].astype(BF16)) + b_ref[0]


def _modulation(cond8, w_mod, b_mod):
    nt = 768
    return pl.pallas_call(
        _mod_kernel,
        grid=(DEPTH, 6 * D_MODEL // nt),
        in_specs=[
            pl.BlockSpec((8, D_MODEL), lambda l, j: (0, 0)),
            pl.BlockSpec((1, D_MODEL, nt), lambda l, j: (l, 0, j)),
            pl.BlockSpec((1, 1, nt), lambda l, j: (l, 0, j)),
        ],
        out_specs=pl.BlockSpec((1, 8, nt), lambda l, j: (l, 0, j)),
        out_shape=jax.ShapeDtypeStruct((DEPTH, 8, 6 * D_MODEL), F32),
        name="modulation",
    )(cond8, w_mod, b_mod.reshape(DEPTH, 1, 6 * D_MODEL))


def _rope_tables(n_rope, lane0, width):
    half = n_rope // 2
    nf = half // 2
    inv = ROPE_BASE ** (-np.arange(nf, dtype=np.float64) / nf)
    pos = np.arange(GRID_W, dtype=np.float64)[:, None]
    cr = np.zeros((GRID_W, LANES)); cc = np.ones((GRID_W, LANES))
    sr = np.zeros((GRID_W, LANES)); sc = np.zeros((GRID_W, LANES))
    for base in range(0, LANES, width):
        for d in range(n_rope):
            lane = base + lane0 + d
            within = d % half
            ang = pos[:, 0] * inv[within % nf]
            sign = -1.0 if within < nf else 1.0
            if d < half:
                cr[:, lane] = np.cos(ang); cc[:, lane] = 0.0
                sr[:, lane] = sign * np.sin(ang)
            else:
                cc[:, lane] = np.cos(ang)
                sc[:, lane] = sign * np.sin(ang)
    return tuple(jnp.asarray(t, F32) for t in (cr, cc, sr, sc))


def _tile_tables(tabs, row0, n_sub):
    cr_ref, cc_ref, sr_ref, sc_ref = tabs
    cs, ss = [], []
    for j in range(n_sub):
        r = (row0 + j) % GRID_W
        cs.append(cr_ref[pl.ds(r, 1), :] + cc_ref[...])
        ss.append(sr_ref[pl.ds(r, 1), :] + sc_ref[...])
    return jnp.concatenate(cs, axis=0), jnp.concatenate(ss, axis=0)


def _rotate(x, cos_t, sin_t, nf):
    lane = _lane_iota(x.shape)
    up = pltpu.roll(x, LANES - nf, 1)
    dn = pltpu.roll(x, nf, 1)
    partner = jnp.where((lane & nf) == 0, up, dn)
    return x * cos_t + partner * sin_t


def _inproj_kernel(*refs, rope, emit_state, tiles_per_seq):
    it = iter(refs)
    x_ref, mod_ref, g1_ref, w_ref, wa2_ref, ba_ref = (next(it) for _ in range(6))
    gqa_ref, gkva_ref, wuq_ref, wuk_ref, wuv_ref, gq_ref, gk_ref = (next(it) for _ in range(7))
    gdq_ref, gdk_ref = next(it), next(it)
    if rope:
        tab_m = tuple(next(it) for _ in range(4))
        tab_d = tuple(next(it) for _ in range(4))
    o_gq, o_gk, o_gv, o_sr, o_gf, o_gb = (next(it) for _ in range(6))
    o_mq, o_mk, o_mv, o_dq, o_dk, o_dv = (next(it) for _ in range(6))
    if emit_state:
        o_ckv, o_misc, o_dk32, o_dv32 = (next(it) for _ in range(4))

    tm = x_ref.shape[0]
    x = x_ref[...]
    m = mod_ref[0]
    h = _rms(x, g1_ref[...]) * (1.0 + m[1:2]) + m[0:1]
    y = _dot(h.astype(BF16), w_ref[...])

    o_gq[...] = y[:, C_AQ:C_AQ + 256] * (DK_A ** -0.5)
    o_gk[...] = y[:, C_AK:C_AK + 256]
    o_gv[...] = y[:, C_AV:C_AV + 512].astype(BF16)
    r = y[:, C_AR:C_AR + 512]
    o_sr[...] = r * _sigmoid(r)
    misc = y[:, C_MISC:C_MISC + LANES]
    logits = _dot(misc.astype(BF16), wa2_ref[...]) + ba_ref[...]
    logsig = jnp.minimum(logits, 0.0) - jnp.log(1.0 + jnp.exp(-jnp.abs(logits)))
    gate = logsig * (1.0 / GATE_TAU)
    o_gf[...] = gate[:, 0:256]
    o_gb[...] = gate[:, 256:512]

    if rope:
        row0 = (pl.program_id(0) % tiles_per_seq) * (tm // GRID_W)
        cos_m, sin_m = _tile_tables(tab_m, row0, tm // GRID_W)
        cos_d, sin_d = _tile_tables(tab_d, row0, tm // GRID_W)

    qn = _rms(y[:, C_QD:C_QD + Q_RANK], gqa_ref[...])
    q8 = _dot(qn.astype(BF16), wuq_ref[...])
    ckv = _rms(y[:, C_KVD:C_KVD + KV_RANK], gkva_ref[...])
    kn = _dot(ckv.astype(BF16), wuk_ref[...])
    vb = _dot(ckv.astype(BF16), wuv_ref[...])
    o_mv[0] = jnp.where(_lane_iota(vb.shape) % HEAD_PAD == V_B, 1.0, vb).astype(BF16)
    krp = y[:, C_KRP:C_KRP + LANES]
    for hd in range(H_B):
        sl = slice(hd * HEAD_PAD, (hd + 1) * HEAD_PAD)
        qt = q8[:, sl]
        qt = qt * lax.rsqrt(jnp.sum(qt * qt, axis=-1, keepdims=True) * (1.0 / QK_B) + EPS) * gq_ref[...]
        kt = kn[:, sl] + krp
        kt = kt * lax.rsqrt(jnp.sum(kt * kt, axis=-1, keepdims=True) * (1.0 / QK_B) + EPS) * gk_ref[...]
        if rope:
            qt = _rotate(qt, cos_m, sin_m, ROPE_B // 4)
            kt = _rotate(kt, cos_m, sin_m, ROPE_B // 4)
        o_mq[:, sl] = (qt * (QK_B ** -0.5 * LOG2E)).astype(BF16)
        o_mk[0, :, sl] = kt.astype(BF16)

    lane = _lane_iota((tm, LANES))
    lo = lane < D_C
    for hd in range(H_C):
        sl = slice(hd * LANES, (hd + 1) * LANES)
        for src, g_ref, dst, dst32, scale in (
            (C_DQ, gdq_ref, o_dq, None, D_C ** -0.5 * LOG2E),
            (C_DK, gdk_ref, o_dk.at[0], o_dk32 if emit_state else None, 1.0),
        ):
            t = y[:, src + hd * LANES:src + (hd + 1) * LANES]
            sq = t * t
            s_lo = jnp.sum(jnp.where(lo, sq, 0.0), axis=-1, keepdims=True)
            s_hi = jnp.sum(jnp.where(lo, 0.0, sq), axis=-1, keepdims=True)
            inv = jnp.where(lo, lax.rsqrt(s_lo * (1.0 / D_C) + EPS), lax.rsqrt(s_hi * (1.0 / D_C) + EPS))
            t = t * inv * g_ref[...]
            if rope:
                t = _rotate(t, cos_d, sin_d, D_C // 4)
            if dst32 is not None:
                dst32[:, sl] = t
            dst[:, sl] = (t * scale).astype(BF16)
    dv = y[:, C_DV:C_DV + W_C]
    one_tile = _ones_lane0((tm, LANES))
    o_dv[0] = jnp.concatenate(
        [t for hd in range(H_C) for t in (dv[:, hd * LANES:(hd + 1) * LANES], one_tile)], axis=1).astype(BF16)
    if emit_state:
        o_dv32[...] = dv
        o_ckv[...] = ckv
        o_misc[...] = misc


def _inproj(x2, bsz, past, mod8, group0, lw, tabs, *, rope, emit_state):
    t = x2.shape[0]
    tm = TOKEN_TILE
    seq = t // bsz
    tiles_per_seq = LAT_SEQ // tm
    tps = seq // tm
    assert past % tm == 0
    kv_rows = lambda w: pl.BlockSpec((1, tm, w), lambda i: (i // tps, past // tm + i % tps, 0))
    const = lambda shape: pl.BlockSpec(shape, lambda i: (0,) * len(shape))
    row = lambda w: pl.BlockSpec((tm, w), lambda i: (i, 0))
    if rope:
        mod_spec = pl.BlockSpec((1, 8, D_MODEL), lambda i: (group0 + i // tiles_per_seq, 0, 0))
    else:
        mod_spec = pl.BlockSpec((1, 8, D_MODEL), lambda i: (group0, 0, 0))
    in_specs = [
        row(D_MODEL), mod_spec, const((1, D_MODEL)), const((D_MODEL, IN_COLS_A)),
        const((LANES, 512)), const((1, 512)),
        const((1, Q_RANK)), const((1, KV_RANK)), const((Q_RANK, H_B * HEAD_PAD)),
        const((KV_RANK, H_B * HEAD_PAD)), const((KV_RANK, H_B * HEAD_PAD)), const((1, LANES)), const((1, LANES)),
        const((1, LANES)), const((1, LANES)),
    ]
    args = [x2, mod8, lw["g1"], lw["w_in_a"], lw["wa2"], lw["ba"], lw["gqa"], lw["gkva"], lw["wuq"],
            lw["wuk"], lw["wuv"], lw["gq"], lw["gk"], lw["gdq"], lw["gdk"]]
    if rope:
        in_specs += [const((GRID_W, LANES))] * 8
        args += list(tabs[0]) + list(tabs[1])
    flat, keyed = "flat", "keyed"
    outs = [(256, F32, flat), (256, F32, flat), (512, BF16, flat), (512, F32, flat), (256, F32, flat),
            (256, F32, flat), (H_B * HEAD_PAD, BF16, flat), (H_B * HEAD_PAD, BF16, keyed),
            (H_B * HEAD_PAD, BF16, keyed), (W_C, BF16, flat), (W_C, BF16, keyed), (2 * W_C, BF16, keyed)]
    if emit_state:
        outs += [(KV_RANK, F32, flat), (LANES, F32, flat), (W_C, F32, flat), (W_C, F32, flat)]
    return pl.pallas_call(
        functools.partial(_inproj_kernel, rope=rope, emit_state=emit_state, tiles_per_seq=tiles_per_seq),
        grid=(t // tm,),
        in_specs=in_specs,
        out_specs=[row(w) if kind == flat else kv_rows(w) for w, _, kind in outs],
        out_shape=[jax.ShapeDtypeStruct((t, w) if kind == flat else (bsz, past + seq, w), dt)
                   for w, dt, kind in outs],
        compiler_params=pltpu.CompilerParams(dimension_semantics=("arbitrary",), vmem_limit_bytes=VMEM_LIMIT),
        name="inproj_lat" if rope else "inproj_ctx",
    )(*args)


def _cache_kv_kernel(ckv_ref, krp_ref, dkc_ref, dvc_ref, wuk_ref, wuv_ref, gk_ref, _mk, _mv, _dk, _dv,
                     mk_ref, mv_ref, dk_ref, dv_ref):
    c = ckv_ref[0].astype(BF16)
    kn = _dot(c, wuk_ref[...])
    vb = _dot(c, wuv_ref[...])
    mv_ref[0] = jnp.where(_lane_iota(vb.shape) % HEAD_PAD == V_B, 1.0, vb).astype(BF16)
    krp = krp_ref[0]
    for hd in range(H_B):
        sl = slice(hd * HEAD_PAD, (hd + 1) * HEAD_PAD)
        kt = kn[:, sl] + krp
        kt = kt * lax.rsqrt(jnp.sum(kt * kt, axis=-1, keepdims=True) * (1.0 / QK_B) + EPS) * gk_ref[...]
        mk_ref[0, :, sl] = kt.astype(BF16)
    dk_ref[0] = dkc_ref[0].astype(BF16)
    dvc = dvc_ref[0]
    one_tile = _ones_lane0((dvc.shape[0], LANES))
    dv_ref[0] = jnp.concatenate(
        [t for hd in range(H_C) for t in (dvc[:, hd * LANES:(hd + 1) * LANES], one_tile)], axis=1).astype(BF16)


def _cache_kv(ckv3, krp3, dkc3, dvc3, lw, bufs):
    bsz, past, _ = ckv3.shape
    const = lambda shape: pl.BlockSpec(shape, lambda b: (0,) * len(shape))
    head = lambda w: pl.BlockSpec((1, past, w), lambda b: (b, 0, 0))
    anyspec = pl.BlockSpec(memory_space=pl.ANY)
    return pl.pallas_call(
        _cache_kv_kernel,
        grid=(bsz,),
        in_specs=[head(KV_RANK), head(LANES), head(W_C), head(W_C), const((KV_RANK, H_B * HEAD_PAD)),
                  const((KV_RANK, H_B * HEAD_PAD)), const((1, LANES))] + [anyspec] * 4,
        out_specs=[head(b.shape[-1]) for b in bufs],
        out_shape=[jax.ShapeDtypeStruct(b.shape, b.dtype) for b in bufs],
        input_output_aliases={7: 0, 8: 1, 9: 2, 10: 3},
        name="cache_kv",
    )(ckv3, krp3, dkc3, dvc3, lw["wuk"], lw["wuv"], lw["gk"], *bufs)


def _gla_constants(rev):
    c = CHUNK
    idx = np.arange(c)
    t, u = idx[:, None], idx[None, :]
    mats = [(u <= t)]
    masks = [np.eye(c, dtype=bool)]
    hs = c // 2
    while hs >= 1:
        blk = idx // (2 * hs)
        lower = (idx % (2 * hs)) < hs
        p = blk * 2 * hs + hs - 1
        m_low = lower[:, None] & (u > t) & (u <= p[:, None])
        m_up = (~lower)[:, None] & (u > p[:, None]) & (u <= t)
        mats.append(m_low | m_up)
        masks.append((blk[:, None] == blk[None, :]) & (~lower)[:, None] & lower[None, :])
        hs //= 2
    mats.append(u > t)
    mats = [m.astype(np.float32) for m in mats]
    masks = [m.astype(np.float32) for m in masks]
    if rev:
        mats = [m[::-1, ::-1] for m in mats]
        masks = [m[::-1, ::-1] for m in masks]
    cm = np.concatenate(mats, axis=0)
    cms = np.concatenate([cm, cm], axis=1)
    lm = np.stack([np.tile(m, (1, H_A)) for m in masks])
    same = lambda n_row, n_col: (np.arange(H_A * n_row)[:, None] // n_row
                                 == np.arange(H_A * n_col)[None, :] // n_col).astype(np.float32)
    km = same(c, DK_A)
    vm = same(c, DV_A)
    sm = same(DV_A, DK_A)
    return (jnp.asarray(cms, BF16), jnp.asarray(lm, F32), jnp.asarray(km, BF16), jnp.asarray(vm, BF16),
            jnp.asarray(sm, BF16))


N_LEVEL = 6


def _gla_kernel(q_ref, k_ref, v_ref, g_ref, s0_ref, cms_ref, lm_ref, km_ref, vm_ref, sm_ref, o_ref, sf_ref,
                st_ref, *, rev):
    ti = pl.program_id(1)
    nt = pl.num_programs(1)

    @pl.when(ti == 0)
    def _():
        st_ref[...] = s0_ref[0]

    def per_head_rows(x):
        return jnp.concatenate([x.astype(BF16)] * H_A, axis=0) * km_ref[...]

    n_chunk = q_ref.shape[1] // CHUNK
    order = range(n_chunk - 1, -1, -1) if rev else range(n_chunk)
    st = st_ref[...]
    for ci in order:
        rows = slice(ci * CHUNK, (ci + 1) * CHUNK)
        q = q_ref[0, rows, :]
        k = k_ref[0, rows, :]
        v16 = v_ref[0, rows, :]
        g = g_ref[0, rows, :]
        g_hi = g.astype(BF16)
        g_lo = (g - g_hi.astype(F32)).astype(BF16)
        sums = _dot(cms_ref[...], jnp.concatenate([g_hi, g_lo], axis=0))
        b = sums[0:CHUNK]
        att = lm_ref[0] * _dot_nt(q.astype(BF16), per_head_rows(k))
        for lv in range(1, N_LEVEL + 1):
            e = jnp.exp(sums[lv * CHUNK:(lv + 1) * CHUNK])
            att = att + lm_ref[lv] * _dot_nt((q * e).astype(BF16), per_head_rows(k * e))
        v_rows = jnp.concatenate([v16] * H_A, axis=0) * vm_ref[...]
        qe = (q * jnp.exp(b)).astype(BF16)
        o_ref[0, rows, :] = _dot(att.astype(BF16), v_rows) + _dot_nt(qe, st.astype(BF16) * sm_ref[...])
        ke = (k * jnp.exp(sums[(N_LEVEL + 1) * CHUNK:(N_LEVEL + 2) * CHUNK])).astype(BF16)
        upd = pl.dot(v16, ke, trans_a=True)
        b_end = b[0:1] if rev else b[CHUNK - 1:CHUNK]
        st = st * jnp.exp(b_end) + upd
    st_ref[...] = st

    @pl.when(ti == nt - 1)
    def _():
        sf_ref[0] = st_ref[...]


def _gla_scan(q3, k3, v3, g3, s0, consts, *, rev):
    bsz, s, _ = q3.shape
    tt = GLA_TILE
    nt = s // tt
    tmap = (lambda b, t: (b, nt - 1 - t, 0)) if rev else (lambda b, t: (b, t, 0))
    bmap = lambda b, t: (b, 0, 0)
    const = lambda shape: pl.BlockSpec(shape, lambda b, t: (0,) * len(shape))
    return pl.pallas_call(
        functools.partial(_gla_kernel, rev=rev),
        grid=(bsz, nt),
        in_specs=[
            pl.BlockSpec((1, tt, 256), tmap), pl.BlockSpec((1, tt, 256), tmap),
            pl.BlockSpec((1, tt, 512), tmap), pl.BlockSpec((1, tt, 256), tmap),
            pl.BlockSpec((1, 512, 256), bmap),
        ] + [const(c.shape) for c in consts],
        out_specs=[pl.BlockSpec((1, tt, 512), tmap), pl.BlockSpec((1, 512, 256), bmap)],
        out_shape=[jax.ShapeDtypeStruct((bsz, s, W_A), F32), jax.ShapeDtypeStruct((bsz, 512, 256), F32)],
        scratch_shapes=[pltpu.VMEM((512, 256), F32)],
        compiler_params=pltpu.CompilerParams(dimension_semantics=("arbitrary", "arbitrary"),
                                             vmem_limit_bytes=VMEM_LIMIT),
        name="gla_bwd" if rev else "gla_fwd",
    )(q3, k3, v3, g3, s0, *consts)


def _state_to_blocks(s0):
    st = jnp.swapaxes(s0, -1, -2)
    eye = jnp.eye(H_A, dtype=s0.dtype)
    full = st[:, :, :, None, :] * eye[None, :, None, :, None]
    return full.reshape(s0.shape[0], H_A * DV_A, H_A * DK_A)


def _blocks_to_state(sf):
    bsz = sf.shape[0]
    full = sf.reshape(bsz, H_A, DV_A, H_A, DK_A)
    diag = jnp.stack([full[:, h, :, h, :] for h in range(H_A)], axis=1)
    return jnp.swapaxes(diag, -1, -2)


def _online_step(carry, q, kt, vt):
    m, acc = carry
    s = _dot_nt(q, kt)
    m_new = jnp.maximum(m, jnp.max(s, axis=-1, keepdims=True))
    p = jnp.exp2(s - m_new)
    acc = jnp.exp2(m - m_new) * acc + _dot(p.astype(BF16), vt)
    return m_new, acc


def _key_rounds(n_keys, n_rounds, align):
    units = n_keys // align
    assert units * align == n_keys
    n_rounds = 1 if n_keys <= 2 * MXU_WIDTH else min(n_rounds, units)
    sizes = [(units // n_rounds + (1 if r < units % n_rounds else 0)) * align for r in range(n_rounds)]
    starts = np.concatenate([[0], np.cumsum(sizes)[:-1]])
    return [(int(s), int(z)) for s, z in zip(starts, sizes)]


def _attend(qs, k_ref, v_ref, ksls, vsls, n_rounds, align):
    tq = qs[0].shape[0]
    n = len(qs)
    width = vsls[0].stop - vsls[0].start
    init = (jnp.full((tq, 1), NEG_BIG, F32), jnp.zeros((tq, width), F32))
    carries = [init for _ in range(n)]
    for start, size in _key_rounds(k_ref.shape[1], n_rounds, align):
        rows = slice(start, start + size)
        carries = [_online_step(carries[i], qs[i], k_ref[0, rows, ksls[i]], v_ref[0, rows, vsls[i]])
                   for i in range(n)]
    return [c[1] for c in carries]


def _mla_kernel(q_ref, k_ref, v_ref, o_ref):
    sls = [slice(hh * HEAD_PAD, (hh + 1) * HEAD_PAD) for hh in range(2)]
    accs = _attend([q_ref[0, :, sl] for sl in sls], k_ref, v_ref, sls, sls, MLA_ROUNDS, MXU_WIDTH)
    outs = [acc / acc[:, V_B:V_B + 1] for acc in accs]
    lane = _lane_iota(outs[0].shape)
    o_ref[0] = jnp.where(lane < V_B, outs[0], pltpu.roll(outs[1], V_B, 1)).astype(o_ref.dtype)


def _mla_attention(q3, k3, v3):
    bsz, s, _ = q3.shape
    keys = k3.shape[1]
    tq = min(Q_TILE, s)
    pair = 2 * HEAD_PAD
    return pl.pallas_call(
        _mla_kernel,
        grid=(bsz, H_B // 2, s // tq),
        in_specs=[
            pl.BlockSpec((1, tq, pair), lambda b, h, i: (b, i, h)),
            pl.BlockSpec((1, keys, pair), lambda b, h, i: (b, 0, h)),
            pl.BlockSpec((1, keys, pair), lambda b, h, i: (b, 0, h)),
        ],
        out_specs=pl.BlockSpec((1, tq, LANES), lambda b, h, i: (b, i, h)),
        out_shape=jax.ShapeDtypeStruct((bsz, s, W_B), BF16),
        compiler_params=pltpu.CompilerParams(dimension_semantics=("arbitrary",) * 3,
                                             vmem_limit_bytes=VMEM_LIMIT),
        name="mla_lat" if keys > s else "mla_ctx",
    )(q3, k3, v3)


def _diff_kernel(lam_ref, q_ref, k_ref, v_ref, o_ref, *, lam_init):
    lq = lam_ref[...]
    lam = (jnp.exp(jnp.sum(lq[0:1] * lq[1:2], axis=-1, keepdims=True))
           - jnp.exp(jnp.sum(lq[2:3] * lq[3:4], axis=-1, keepdims=True)) + lam_init)
    q = q_ref[0].astype(F32)
    lane = _lane_iota(q.shape)
    full, wide = slice(0, LANES), slice(0, 2 * LANES)
    qcs = [jnp.where(lane < D_C, q, 0.0).astype(BF16), jnp.where(lane >= D_C, q, 0.0).astype(BF16)]
    accs = _attend(qcs, k_ref, v_ref, [full, full], [wide, wide], DIFF_ROUNDS, LANES)
    res = [acc[:, 0:LANES] / acc[:, LANES:LANES + 1] for acc in accs]
    o_ref[0] = res[0] - lam * res[1]


def _diff_attention(lam_qk, q3, k3, v3, lam_init):
    bsz, s, _ = q3.shape
    keys = k3.shape[1]
    tq = min(Q_TILE, s)
    return pl.pallas_call(
        functools.partial(_diff_kernel, lam_init=lam_init),
        grid=(bsz, H_C, s // tq),
        in_specs=[
            pl.BlockSpec((4, D_C), lambda b, h, i: (0, 0)),
            pl.BlockSpec((1, tq, LANES), lambda b, h, i: (b, i, h)),
            pl.BlockSpec((1, keys, LANES), lambda b, h, i: (b, 0, h)),
            pl.BlockSpec((1, keys, 2 * LANES), lambda b, h, i: (b, 0, h)),
        ],
        out_specs=pl.BlockSpec((1, tq, LANES), lambda b, h, i: (b, i, h)),
        out_shape=jax.ShapeDtypeStruct((bsz, s, W_C), F32),
        compiler_params=pltpu.CompilerParams(dimension_semantics=("arbitrary",) * 3,
                                             vmem_limit_bytes=VMEM_LIMIT),
        name="diff_lat" if keys > s else "diff_ctx",
    )(lam_qk, q3, k3, v3)


def _merge_kernel(x_ref, mod_ref, g1_ref, wg_ref, of_ref, ob_ref, sr_ref, yb_ref, yc_ref, ggla_ref, gsub_ref,
                  woa_ref, wob_ref, woc_ref, wout_ref, o_ref, *, lam_init):
    x = x_ref[...]
    m = mod_ref[0]
    h = _rms(x, g1_ref[...]) * (1.0 + m[1:2]) + m[0:1]
    gates = _sigmoid(_dot(h.astype(BF16), wg_ref[...]))
    o = of_ref[...] + ob_ref[...]
    sr = sr_ref[...]
    yc = yc_ref[...]
    ya_t, yc_t = [], []
    for hd in range(H_A):
        sl = slice(hd * LANES, (hd + 1) * LANES)
        ya_t.append(_rms(o[:, sl], ggla_ref[...]) * sr[:, sl])
        yc_t.append(_rms(yc[:, sl], gsub_ref[...]) * (1.0 - lam_init))
    ya = jnp.concatenate(ya_t, axis=1).astype(BF16)
    ycn = jnp.concatenate(yc_t, axis=1).astype(BF16)
    merged = (gates[:, 0:D_MODEL] * _dot(ya, woa_ref[...])
              + gates[:, D_MODEL:2 * D_MODEL] * _dot(yb_ref[...], wob_ref[...])
              + gates[:, 2 * D_MODEL:3 * D_MODEL] * _dot(ycn, woc_ref[...]))
    o_ref[...] = x + m[2:3] * _dot(merged.astype(BF16), wout_ref[...])


def _merge(x2, mod8, group0, lw, of2, ob2, sr2, yb2, yc2, lam_init, per_seq_groups):
    t = x2.shape[0]
    tm = TOKEN_TILE
    tiles_per_seq = LAT_SEQ // tm
    const = lambda shape: pl.BlockSpec(shape, lambda i: (0,) * len(shape))
    row = lambda w: pl.BlockSpec((tm, w), lambda i: (i, 0))
    if per_seq_groups:
        mod_spec = pl.BlockSpec((1, 8, D_MODEL), lambda i: (group0 + i // tiles_per_seq, 0, 0))
    else:
        mod_spec = pl.BlockSpec((1, 8, D_MODEL), lambda i: (group0, 0, 0))
    return pl.pallas_call(
        functools.partial(_merge_kernel, lam_init=lam_init),
        grid=(t // tm,),
        in_specs=[row(D_MODEL), mod_spec, const((1, D_MODEL)), const((D_MODEL, GATE_COLS)),
                  row(W_A), row(W_A), row(W_A), row(W_B), row(W_C), const((1, LANES)), const((1, LANES)),
                  const((W_A, D_MODEL)), const((W_B, D_MODEL)), const((W_C, D_MODEL)),
                  const((D_MODEL, D_MODEL))],
        out_specs=row(D_MODEL),
        out_shape=jax.ShapeDtypeStruct((t, D_MODEL), F32),
        compiler_params=pltpu.CompilerParams(dimension_semantics=("arbitrary",), vmem_limit_bytes=VMEM_LIMIT),
        name="merge",
    )(x2, mod8, lw["g1"], lw["w_gates"], of2, ob2, sr2, yb2, yc2, lw["ggla"], lw["gsub"],
      lw["woa"], lw["wob"], lw["woc"], lw["wout"])


def _mlp_kernel(x_ref, mod_ref, g2_ref, w1_ref, w2_ref, o_ref):
    x = x_ref[...]
    m = mod_ref[0]
    h = _rms(x, g2_ref[...]) * (1.0 + m[4:5]) + m[3:4]
    a = jnp.maximum(_dot(h.astype(BF16), w1_ref[...]), 0.0)
    o_ref[...] = x + m[5:6] * _dot((a * a).astype(BF16), w2_ref[...])


def _mlp(x2, mod8, group0, lw, per_seq_groups):
    t = x2.shape[0]
    tm = TOKEN_TILE
    tiles_per_seq = LAT_SEQ // tm
    const = lambda shape: pl.BlockSpec(shape, lambda i: (0,) * len(shape))
    row = lambda w: pl.BlockSpec((tm, w), lambda i: (i, 0))
    if per_seq_groups:
        mod_spec = pl.BlockSpec((1, 8, D_MODEL), lambda i: (group0 + i // tiles_per_seq, 0, 0))
    else:
        mod_spec = pl.BlockSpec((1, 8, D_MODEL), lambda i: (group0, 0, 0))
    return pl.pallas_call(
        _mlp_kernel,
        grid=(t // tm,),
        in_specs=[row(D_MODEL), mod_spec, const((1, D_MODEL)), const((D_MODEL, D_FF)), const((D_FF, D_MODEL))],
        out_specs=row(D_MODEL),
        out_shape=jax.ShapeDtypeStruct((t, D_MODEL), F32),
        compiler_params=pltpu.CompilerParams(dimension_semantics=("arbitrary",), vmem_limit_bytes=VMEM_LIMIT),
        name="mlp",
    )(x2, mod8, lw["g2"], lw["w1"], lw["w2"])


def _pad_heads(w, real, pad):
    k = w.shape[0]
    w = w.reshape(k, -1, real)
    return jnp.pad(w, ((0, 0), (0, 0), (0, pad - real))).reshape(k, -1)


def _layer_weights(l, p):
    widths = (H_A * DK_A, H_A * DK_A, W_A, W_A, 2 * GATE_RANK, Q_RANK, KV_RANK, ROPE_B, W_C, W_C, W_C, GATE_COLS)
    offs = np.concatenate([[0], np.cumsum(widths)])
    w_in = p["w_in"][l]
    aq, ak, av, ar, aa, qd, kvd, kr, dq, dk, dv, gates = (w_in[:, offs[i]:offs[i + 1]] for i in range(12))
    z = lambda n: jnp.zeros((D_MODEL, n), F32)
    misc = jnp.concatenate([aa, kr, z(LANES - 2 * GATE_RANK - ROPE_B)], axis=1)
    krp = jnp.concatenate([z(NOPE_B), kr, z(LANES - QK_B)], axis=1)
    w_in_a = jnp.concatenate([aq, ak, av, ar, qd, kvd, dq, dk, dv, misc, krp], axis=1).astype(BF16)
    wa2 = jnp.zeros((LANES, 512), F32)
    wa2 = wa2.at[0:GATE_RANK, 0:256].set(p["w_gla_a2"][l, 0])
    wa2 = wa2.at[GATE_RANK:2 * GATE_RANK, 256:512].set(p["w_gla_a2"][l, 1])
    pad_gain = lambda g: jnp.pad(g, (0, HEAD_PAD - QK_B)).reshape(1, HEAD_PAD)
    twice = lambda g: jnp.concatenate([g, g]).reshape(1, LANES)
    return dict(
        g1=p["g_norm1"][l].reshape(1, D_MODEL), g2=p["g_norm2"][l].reshape(1, D_MODEL),
        w_in_a=w_in_a, w_gates=gates.astype(BF16),
        wa2=wa2.astype(BF16), ba=p["b_gla_a"][l].reshape(1, 512),
        gqa=p["g_mla_qa"][l].reshape(1, Q_RANK), gkva=p["g_mla_kva"][l].reshape(1, KV_RANK),
        wuq=_pad_heads(p["w_mla_uq"][l], QK_B, HEAD_PAD).astype(BF16),
        wuk=_pad_heads(p["w_mla_uk"][l], NOPE_B, HEAD_PAD).astype(BF16),
        wuv=_pad_heads(p["w_mla_uv"][l], V_B, HEAD_PAD).astype(BF16),
        gq=pad_gain(p["g_mla_q"][l]), gk=pad_gain(p["g_mla_k"][l]),
        gdq=twice(p["g_diff_q"][l]), gdk=twice(p["g_diff_k"][l]),
        ggla=p["g_gla_out"][l].reshape(1, DV_A), gsub=p["g_diff_sub"][l].reshape(1, 2 * D_C),
        woa=p["w_o_gla"][l].astype(BF16), wob=p["w_o_mla"][l].astype(BF16), woc=p["w_o_diff"][l].astype(BF16),
        wout=p["w_out"][l].astype(BF16), w1=p["w_mlp1"][l].astype(BF16), w2=p["w_mlp2"][l].astype(BF16),
        lam_qk=p["lam_qk"][l],
    )


def _layer(x2, bsz, seq, mod8, group0, lw, lam_init, consts, tabs, cached):
    is_latent = cached is not None
    past = cached[1].shape[1] if is_latent else 0
    outs = _inproj(x2, bsz, past, mod8, group0, lw, tabs, rope=is_latent, emit_state=not is_latent)
    gq, gk, gv, sr, gf, gb, mq, mk, mv, dq, dk, dv = outs[:12]
    r3 = lambda a: a.reshape(bsz, seq, a.shape[-1])

    if is_latent:
        s0, ckv_c, krope_c, dk_c, dv_c = cached
        s0f, s0b = _state_to_blocks(s0[:, 0]), _state_to_blocks(s0[:, 1])
    else:
        s0f = s0b = jnp.zeros((bsz, H_A * DV_A, H_A * DK_A), F32)
    of, sf = _gla_scan(r3(gq), r3(gk), r3(gv), r3(gf), s0f, consts[0], rev=False)
    ob, sb = _gla_scan(r3(gq), r3(gk), r3(gv), r3(gb), s0b, consts[1], rev=True)

    if is_latent:
        krp_c = jnp.pad(krope_c, ((0, 0), (0, 0), (NOPE_B, LANES - QK_B)))
        mk, mv, dk, dv = _cache_kv(ckv_c, krp_c, dk_c.reshape(bsz, past, W_C), dv_c.reshape(bsz, past, W_C),
                                   lw, (mk, mv, dk, dv))
    yb = _mla_attention(r3(mq), mk, mv)
    yc = _diff_attention(lw["lam_qk"], r3(dq), dk, dv, lam_init)

    x1 = _merge(x2, mod8, group0, lw, of.reshape(-1, W_A), ob.reshape(-1, W_A), sr, yb.reshape(-1, W_B),
                yc.reshape(-1, W_C), lam_init, is_latent)
    x_out = _mlp(x1, mod8, group0, lw, is_latent)

    new_state = None
    if not is_latent:
        ckv, misc, dk32, dv32 = outs[12:]
        gla_state = jnp.stack([_blocks_to_state(sf), _blocks_to_state(sb)], axis=1)
        new_state = (gla_state, ckv.reshape(bsz, seq, KV_RANK),
                     misc[:, 2 * GATE_RANK:2 * GATE_RANK + ROPE_B].reshape(bsz, seq, ROPE_B),
                     dk32.reshape(bsz, seq, H_C, 2, D_C), dv32.reshape(bsz, seq, H_C, 2 * D_C))
    return x_out, new_state


def kernel(x_prompt, x_sample, state_gla, cache_mla_ckv, cache_mla_krope, cache_diff_k, cache_diff_v, c, c_ctx, w_mod, b_mod, g_norm1, g_norm2, w_in, w_gla_a2, b_gla_a, g_gla_out, g_mla_qa, g_mla_kva, w_mla_uq, w_mla_uk, w_mla_uv, g_mla_q, g_mla_k, g_diff_q, g_diff_k, lam_qk, g_diff_sub, w_o_gla, w_o_mla, w_o_diff, w_out, w_mlp1, w_mlp2):
    params = dict(w_in=w_in, g_norm1=g_norm1, g_norm2=g_norm2, w_gla_a2=w_gla_a2, b_gla_a=b_gla_a,
                  g_gla_out=g_gla_out, g_mla_qa=g_mla_qa, g_mla_kva=g_mla_kva, w_mla_uq=w_mla_uq,
                  w_mla_uk=w_mla_uk, w_mla_uv=w_mla_uv, g_mla_q=g_mla_q, g_mla_k=g_mla_k, g_diff_q=g_diff_q,
                  g_diff_k=g_diff_k, lam_qk=lam_qk, g_diff_sub=g_diff_sub, w_o_gla=w_o_gla, w_o_mla=w_o_mla,
                  w_o_diff=w_o_diff, w_out=w_out, w_mlp1=w_mlp1, w_mlp2=w_mlp2)
    nb, ns, _ = x_prompt.shape
    db, ds, _ = x_sample.shape
    assert ds == LAT_SEQ and db + 1 <= 8

    cond8 = jnp.concatenate([c_ctx[None, :], c, jnp.zeros((8 - 1 - db, D_MODEL), F32)], axis=0)
    mod = _modulation(cond8, w_mod, b_mod).reshape(DEPTH, 8, 6, D_MODEL)
    mod = jnp.pad(mod, ((0, 0), (0, 0), (0, 2), (0, 0)))

    consts = (_gla_constants(False), _gla_constants(True))
    tabs = (_rope_tables(ROPE_B, NOPE_B, LANES), _rope_tables(D_C, 0, D_C))

    yp = x_prompt.reshape(nb * ns, D_MODEL)
    ys = x_sample.reshape(db * ds, D_MODEL)
    states = []
    for l in range(DEPTH):
        lw = _layer_weights(l, params)
        lam_init = 0.8 - 0.6 * math.exp(-0.3 * l)
        yp, st = _layer(yp, nb, ns, mod[l], 0, lw, lam_init, consts, tabs, None)
        states.append(st)
        cached = (state_gla[:, l], cache_mla_ckv[:, l], cache_mla_krope[:, l], cache_diff_k[:, l],
                  cache_diff_v[:, l])
        ys, _ = _layer(ys, db, ds, mod[l], 1, lw, lam_init, consts, tabs, cached)
    stack = lambda i: jnp.stack([s[i] for s in states], axis=1)
    return (yp.reshape(nb, ns, D_MODEL), ys.reshape(db, ds, D_MODEL), stack(0), stack(1), stack(2), stack(3),
            stack(4))
```

```python
import functools
import math

import numpy as np
import jax
import jax.numpy as jnp
from jax import lax
from jax.experimental import pallas as pl
from jax.experimental.pallas import tpu as pltpu

F32 = jnp.float32
BF16 = jnp.bfloat16

D_MODEL = 1024
DEPTH = 2
GRID_W = 64
ROPE_BASE = 10000.0
EPS = 1e-6
H_A, DK_A, DV_A = 4, 64, 128
GATE_RANK = 16
GATE_TAU = 16.0
CHUNK = 64
W_A = H_A * DV_A
H_B, Q_RANK, KV_RANK = 8, 384, 256
NOPE_B, ROPE_B, V_B = 64, 32, 64
QK_B = NOPE_B + ROPE_B
W_B = H_B * V_B
H_C, D_C = 4, 64
W_C = H_C * 2 * D_C
D_FF = 4 * D_MODEL
GATE_COLS = 3 * D_MODEL
LAT_SEQ = 4096

LANES = 128
HEAD_PAD = LANES
MXU_WIDTH = 256
VMEM_LIMIT = 56 * 1024 * 1024

TOKEN_TILE = 256
Q_TILE = 1024
MLA_ROUNDS = 2
DIFF_ROUNDS = 2
GLA_TILE = 256

C_AQ, C_AK, C_AV, C_AR = 0, 256, 512, 1024
C_QD, C_KVD = 1536, 1920
C_DQ, C_DK, C_DV = 2176, 2688, 3200
C_MISC, C_KRP = 3712, 3840
IN_COLS_A = 3968

NEG_BIG = -1e30
LOG2E = math.log2(math.e)


def _dot(a, b):
    return jnp.dot(a, b, preferred_element_type=F32)


def _dot_nt(a, b):
    return lax.dot_general(a, b, (((1,), (1,)), ((), ())), preferred_element_type=F32)


def _sigmoid(x):
    return 1.0 / (1.0 + jnp.exp(-x))


def _rms(x, g):
    ms = jnp.mean(x * x, axis=-1, keepdims=True)
    return x * lax.rsqrt(ms + EPS) * g


def _lane_iota(shape):
    return lax.broadcasted_iota(jnp.int32, shape, len(shape) - 1)


def _ones_lane0(shape):
    return jnp.where(_lane_iota(shape) % LANES == 0, 1.0, 0.0).astype(F32)


def _mod_kernel(c_ref, w_ref, b_ref, o_ref):
    c = c_ref[...]
    s = c * _sigmoid(c)
    o_ref[0] = _dot(s.astype(BF16), w_ref[0].astype(BF16)) + b_ref[0]


def _modulation(cond8, w_mod, b_mod):
    nt = 768
    return pl.pallas_call(
        _mod_kernel,
        grid=(DEPTH, 6 * D_MODEL // nt),
        in_specs=[
            pl.BlockSpec((8, D_MODEL), lambda l, j: (0, 0)),
            pl.BlockSpec((1, D_MODEL, nt), lambda l, j: (l, 0, j)),
            pl.BlockSpec((1, 1, nt), lambda l, j: (l, 0, j)),
        ],
        out_specs=pl.BlockSpec((1, 8, nt), lambda l, j: (l, 0, j)),
        out_shape=jax.ShapeDtypeStruct((DEPTH, 8, 6 * D_MODEL), F32),
        name="modulation",
    )(cond8, w_mod, b_mod.reshape(DEPTH, 1, 6 * D_MODEL))


def _rope_tables(n_rope, lane0, width):
    half = n_rope // 2
    nf = half // 2
    inv = ROPE_BASE ** (-np.arange(nf, dtype=np.float64) / nf)
    pos = np.arange(GRID_W, dtype=np.float64)[:, None]
    cr = np.zeros((GRID_W, LANES)); cc = np.ones((GRID_W, LANES))
    sr = np.zeros((GRID_W, LANES)); sc = np.zeros((GRID_W, LANES))
    for base in range(0, LANES, width):
        for d in range(n_rope):
            lane = base + lane0 + d
            within = d % half
            ang = pos[:, 0] * inv[within % nf]
            sign = -1.0 if within < nf else 1.0
            if d < half:
                cr[:, lane] = np.cos(ang); cc[:, lane] = 0.0
                sr[:, lane] = sign * np.sin(ang)
            else:
                cc[:, lane] = np.cos(ang)
                sc[:, lane] = sign * np.sin(ang)
    return tuple(jnp.asarray(t, F32) for t in (cr, cc, sr, sc))


def _tile_tables(tabs, row0, n_sub):
    cr_ref, cc_ref, sr_ref, sc_ref = tabs
    cs, ss = [], []
    for j in range(n_sub):
        r = (row0 + j) % GRID_W
        cs.append(cr_ref[pl.ds(r, 1), :] + cc_ref[...])
        ss.append(sr_ref[pl.ds(r, 1), :] + sc_ref[...])
    return jnp.concatenate(cs, axis=0), jnp.concatenate(ss, axis=0)


def _rotate(x, cos_t, sin_t, nf):
    lane = _lane_iota(x.shape)
    up = pltpu.roll(x, LANES - nf, 1)
    dn = pltpu.roll(x, nf, 1)
    partner = jnp.where((lane & nf) == 0, up, dn)
    return x * cos_t + partner * sin_t


def _inproj_kernel(*refs, rope, emit_state, tiles_per_seq, has_cache):
    it = iter(refs)
    if has_cache:
        ckvc_ref, krpc_ref, dkc_ref, dvc_ref = (next(it) for _ in range(4))
    x_ref, mod_ref, g1_ref, w_ref, wa2_ref, ba_ref = (next(it) for _ in range(6))
    gqa_ref, gkva_ref, wuq_ref, wuk_ref, wuv_ref, gq_ref, gk_ref = (next(it) for _ in range(7))
    gdq_ref, gdk_ref = next(it), next(it)
    if rope:
        tab_m = tuple(next(it) for _ in range(4))
        tab_d = tuple(next(it) for _ in range(4))
    o_gq, o_gk, o_gv, o_sr, o_gf, o_gb = (next(it) for _ in range(6))
    o_mq, o_mk, o_mv, o_dq, o_dk, o_dv = (next(it) for _ in range(6))
    if emit_state:
        o_ckv, o_misc, o_dk32, o_dv32 = (next(it) for _ in range(4))

    def mla_keys(kn, krp, dst):
        for hd in range(H_B):
            sl = slice(hd * HEAD_PAD, (hd + 1) * HEAD_PAD)
            kt = kn[:, sl] + krp
            kt = kt * lax.rsqrt(jnp.sum(kt * kt, axis=-1, keepdims=True) * (1.0 / QK_B) + EPS) * gk_ref[...]
            dst[0, :, sl] = kt.astype(BF16)

    def with_ones(v):
        one_tile = _ones_lane0((v.shape[0], LANES))
        return jnp.concatenate(
            [t for hd in range(H_C) for t in (v[:, hd * LANES:(hd + 1) * LANES], one_tile)], axis=1).astype(BF16)

    def cache_rows():
        c = ckvc_ref[0].astype(BF16)
        vb = _dot(c, wuv_ref[...])
        o_mv[0] = jnp.where(_lane_iota(vb.shape) % HEAD_PAD == V_B, 1.0, vb).astype(BF16)
        mla_keys(_dot(c, wuk_ref[...]), krpc_ref[0], o_mk)
        o_dk[0] = dkc_ref[0].astype(BF16)
        o_dv[0] = with_ones(dvc_ref[0])

    def token_rows(seq_tile):
        tm = x_ref.shape[0]
        x = x_ref[...]
        m = mod_ref[0]
        h = _rms(x, g1_ref[...]) * (1.0 + m[1:2]) + m[0:1]
        y = _dot(h.astype(BF16), w_ref[...])

        o_gq[...] = y[:, C_AQ:C_AQ + 256] * (DK_A ** -0.5)
        o_gk[...] = y[:, C_AK:C_AK + 256]
        o_gv[...] = y[:, C_AV:C_AV + 512].astype(BF16)
        r = y[:, C_AR:C_AR + 512]
        o_sr[...] = r * _sigmoid(r)
        misc = y[:, C_MISC:C_MISC + LANES]
        logits = _dot(misc.astype(BF16), wa2_ref[...]) + ba_ref[...]
        logsig = jnp.minimum(logits, 0.0) - jnp.log(1.0 + jnp.exp(-jnp.abs(logits)))
        gate = logsig * (1.0 / GATE_TAU)
        o_gf[...] = gate[:, 0:256]
        o_gb[...] = gate[:, 256:512]

        if rope:
            row0 = seq_tile * (tm // GRID_W)
            cos_m, sin_m = _tile_tables(tab_m, row0, tm // GRID_W)
            cos_d, sin_d = _tile_tables(tab_d, row0, tm // GRID_W)

        qn = _rms(y[:, C_QD:C_QD + Q_RANK], gqa_ref[...])
        q8 = _dot(qn.astype(BF16), wuq_ref[...])
        ckv = _rms(y[:, C_KVD:C_KVD + KV_RANK], gkva_ref[...])
        kn = _dot(ckv.astype(BF16), wuk_ref[...])
        vb = _dot(ckv.astype(BF16), wuv_ref[...])
        o_mv[0] = jnp.where(_lane_iota(vb.shape) % HEAD_PAD == V_B, 1.0, vb).astype(BF16)
        krp = y[:, C_KRP:C_KRP + LANES]
        for hd in range(H_B):
            sl = slice(hd * HEAD_PAD, (hd + 1) * HEAD_PAD)
            qt = q8[:, sl]
            qt = qt * lax.rsqrt(jnp.sum(qt * qt, axis=-1, keepdims=True) * (1.0 / QK_B) + EPS) * gq_ref[...]
            kt = kn[:, sl] + krp
            kt = kt * lax.rsqrt(jnp.sum(kt * kt, axis=-1, keepdims=True) * (1.0 / QK_B) + EPS) * gk_ref[...]
            if rope:
                qt = _rotate(qt, cos_m, sin_m, ROPE_B // 4)
                kt = _rotate(kt, cos_m, sin_m, ROPE_B // 4)
            o_mq[:, sl] = (qt * (QK_B ** -0.5 * LOG2E)).astype(BF16)
            o_mk[0, :, sl] = kt.astype(BF16)

        lane = _lane_iota((tm, LANES))
        lo = lane < D_C
        for hd in range(H_C):
            sl = slice(hd * LANES, (hd + 1) * LANES)
            for src, g_ref, dst, dst32, scale in (
                (C_DQ, gdq_ref, o_dq, None, D_C ** -0.5 * LOG2E),
                (C_DK, gdk_ref, o_dk.at[0], o_dk32 if emit_state else None, 1.0),
            ):
                t = y[:, src + hd * LANES:src + (hd + 1) * LANES]
                sq = t * t
                s_lo = jnp.sum(jnp.where(lo, sq, 0.0), axis=-1, keepdims=True)
                s_hi = jnp.sum(jnp.where(lo, 0.0, sq), axis=-1, keepdims=True)
                inv = jnp.where(lo, lax.rsqrt(s_lo * (1.0 / D_C) + EPS), lax.rsqrt(s_hi * (1.0 / D_C) + EPS))
                t = t * inv * g_ref[...]
                if rope:
                    t = _rotate(t, cos_d, sin_d, D_C // 4)
                if dst32 is not None:
                    dst32[:, sl] = t
                dst[:, sl] = (t * scale).astype(BF16)
        dv = y[:, C_DV:C_DV + W_C]
        o_dv[0] = with_ones(dv)
        if emit_state:
            o_dv32[...] = dv
            o_ckv[...] = ckv
            o_misc[...] = misc

    if not has_cache:
        token_rows(pl.program_id(0) % tiles_per_seq)
        return
    step = pl.program_id(0) % (tiles_per_seq + 1)
    pl.when(step == 0)(cache_rows)
    pl.when(step > 0)(lambda: token_rows(step - 1))


def _inproj(x2, bsz, mod8, group0, lw, tabs, cache, *, rope, emit_state):
    t = x2.shape[0]
    tm = TOKEN_TILE
    seq = t // bsz
    tps = seq // tm
    has_cache = cache is not None
    past = cache[0].shape[1] if has_cache else 0
    assert past in (0, tm)
    lead = 1 if has_cache else 0
    req = lambda i: i // (tps + lead)
    tile = lambda i: jnp.maximum(i % (tps + lead) - lead, 0)
    kv_rows = lambda w: pl.BlockSpec((1, tm, w), lambda i: (req(i), i % (tps + lead), 0))
    const = lambda shape: pl.BlockSpec(shape, lambda i: (0,) * len(shape))
    row = lambda w: pl.BlockSpec((tm, w), lambda i: (req(i) * tps + tile(i), 0))
    if rope:
        mod_spec = pl.BlockSpec((1, 8, D_MODEL), lambda i: (group0 + req(i), 0, 0))
    else:
        mod_spec = pl.BlockSpec((1, 8, D_MODEL), lambda i: (group0, 0, 0))
    cache_specs = [pl.BlockSpec((1, past, c.shape[-1]), lambda i: (req(i), 0, 0)) for c in cache] if has_cache else []
    in_specs = cache_specs + [
        row(D_MODEL), mod_spec, const((1, D_MODEL)), const((D_MODEL, IN_COLS_A)),
        const((LANES, 512)), const((1, 512)),
        const((1, Q_RANK)), const((1, KV_RANK)), const((Q_RANK, H_B * HEAD_PAD)),
        const((KV_RANK, H_B * HEAD_PAD)), const((KV_RANK, H_B * HEAD_PAD)), const((1, LANES)), const((1, LANES)),
        const((1, LANES)), const((1, LANES)),
    ]
    args = (list(cache) if has_cache else []) + [
        x2, mod8, lw["g1"], lw["w_in_a"], lw["wa2"], lw["ba"], lw["gqa"], lw["gkva"], lw["wuq"],
        lw["wuk"], lw["wuv"], lw["gq"], lw["gk"], lw["gdq"], lw["gdk"]]
    if rope:
        in_specs += [const((GRID_W, LANES))] * 8
        args += list(tabs[0]) + list(tabs[1])
    flat, keyed = "flat", "keyed"
    outs = [(256, F32, flat), (256, F32, flat), (512, BF16, flat), (512, F32, flat), (256, F32, flat),
            (256, F32, flat), (H_B * HEAD_PAD, BF16, flat), (H_B * HEAD_PAD, BF16, keyed),
            (H_B * HEAD_PAD, BF16, keyed), (W_C, BF16, flat), (W_C, BF16, keyed), (2 * W_C, BF16, keyed)]
    if emit_state:
        outs += [(KV_RANK, F32, flat), (LANES, F32, flat), (W_C, F32, flat), (W_C, F32, flat)]
    return pl.pallas_call(
        functools.partial(_inproj_kernel, rope=rope, emit_state=emit_state, tiles_per_seq=tps,
                          has_cache=has_cache),
        grid=(bsz * (tps + lead),),
        in_specs=in_specs,
        out_specs=[row(w) if kind == flat else kv_rows(w) for w, _, kind in outs],
        out_shape=[jax.ShapeDtypeStruct((t, w) if kind == flat else (bsz, past + seq, w), dt)
                   for w, dt, kind in outs],
        compiler_params=pltpu.CompilerParams(dimension_semantics=("arbitrary",), vmem_limit_bytes=VMEM_LIMIT),
        name="inproj_lat" if rope else "inproj_ctx",
    )(*args)


def _gla_constants(rev):
    c = CHUNK
    idx = np.arange(c)
    t, u = idx[:, None], idx[None, :]
    mats = [(u <= t)]
    masks = [np.eye(c, dtype=bool)]
    hs = c // 2
    while hs >= 1:
        blk = idx // (2 * hs)
        lower = (idx % (2 * hs)) < hs
        p = blk * 2 * hs + hs - 1
        m_low = lower[:, None] & (u > t) & (u <= p[:, None])
        m_up = (~lower)[:, None] & (u > p[:, None]) & (u <= t)
        mats.append(m_low | m_up)
        masks.append((blk[:, None] == blk[None, :]) & (~lower)[:, None] & lower[None, :])
        hs //= 2
    mats.append(u > t)
    mats = [m.astype(np.float32) for m in mats]
    masks = [m.astype(np.float32) for m in masks]
    if rev:
        mats = [m[::-1, ::-1] for m in mats]
        masks = [m[::-1, ::-1] for m in masks]
    cm = np.concatenate(mats, axis=0)
    cms = np.concatenate([cm, cm], axis=1)
    lm = np.stack([np.tile(m, (1, H_A)) for m in masks])
    same = lambda n_row, n_col: (np.arange(H_A * n_row)[:, None] // n_row
                                 == np.arange(H_A * n_col)[None, :] // n_col).astype(np.float32)
    km = same(c, DK_A)
    vm = same(c, DV_A)
    sm = same(DV_A, DK_A)
    return (jnp.asarray(cms, BF16), jnp.asarray(lm, F32)), (jnp.asarray(km, BF16), jnp.asarray(vm, BF16),
                                                           jnp.asarray(sm, BF16))


N_LEVEL = 6


def _gla_chunk(q, k, v16, g, st, cms_ref, lm_ref, km_ref, vm_ref, sm_ref, rev):
    def per_head_rows(x):
        return jnp.concatenate([x.astype(BF16)] * H_A, axis=0) * km_ref[...]

    g_hi = g.astype(BF16)
    g_lo = (g - g_hi.astype(F32)).astype(BF16)
    sums = _dot(cms_ref[...], jnp.concatenate([g_hi, g_lo], axis=0))
    b = sums[0:CHUNK]
    att = lm_ref[0] * _dot_nt(q.astype(BF16), per_head_rows(k))
    for lv in range(1, N_LEVEL + 1):
        e = jnp.exp(sums[lv * CHUNK:(lv + 1) * CHUNK])
        att = att + lm_ref[lv] * _dot_nt((q * e).astype(BF16), per_head_rows(k * e))
    v_rows = jnp.concatenate([v16] * H_A, axis=0) * vm_ref[...]
    qe = (q * jnp.exp(b)).astype(BF16)
    out = _dot(att.astype(BF16), v_rows) + _dot_nt(qe, st.astype(BF16) * sm_ref[...])
    ke = (k * jnp.exp(sums[(N_LEVEL + 1) * CHUNK:(N_LEVEL + 2) * CHUNK])).astype(BF16)
    upd = pl.dot(v16, ke, trans_a=True)
    b_end = b[0:1] if rev else b[CHUNK - 1:CHUNK]
    return out, st * jnp.exp(b_end) + upd


def _gla_kernel(*refs, has_init, emit_state):
    it = iter(refs)
    qf_ref, kf_ref, vf_ref, gf_ref, qb_ref, kb_ref, vb_ref, gb_ref = (next(it) for _ in range(8))
    if has_init:
        s0f_ref, s0b_ref = next(it), next(it)
    cmsf_ref, lmf_ref, cmsb_ref, lmb_ref, km_ref, vm_ref, sm_ref = (next(it) for _ in range(7))
    of_ref, ob_ref = next(it), next(it)
    if emit_state:
        sff_ref, sfb_ref = next(it), next(it)
    stf_ref, stb_ref = next(it), next(it)
    ti = pl.program_id(1)
    nt = pl.num_programs(1)

    @pl.when(ti == 0)
    def _():
        if has_init:
            stf_ref[...] = s0f_ref[0]
            stb_ref[...] = s0b_ref[0]
        else:
            stf_ref[...] = jnp.zeros(stf_ref.shape, F32)
            stb_ref[...] = jnp.zeros(stb_ref.shape, F32)

    n_chunk = qf_ref.shape[1] // CHUNK
    stf = stf_ref[...]
    stb = stb_ref[...]
    for ci in range(n_chunk):
        rows = slice(ci * CHUNK, (ci + 1) * CHUNK)
        out, stf = _gla_chunk(qf_ref[0, rows, :], kf_ref[0, rows, :], vf_ref[0, rows, :], gf_ref[0, rows, :], stf,
                              cmsf_ref, lmf_ref, km_ref, vm_ref, sm_ref, False)
        of_ref[0, rows, :] = out
        cb = n_chunk - 1 - ci
        rows = slice(cb * CHUNK, (cb + 1) * CHUNK)
        out, stb = _gla_chunk(qb_ref[0, rows, :], kb_ref[0, rows, :], vb_ref[0, rows, :], gb_ref[0, rows, :], stb,
                              cmsb_ref, lmb_ref, km_ref, vm_ref, sm_ref, True)
        ob_ref[0, rows, :] = out
    stf_ref[...] = stf
    stb_ref[...] = stb

    if emit_state:
        @pl.when(ti == nt - 1)
        def _():
            sff_ref[0] = stf_ref[...]
            sfb_ref[0] = stb_ref[...]


def _gla_scan(q3, k3, v3, gf3, gb3, init, consts, *, emit_state):
    bsz, s, _ = q3.shape
    tt = GLA_TILE
    nt = s // tt
    fmap = lambda b, t: (b, t, 0)
    rmap = lambda b, t: (b, nt - 1 - t, 0)
    bmap = lambda b, t: (b, 0, 0)
    const = lambda shape: pl.BlockSpec(shape, lambda b, t: (0,) * len(shape))
    tile = lambda w, m: pl.BlockSpec((1, tt, w), m)
    state = pl.BlockSpec((1, H_A * DV_A, H_A * DK_A), bmap)
    in_specs = [tile(256, fmap), tile(256, fmap), tile(512, fmap), tile(256, fmap),
                tile(256, rmap), tile(256, rmap), tile(512, rmap), tile(256, rmap)]
    args = [q3, k3, v3, gf3, q3, k3, v3, gb3]
    if init is not None:
        in_specs += [state, state]
        args += list(init)
    in_specs += [const(c.shape) for c in consts]
    args += list(consts)
    out_specs = [tile(W_A, fmap), tile(W_A, rmap)]
    out_shape = [jax.ShapeDtypeStruct((bsz, s, W_A), F32)] * 2
    if emit_state:
        out_specs += [state, state]
        out_shape += [jax.ShapeDtypeStruct((bsz, H_A * DV_A, H_A * DK_A), F32)] * 2
    return pl.pallas_call(
        functools.partial(_gla_kernel, has_init=init is not None, emit_state=emit_state),
        grid=(bsz, nt),
        in_specs=in_specs,
        out_specs=out_specs,
        out_shape=out_shape,
        scratch_shapes=[pltpu.VMEM((H_A * DV_A, H_A * DK_A), F32)] * 2,
        compiler_params=pltpu.CompilerParams(dimension_semantics=("arbitrary", "arbitrary"),
                                             vmem_limit_bytes=VMEM_LIMIT),
        name="gla_lat" if init is not None else "gla_ctx",
    )(*args)


def _state_to_blocks(s0):
    st = jnp.swapaxes(s0, -1, -2)
    eye = jnp.eye(H_A, dtype=s0.dtype)
    full = st[:, :, :, None, :] * eye[None, :, None, :, None]
    return full.reshape(s0.shape[0], H_A * DV_A, H_A * DK_A)


def _blocks_to_state(sf):
    bsz = sf.shape[0]
    full = sf.reshape(bsz, H_A, DV_A, H_A, DK_A)
    diag = jnp.stack([full[:, h, :, h, :] for h in range(H_A)], axis=1)
    return jnp.swapaxes(diag, -1, -2)


def _online_step(carry, q, kt, vt):
    m, acc = carry
    s = _dot_nt(q, kt)
    m_new = jnp.maximum(m, jnp.max(s, axis=-1, keepdims=True))
    p = jnp.exp2(s - m_new)
    acc = jnp.exp2(m - m_new) * acc + _dot(p.astype(BF16), vt)
    return m_new, acc


def _key_rounds(n_keys, n_rounds, align):
    units = n_keys // align
    assert units * align == n_keys
    n_rounds = 1 if n_keys <= 2 * MXU_WIDTH else min(n_rounds, units)
    sizes = [(units // n_rounds + (1 if r < units % n_rounds else 0)) * align for r in range(n_rounds)]
    starts = np.concatenate([[0], np.cumsum(sizes)[:-1]])
    return [(int(s), int(z)) for s, z in zip(starts, sizes)]


def _attend(qs, k_ref, v_ref, ksls, vsls, n_rounds, align):
    tq = qs[0].shape[0]
    n = len(qs)
    width = vsls[0].stop - vsls[0].start
    init = (jnp.full((tq, 1), NEG_BIG, F32), jnp.zeros((tq, width), F32))
    carries = [init for _ in range(n)]
    for start, size in _key_rounds(k_ref.shape[1], n_rounds, align):
        rows = slice(start, start + size)
        carries = [_online_step(carries[i], qs[i], k_ref[0, rows, ksls[i]], v_ref[0, rows, vsls[i]])
                   for i in range(n)]
    return [c[1] for c in carries]


def _mla_kernel(q_ref, k_ref, v_ref, o_ref):
    sls = [slice(hh * HEAD_PAD, (hh + 1) * HEAD_PAD) for hh in range(2)]
    accs = _attend([q_ref[0, :, sl] for sl in sls], k_ref, v_ref, sls, sls, MLA_ROUNDS, MXU_WIDTH)
    outs = [acc / acc[:, V_B:V_B + 1] for acc in accs]
    lane = _lane_iota(outs[0].shape)
    o_ref[0] = jnp.where(lane < V_B, outs[0], pltpu.roll(outs[1], V_B, 1)).astype(o_ref.dtype)


def _mla_attention(q3, k3, v3):
    bsz, s, _ = q3.shape
    keys = k3.shape[1]
    tq = min(Q_TILE, s)
    pair = 2 * HEAD_PAD
    return pl.pallas_call(
        _mla_kernel,
        grid=(bsz, H_B // 2, s // tq),
        in_specs=[
            pl.BlockSpec((1, tq, pair), lambda b, h, i: (b, i, h)),
            pl.BlockSpec((1, keys, pair), lambda b, h, i: (b, 0, h)),
            pl.BlockSpec((1, keys, pair), lambda b, h, i: (b, 0, h)),
        ],
        out_specs=pl.BlockSpec((1, tq, LANES), lambda b, h, i: (b, i, h)),
        out_shape=jax.ShapeDtypeStruct((bsz, s, W_B), BF16),
        compiler_params=pltpu.CompilerParams(dimension_semantics=("arbitrary",) * 3,
                                             vmem_limit_bytes=VMEM_LIMIT),
        name="mla_lat" if keys > s else "mla_ctx",
    )(q3, k3, v3)


def _diff_kernel(lam_ref, q_ref, k_ref, v_ref, o_ref, *, lam_init):
    lq = lam_ref[...]
    lam = (jnp.exp(jnp.sum(lq[0:1] * lq[1:2], axis=-1, keepdims=True))
           - jnp.exp(jnp.sum(lq[2:3] * lq[3:4], axis=-1, keepdims=True)) + lam_init)
    q = q_ref[0].astype(F32)
    lane = _lane_iota(q.shape)
    full, wide = slice(0, LANES), slice(0, 2 * LANES)
    qcs = [jnp.where(lane < D_C, q, 0.0).astype(BF16), jnp.where(lane >= D_C, q, 0.0).astype(BF16)]
    accs = _attend(qcs, k_ref, v_ref, [full, full], [wide, wide], DIFF_ROUNDS, LANES)
    res = [acc[:, 0:LANES] / acc[:, LANES:LANES + 1] for acc in accs]
    o_ref[0] = res[0] - lam * res[1]


def _diff_attention(lam_qk, q3, k3, v3, lam_init):
    bsz, s, _ = q3.shape
    keys = k3.shape[1]
    tq = min(Q_TILE, s)
    return pl.pallas_call(
        functools.partial(_diff_kernel, lam_init=lam_init),
        grid=(bsz, H_C, s // tq),
        in_specs=[
            pl.BlockSpec((4, D_C), lambda b, h, i: (0, 0)),
            pl.BlockSpec((1, tq, LANES), lambda b, h, i: (b, i, h)),
            pl.BlockSpec((1, keys, LANES), lambda b, h, i: (b, 0, h)),
            pl.BlockSpec((1, keys, 2 * LANES), lambda b, h, i: (b, 0, h)),
        ],
        out_specs=pl.BlockSpec((1, tq, LANES), lambda b, h, i: (b, i, h)),
        out_shape=jax.ShapeDtypeStruct((bsz, s, W_C), F32),
        compiler_params=pltpu.CompilerParams(dimension_semantics=("arbitrary",) * 3,
                                             vmem_limit_bytes=VMEM_LIMIT),
        name="diff_lat" if keys > s else "diff_ctx",
    )(lam_qk, q3, k3, v3)


def _merge_kernel(x_ref, mod_ref, g1_ref, wg_ref, of_ref, ob_ref, sr_ref, yb_ref, yc_ref, ggla_ref, gsub_ref,
                  woa_ref, wob_ref, woc_ref, wout_ref, o_ref, *, lam_init):
    x = x_ref[...]
    m = mod_ref[0]
    h = _rms(x, g1_ref[...]) * (1.0 + m[1:2]) + m[0:1]
    gates = _sigmoid(_dot(h.astype(BF16), wg_ref[...]))
    o = of_ref[...] + ob_ref[...]
    sr = sr_ref[...]
    yc = yc_ref[...]
    ya_t, yc_t = [], []
    for hd in range(H_A):
        sl = slice(hd * LANES, (hd + 1) * LANES)
        ya_t.append(_rms(o[:, sl], ggla_ref[...]) * sr[:, sl])
        yc_t.append(_rms(yc[:, sl], gsub_ref[...]) * (1.0 - lam_init))
    ya = jnp.concatenate(ya_t, axis=1).astype(BF16)
    ycn = jnp.concatenate(yc_t, axis=1).astype(BF16)
    merged = (gates[:, 0:D_MODEL] * _dot(ya, woa_ref[...])
              + gates[:, D_MODEL:2 * D_MODEL] * _dot(yb_ref[...], wob_ref[...])
              + gates[:, 2 * D_MODEL:3 * D_MODEL] * _dot(ycn, woc_ref[...]))
    o_ref[...] = x + m[2:3] * _dot(merged.astype(BF16), wout_ref[...])


def _merge(x2, mod8, group0, lw, of2, ob2, sr2, yb2, yc2, lam_init, per_seq_groups):
    t = x2.shape[0]
    tm = TOKEN_TILE
    tiles_per_seq = LAT_SEQ // tm
    const = lambda shape: pl.BlockSpec(shape, lambda i: (0,) * len(shape))
    row = lambda w: pl.BlockSpec((tm, w), lambda i: (i, 0))
    if per_seq_groups:
        mod_spec = pl.BlockSpec((1, 8, D_MODEL), lambda i: (group0 + i // tiles_per_seq, 0, 0))
    else:
        mod_spec = pl.BlockSpec((1, 8, D_MODEL), lambda i: (group0, 0, 0))
    return pl.pallas_call(
        functools.partial(_merge_kernel, lam_init=lam_init),
        grid=(t // tm,),
        in_specs=[row(D_MODEL), mod_spec, const((1, D_MODEL)), const((D_MODEL, GATE_COLS)),
                  row(W_A), row(W_A), row(W_A), row(W_B), row(W_C), const((1, LANES)), const((1, LANES)),
                  const((W_A, D_MODEL)), const((W_B, D_MODEL)), const((W_C, D_MODEL)),
                  const((D_MODEL, D_MODEL))],
        out_specs=row(D_MODEL),
        out_shape=jax.ShapeDtypeStruct((t, D_MODEL), F32),
        compiler_params=pltpu.CompilerParams(dimension_semantics=("arbitrary",), vmem_limit_bytes=VMEM_LIMIT),
        name="merge",
    )(x2, mod8, lw["g1"], lw["w_gates"], of2, ob2, sr2, yb2, yc2, lw["ggla"], lw["gsub"],
      lw["woa"], lw["wob"], lw["woc"], lw["wout"])


def _mlp_kernel(x_ref, mod_ref, g2_ref, w1_ref, w2_ref, o_ref):
    x = x_ref[...]
    m = mod_ref[0]
    h = _rms(x, g2_ref[...]) * (1.0 + m[4:5]) + m[3:4]
    a = jnp.maximum(_dot(h.astype(BF16), w1_ref[...]), 0.0)
    o_ref[...] = x + m[5:6] * _dot((a * a).astype(BF16), w2_ref[...])


def _mlp(x2, mod8, group0, lw, per_seq_groups):
    t = x2.shape[0]
    tm = TOKEN_TILE
    tiles_per_seq = LAT_SEQ // tm
    const = lambda shape: pl.BlockSpec(shape, lambda i: (0,) * len(shape))
    row = lambda w: pl.BlockSpec((tm, w), lambda i: (i, 0))
    if per_seq_groups:
        mod_spec = pl.BlockSpec((1, 8, D_MODEL), lambda i: (group0 + i // tiles_per_seq, 0, 0))
    else:
        mod_spec = pl.BlockSpec((1, 8, D_MODEL), lambda i: (group0, 0, 0))
    return pl.pallas_call(
        _mlp_kernel,
        grid=(t // tm,),
        in_specs=[row(D_MODEL), mod_spec, const((1, D_MODEL)), const((D_MODEL, D_FF)), const((D_FF, D_MODEL))],
        out_specs=row(D_MODEL),
        out_shape=jax.ShapeDtypeStruct((t, D_MODEL), F32),
        compiler_params=pltpu.CompilerParams(dimension_semantics=("arbitrary",), vmem_limit_bytes=VMEM_LIMIT),
        name="mlp",
    )(x2, mod8, lw["g2"], lw["w1"], lw["w2"])


def _pad_heads(w, real, pad):
    k = w.shape[0]
    w = w.reshape(k, -1, real)
    return jnp.pad(w, ((0, 0), (0, 0), (0, pad - real))).reshape(k, -1)


def _layer_weights(l, p):
    widths = (H_A * DK_A, H_A * DK_A, W_A, W_A, 2 * GATE_RANK, Q_RANK, KV_RANK, ROPE_B, W_C, W_C, W_C, GATE_COLS)
    offs = np.concatenate([[0], np.cumsum(widths)])
    w_in = p["w_in"][l]
    aq, ak, av, ar, aa, qd, kvd, kr, dq, dk, dv, gates = (w_in[:, offs[i]:offs[i + 1]] for i in range(12))
    z = lambda n: jnp.zeros((D_MODEL, n), F32)
    misc = jnp.concatenate([aa, kr, z(LANES - 2 * GATE_RANK - ROPE_B)], axis=1)
    krp = jnp.concatenate([z(NOPE_B), kr, z(LANES - QK_B)], axis=1)
    w_in_a = jnp.concatenate([aq, ak, av, ar, qd, kvd, dq, dk, dv, misc, krp], axis=1).astype(BF16)
    wa2 = jnp.zeros((LANES, 512), F32)
    wa2 = wa2.at[0:GATE_RANK, 0:256].set(p["w_gla_a2"][l, 0])
    wa2 = wa2.at[GATE_RANK:2 * GATE_RANK, 256:512].set(p["w_gla_a2"][l, 1])
    pad_gain = lambda g: jnp.pad(g, (0, HEAD_PAD - QK_B)).reshape(1, HEAD_PAD)
    twice = lambda g: jnp.concatenate([g, g]).reshape(1, LANES)
    return dict(
        g1=p["g_norm1"][l].reshape(1, D_MODEL), g2=p["g_norm2"][l].reshape(1, D_MODEL),
        w_in_a=w_in_a, w_gates=gates.astype(BF16),
        wa2=wa2.astype(BF16), ba=p["b_gla_a"][l].reshape(1, 512),
        gqa=p["g_mla_qa"][l].reshape(1, Q_RANK), gkva=p["g_mla_kva"][l].reshape(1, KV_RANK),
        wuq=_pad_heads(p["w_mla_uq"][l], QK_B, HEAD_PAD).astype(BF16),
        wuk=_pad_heads(p["w_mla_uk"][l], NOPE_B, HEAD_PAD).astype(BF16),
        wuv=_pad_heads(p["w_mla_uv"][l], V_B, HEAD_PAD).astype(BF16),
        gq=pad_gain(p["g_mla_q"][l]), gk=pad_gain(p["g_mla_k"][l]),
        gdq=twice(p["g_diff_q"][l]), gdk=twice(p["g_diff_k"][l]),
        ggla=p["g_gla_out"][l].reshape(1, DV_A), gsub=p["g_diff_sub"][l].reshape(1, 2 * D_C),
        woa=p["w_o_gla"][l].astype(BF16), wob=p["w_o_mla"][l].astype(BF16), woc=p["w_o_diff"][l].astype(BF16),
        wout=p["w_out"][l].astype(BF16), w1=p["w_mlp1"][l].astype(BF16), w2=p["w_mlp2"][l].astype(BF16),
        lam_qk=p["lam_qk"][l],
    )


def _layer(x2, bsz, seq, mod8, group0, lw, lam_init, consts, tabs, cached):
    is_latent = cached is not None
    cache = None
    if is_latent:
        s0, ckv_c, krope_c, dk_c, dv_c = cached
        past = ckv_c.shape[1]
        krp_c = jnp.pad(krope_c, ((0, 0), (0, 0), (NOPE_B, LANES - QK_B)))
        cache = (ckv_c, krp_c, dk_c.reshape(bsz, past, W_C), dv_c.reshape(bsz, past, W_C))
    outs = _inproj(x2, bsz, mod8, group0, lw, tabs, cache, rope=is_latent, emit_state=not is_latent)
    gq, gk, gv, sr, gf, gb, mq, mk, mv, dq, dk, dv = outs[:12]
    r3 = lambda a: a.reshape(bsz, seq, a.shape[-1])

    init = (_state_to_blocks(s0[:, 0]), _state_to_blocks(s0[:, 1])) if is_latent else None
    gla = _gla_scan(r3(gq), r3(gk), r3(gv), r3(gf), r3(gb), init, consts, emit_state=not is_latent)
    of, ob = gla[0], gla[1]

    yb = _mla_attention(r3(mq), mk, mv)
    yc = _diff_attention(lw["lam_qk"], r3(dq), dk, dv, lam_init)

    x1 = _merge(x2, mod8, group0, lw, of.reshape(-1, W_A), ob.reshape(-1, W_A), sr, yb.reshape(-1, W_B),
                yc.reshape(-1, W_C), lam_init, is_latent)
    x_out = _mlp(x1, mod8, group0, lw, is_latent)

    new_state = None
    if not is_latent:
        ckv, misc, dk32, dv32 = outs[12:]
        gla_state = jnp.stack([_blocks_to_state(gla[2]), _blocks_to_state(gla[3])], axis=1)
        new_state = (gla_state, ckv.reshape(bsz, seq, KV_RANK),
                     misc[:, 2 * GATE_RANK:2 * GATE_RANK + ROPE_B].reshape(bsz, seq, ROPE_B),
                     dk32.reshape(bsz, seq, H_C, 2, D_C), dv32.reshape(bsz, seq, H_C, 2 * D_C))
    return x_out, new_state


def kernel(x_prompt, x_sample, state_gla, cache_mla_ckv, cache_mla_krope, cache_diff_k, cache_diff_v, c, c_ctx, w_mod, b_mod, g_norm1, g_norm2, w_in, w_gla_a2, b_gla_a, g_gla_out, g_mla_qa, g_mla_kva, w_mla_uq, w_mla_uk, w_mla_uv, g_mla_q, g_mla_k, g_diff_q, g_diff_k, lam_qk, g_diff_sub, w_o_gla, w_o_mla, w_o_diff, w_out, w_mlp1, w_mlp2):
    params = dict(w_in=w_in, g_norm1=g_norm1, g_norm2=g_norm2, w_gla_a2=w_gla_a2, b_gla_a=b_gla_a,
                  g_gla_out=g_gla_out, g_mla_qa=g_mla_qa, g_mla_kva=g_mla_kva, w_mla_uq=w_mla_uq,
                  w_mla_uk=w_mla_uk, w_mla_uv=w_mla_uv, g_mla_q=g_mla_q, g_mla_k=g_mla_k, g_diff_q=g_diff_q,
                  g_diff_k=g_diff_k, lam_qk=lam_qk, g_diff_sub=g_diff_sub, w_o_gla=w_o_gla, w_o_mla=w_o_mla,
                  w_o_diff=w_o_diff, w_out=w_out, w_mlp1=w_mlp1, w_mlp2=w_mlp2)
    nb, ns, _ = x_prompt.shape
    db, ds, _ = x_sample.shape
    assert ds == LAT_SEQ and db + 1 <= 8

    cond8 = jnp.concatenate([c_ctx[None, :], c, jnp.zeros((8 - 1 - db, D_MODEL), F32)], axis=0)
    mod = _modulation(cond8, w_mod, b_mod).reshape(DEPTH, 8, 6, D_MODEL)
    mod = jnp.pad(mod, ((0, 0), (0, 0), (0, 2), (0, 0)))

    (fwd_c, head_masks), (bwd_c, _) = _gla_constants(False), _gla_constants(True)
    consts = fwd_c + bwd_c + head_masks
    tabs = (_rope_tables(ROPE_B, NOPE_B, LANES), _rope_tables(D_C, 0, D_C))

    yp = x_prompt.reshape(nb * ns, D_MODEL)
    ys = x_sample.reshape(db * ds, D_MODEL)
    states = []
    for l in range(DEPTH):
        lw = _layer_weights(l, params)
        lam_init = 0.8 - 0.6 * math.exp(-0.3 * l)
        yp, st = _layer(yp, nb, ns, mod[l], 0, lw, lam_init, consts, tabs, None)
        states.append(st)
        cached = (state_gla[:, l], cache_mla_ckv[:, l], cache_mla_krope[:, l], cache_diff_k[:, l],
                  cache_diff_v[:, l])
        ys, _ = _layer(ys, db, ds, mod[l], 1, lw, lam_init, consts, tabs, cached)
    stack = lambda i: jnp.stack([s[i] for s in states], axis=1)
    return (yp.reshape(nb, ns, D_MODEL), ys.reshape(db, ds, D_MODEL), stack(0), stack(1), stack(2), stack(3),
            stack(4))
```

```python
import functools
import math

import numpy as np
import jax
import jax.numpy as jnp
from jax import lax
from jax.experimental import pallas as pl
from jax.experimental.pallas import tpu as pltpu

F32 = jnp.float32
BF16 = jnp.bfloat16

D_MODEL = 1024
DEPTH = 2
GRID_W = 64
ROPE_BASE = 10000.0
EPS = 1e-6
H_A, DK_A, DV_A = 4, 64, 128
GATE_RANK = 16
GATE_TAU = 16.0
CHUNK = 64
W_A = H_A * DV_A
H_B, Q_RANK, KV_RANK = 8, 384, 256
NOPE_B, ROPE_B, V_B = 64, 32, 64
QK_B = NOPE_B + ROPE_B
W_B = H_B * V_B
H_C, D_C = 4, 64
W_C = H_C * 2 * D_C
D_FF = 4 * D_MODEL
GATE_COLS = 3 * D_MODEL
LAT_SEQ = 4096

LANES = 128
HEAD_PAD = LANES
MXU_WIDTH = 256
VMEM_LIMIT = 56 * 1024 * 1024

TOKEN_TILE = 256
Q_TILE = 1024
MLA_ROUNDS = 2
DIFF_ROUNDS = 2
GLA_TILE = 256

C_AQ, C_AK, C_AV, C_AR = 0, 256, 512, 1024
C_QD, C_KVD = 1536, 1920
C_DQ, C_DK, C_DV = 2176, 2688, 3200
C_MISC, C_KRP = 3712, 3840
IN_COLS_A = 3968

NEG_BIG = -1e30
LOG2E = math.log2(math.e)


def _dot(a, b):
    return jnp.dot(a, b, preferred_element_type=F32)


def _dot_nt(a, b):
    return lax.dot_general(a, b, (((1,), (1,)), ((), ())), preferred_element_type=F32)


def _sigmoid(x):
    return 1.0 / (1.0 + jnp.exp(-x))


def _rms(x, g):
    ms = jnp.mean(x * x, axis=-1, keepdims=True)
    return x * lax.rsqrt(ms + EPS) * g


def _lane_iota(shape):
    return lax.broadcasted_iota(jnp.int32, shape, len(shape) - 1)


def _ones_lane0(shape):
    return jnp.where(_lane_iota(shape) % LANES == 0, 1.0, 0.0).astype(F32)


def _mod_kernel(c_ref, w_ref, b_ref, o_ref):
    c = c_ref[...]
    s = c * _sigmoid(c)
    o_ref[0] = _dot(s.astype(BF16), w_ref[0].astype(BF16)) + b_ref[0]


def _modulation(cond8, w_mod, b_mod):
    nt = 768
    return pl.pallas_call(
        _mod_kernel,
        grid=(DEPTH, 6 * D_MODEL // nt),
        in_specs=[
            pl.BlockSpec((8, D_MODEL), lambda l, j: (0, 0)),
            pl.BlockSpec((1, D_MODEL, nt), lambda l, j: (l, 0, j)),
            pl.BlockSpec((1, 1, nt), lambda l, j: (l, 0, j)),
        ],
        out_specs=pl.BlockSpec((1, 8, nt), lambda l, j: (l, 0, j)),
        out_shape=jax.ShapeDtypeStruct((DEPTH, 8, 6 * D_MODEL), F32),
        name="modulation",
    )(cond8, w_mod, b_mod.reshape(DEPTH, 1, 6 * D_MODEL))


def _rope_tables(n_rope, lane0, width):
    half = n_rope // 2
    nf = half // 2
    inv = ROPE_BASE ** (-np.arange(nf, dtype=np.float64) / nf)
    pos = np.arange(GRID_W, dtype=np.float64)[:, None]
    cr = np.zeros((GRID_W, LANES)); cc = np.ones((GRID_W, LANES))
    sr = np.zeros((GRID_W, LANES)); sc = np.zeros((GRID_W, LANES))
    for base in range(0, LANES, width):
        for d in range(n_rope):
            lane = base + lane0 + d
            within = d % half
            ang = pos[:, 0] * inv[within % nf]
            sign = -1.0 if within < nf else 1.0
            if d < half:
                cr[:, lane] = np.cos(ang); cc[:, lane] = 0.0
                sr[:, lane] = sign * np.sin(ang)
            else:
                cc[:, lane] = np.cos(ang)
                sc[:, lane] = sign * np.sin(ang)
    return tuple(jnp.asarray(t, F32) for t in (cr, cc, sr, sc))


def _tile_tables(tabs, row0, n_sub):
    cr_ref, cc_ref, sr_ref, sc_ref = tabs
    cs, ss = [], []
    for j in range(n_sub):
        r = (row0 + j) % GRID_W
        cs.append(cr_ref[pl.ds(r, 1), :] + cc_ref[...])
        ss.append(sr_ref[pl.ds(r, 1), :] + sc_ref[...])
    return jnp.concatenate(cs, axis=0), jnp.concatenate(ss, axis=0)


def _rotate(x, cos_t, sin_t, nf):
    lane = _lane_iota(x.shape)
    up = pltpu.roll(x, LANES - nf, 1)
    dn = pltpu.roll(x, nf, 1)
    partner = jnp.where((lane & nf) == 0, up, dn)
    return x * cos_t + partner * sin_t


def _inproj_kernel(*refs, rope, emit_state, tiles_per_seq, has_cache):
    it = iter(refs)
    if has_cache:
        ckvc_ref, krpc_ref, dkc_ref, dvc_ref = (next(it) for _ in range(4))
    x_ref, mod_ref, g1_ref, w_ref, wa2_ref, ba_ref = (next(it) for _ in range(6))
    gqa_ref, gkva_ref, wuq_ref, wuk_ref, wuv_ref, gq_ref, gk_ref = (next(it) for _ in range(7))
    gdq_ref, gdk_ref = next(it), next(it)
    if rope:
        tab_m = tuple(next(it) for _ in range(4))
        tab_d = tuple(next(it) for _ in range(4))
    o_gq, o_gk, o_gv, o_sr, o_gf, o_gb = (next(it) for _ in range(6))
    o_mq, o_mk, o_mv, o_dq, o_dk, o_dv = (next(it) for _ in range(6))
    if emit_state:
        o_ckv, o_misc, o_dk32, o_dv32 = (next(it) for _ in range(4))

    def mla_keys(kn, krp, dst):
        for hd in range(H_B):
            sl = slice(hd * HEAD_PAD, (hd + 1) * HEAD_PAD)
            kt = kn[:, sl] + krp
            kt = kt * lax.rsqrt(jnp.sum(kt * kt, axis=-1, keepdims=True) * (1.0 / QK_B) + EPS) * gk_ref[...]
            dst[0, :, sl] = kt.astype(BF16)

    def with_ones(v):
        one_tile = _ones_lane0((v.shape[0], LANES))
        return jnp.concatenate(
            [t for hd in range(H_C) for t in (v[:, hd * LANES:(hd + 1) * LANES], one_tile)], axis=1).astype(BF16)

    def cache_rows():
        c = ckvc_ref[0].astype(BF16)
        vb = _dot(c, wuv_ref[...])
        o_mv[0] = jnp.where(_lane_iota(vb.shape) % HEAD_PAD == V_B, 1.0, vb).astype(BF16)
        mla_keys(_dot(c, wuk_ref[...]), krpc_ref[0], o_mk)
        o_dk[0] = dkc_ref[0].astype(BF16)
        o_dv[0] = with_ones(dvc_ref[0])

    def token_rows(seq_tile):
        tm = x_ref.shape[0]
        x = x_ref[...]
        m = mod_ref[0]
        h = _rms(x, g1_ref[...]) * (1.0 + m[1:2]) + m[0:1]
        y = _dot(h.astype(BF16), w_ref[...])

        o_gq[...] = y[:, C_AQ:C_AQ + 256] * (DK_A ** -0.5)
        o_gk[...] = y[:, C_AK:C_AK + 256]
        o_gv[...] = y[:, C_AV:C_AV + 512].astype(BF16)
        r = y[:, C_AR:C_AR + 512]
        o_sr[...] = r * _sigmoid(r)
        misc = y[:, C_MISC:C_MISC + LANES]
        logits = _dot(misc.astype(BF16), wa2_ref[...]) + ba_ref[...]
        logsig = jnp.minimum(logits, 0.0) - jnp.log(1.0 + jnp.exp(-jnp.abs(logits)))
        gate = logsig * (1.0 / GATE_TAU)
        o_gf[...] = gate[:, 0:256]
        o_gb[...] = gate[:, 256:512]

        if rope:
            row0 = seq_tile * (tm // GRID_W)
            cos_m, sin_m = _tile_tables(tab_m, row0, tm // GRID_W)
            cos_d, sin_d = _tile_tables(tab_d, row0, tm // GRID_W)

        qn = _rms(y[:, C_QD:C_QD + Q_RANK], gqa_ref[...])
        q8 = _dot(qn.astype(BF16), wuq_ref[...])
        ckv = _rms(y[:, C_KVD:C_KVD + KV_RANK], gkva_ref[...])
        kn = _dot(ckv.astype(BF16), wuk_ref[...])
        vb = _dot(ckv.astype(BF16), wuv_ref[...])
        o_mv[0] = jnp.where(_lane_iota(vb.shape) % HEAD_PAD == V_B, 1.0, vb).astype(BF16)
        krp = y[:, C_KRP:C_KRP + LANES]
        for hd in range(H_B):
            sl = slice(hd * HEAD_PAD, (hd + 1) * HEAD_PAD)
            qt = q8[:, sl]
            qt = qt * lax.rsqrt(jnp.sum(qt * qt, axis=-1, keepdims=True) * (1.0 / QK_B) + EPS) * gq_ref[...]
            kt = kn[:, sl] + krp
            kt = kt * lax.rsqrt(jnp.sum(kt * kt, axis=-1, keepdims=True) * (1.0 / QK_B) + EPS) * gk_ref[...]
            if rope:
                qt = _rotate(qt, cos_m, sin_m, ROPE_B // 4)
                kt = _rotate(kt, cos_m, sin_m, ROPE_B // 4)
            o_mq[:, sl] = (qt * (QK_B ** -0.5 * LOG2E)).astype(BF16)
            o_mk[0, :, sl] = kt.astype(BF16)

        lane = _lane_iota((tm, LANES))
        lo = lane < D_C
        for hd in range(H_C):
            sl = slice(hd * LANES, (hd + 1) * LANES)
            for src, g_ref, dst, dst32, scale in (
                (C_DQ, gdq_ref, o_dq, None, D_C ** -0.5 * LOG2E),
                (C_DK, gdk_ref, o_dk.at[0], o_dk32 if emit_state else None, 1.0),
            ):
                t = y[:, src + hd * LANES:src + (hd + 1) * LANES]
                sq = t * t
                s_lo = jnp.sum(jnp.where(lo, sq, 0.0), axis=-1, keepdims=True)
                s_hi = jnp.sum(jnp.where(lo, 0.0, sq), axis=-1, keepdims=True)
                inv = jnp.where(lo, lax.rsqrt(s_lo * (1.0 / D_C) + EPS), lax.rsqrt(s_hi * (1.0 / D_C) + EPS))
                t = t * inv * g_ref[...]
                if rope:
                    t = _rotate(t, cos_d, sin_d, D_C // 4)
                if dst32 is not None:
                    dst32[:, sl] = t
                dst[:, sl] = (t * scale).astype(BF16)
        dv = y[:, C_DV:C_DV + W_C]
        o_dv[0] = with_ones(dv)
        if emit_state:
            o_dv32[...] = dv
            o_ckv[...] = ckv
            o_misc[...] = misc

    if not has_cache:
        token_rows(pl.program_id(0) % tiles_per_seq)
        return
    step = pl.program_id(0) % (tiles_per_seq + 1)
    pl.when(step == 0)(cache_rows)
    pl.when(step > 0)(lambda: token_rows(step - 1))


def _inproj(x2, bsz, mod8, group0, lw, tabs, cache, *, rope, emit_state):
    t = x2.shape[0]
    tm = TOKEN_TILE
    seq = t // bsz
    tps = seq // tm
    has_cache = cache is not None
    past = cache[0].shape[1] if has_cache else 0
    assert past in (0, tm)
    lead = 1 if has_cache else 0
    req = lambda i: i // (tps + lead)
    tile = lambda i: jnp.maximum(i % (tps + lead) - lead, 0)
    kv_rows = lambda w: pl.BlockSpec((1, tm, w), lambda i: (req(i), i % (tps + lead), 0))
    const = lambda shape: pl.BlockSpec(shape, lambda i: (0,) * len(shape))
    row = lambda w: pl.BlockSpec((tm, w), lambda i: (req(i) * tps + tile(i), 0))
    if rope:
        mod_spec = pl.BlockSpec((1, 8, D_MODEL), lambda i: (group0 + req(i), 0, 0))
    else:
        mod_spec = pl.BlockSpec((1, 8, D_MODEL), lambda i: (group0, 0, 0))
    cache_specs = [pl.BlockSpec((1, past, c.shape[-1]), lambda i: (req(i), 0, 0)) for c in cache] if has_cache else []
    in_specs = cache_specs + [
        row(D_MODEL), mod_spec, const((1, D_MODEL)), const((D_MODEL, IN_COLS_A)),
        const((LANES, 512)), const((1, 512)),
        const((1, Q_RANK)), const((1, KV_RANK)), const((Q_RANK, H_B * HEAD_PAD)),
        const((KV_RANK, H_B * HEAD_PAD)), const((KV_RANK, H_B * HEAD_PAD)), const((1, LANES)), const((1, LANES)),
        const((1, LANES)), const((1, LANES)),
    ]
    args = (list(cache) if has_cache else []) + [
        x2, mod8, lw["g1"], lw["w_in_a"], lw["wa2"], lw["ba"], lw["gqa"], lw["gkva"], lw["wuq"],
        lw["wuk"], lw["wuv"], lw["gq"], lw["gk"], lw["gdq"], lw["gdk"]]
    if rope:
        in_specs += [const((GRID_W, LANES))] * 8
        args += list(tabs[0]) + list(tabs[1])
    flat, keyed = "flat", "keyed"
    outs = [(256, F32, flat), (256, F32, flat), (512, BF16, flat), (512, F32, flat), (256, F32, flat),
            (256, F32, flat), (H_B * HEAD_PAD, BF16, flat), (H_B * HEAD_PAD, BF16, keyed),
            (H_B * HEAD_PAD, BF16, keyed), (W_C, BF16, flat), (W_C, BF16, keyed), (2 * W_C, BF16, keyed)]
    if emit_state:
        outs += [(KV_RANK, F32, flat), (LANES, F32, flat), (W_C, F32, flat), (W_C, F32, flat)]
    return pl.pallas_call(
        functools.partial(_inproj_kernel, rope=rope, emit_state=emit_state, tiles_per_seq=tps,
                          has_cache=has_cache),
        grid=(bsz * (tps + lead),),
        in_specs=in_specs,
        out_specs=[row(w) if kind == flat else kv_rows(w) for w, _, kind in outs],
        out_shape=[jax.ShapeDtypeStruct((t, w) if kind == flat else (bsz, past + seq, w), dt)
                   for w, dt, kind in outs],
        compiler_params=pltpu.CompilerParams(dimension_semantics=("arbitrary",), vmem_limit_bytes=VMEM_LIMIT),
        name="inproj_lat" if rope else "inproj_ctx",
    )(*args)


def _gla_constants(rev):
    c = CHUNK
    idx = np.arange(c)
    t, u = idx[:, None], idx[None, :]
    mats = [(u <= t)]
    masks = [np.eye(c, dtype=bool)]
    hs = c // 2
    while hs >= 1:
        blk = idx // (2 * hs)
        lower = (idx % (2 * hs)) < hs
        p = blk * 2 * hs + hs - 1
        m_low = lower[:, None] & (u > t) & (u <= p[:, None])
        m_up = (~lower)[:, None] & (u > p[:, None]) & (u <= t)
        mats.append(m_low | m_up)
        masks.append((blk[:, None] == blk[None, :]) & (~lower)[:, None] & lower[None, :])
        hs //= 2
    mats.append(u > t)
    mats = [m.astype(np.float32) for m in mats]
    masks = [m.astype(np.float32) for m in masks]
    if rev:
        mats = [m[::-1, ::-1] for m in mats]
        masks = [m[::-1, ::-1] for m in masks]
    cm = np.concatenate(mats, axis=0)
    cms = np.concatenate([cm, cm], axis=1)
    lm = np.stack([np.tile(m, (1, H_A)) for m in masks])
    same = lambda n_row, n_col: (np.arange(H_A * n_row)[:, None] // n_row
                                 == np.arange(H_A * n_col)[None, :] // n_col).astype(np.float32)
    km = same(c, DK_A)
    vm = same(c, DV_A)
    sm = same(DV_A, DK_A)
    return (jnp.asarray(cms, BF16), jnp.asarray(lm, F32)), (jnp.asarray(km, BF16), jnp.asarray(vm, BF16),
                                                           jnp.asarray(sm, BF16))


N_LEVEL = 6


def _gla_chunk(q, k, v16, g, st, cms_ref, lm_ref, km_ref, vm_ref, sm_ref, rev):
    def per_head_rows(x):
        return jnp.concatenate([x.astype(BF16)] * H_A, axis=0) * km_ref[...]

    g_hi = g.astype(BF16)
    g_lo = (g - g_hi.astype(F32)).astype(BF16)
    sums = _dot(cms_ref[...], jnp.concatenate([g_hi, g_lo], axis=0))
    b = sums[0:CHUNK]
    att = lm_ref[0] * _dot_nt(q.astype(BF16), per_head_rows(k))
    for lv in range(1, N_LEVEL + 1):
        e = jnp.exp(sums[lv * CHUNK:(lv + 1) * CHUNK])
        att = att + lm_ref[lv] * _dot_nt((q * e).astype(BF16), per_head_rows(k * e))
    v_rows = jnp.concatenate([v16] * H_A, axis=0) * vm_ref[...]
    qe = (q * jnp.exp(b)).astype(BF16)
    out = _dot(att.astype(BF16), v_rows) + _dot_nt(qe, st.astype(BF16) * sm_ref[...])
    ke = (k * jnp.exp(sums[(N_LEVEL + 1) * CHUNK:(N_LEVEL + 2) * CHUNK])).astype(BF16)
    upd = pl.dot(v16, ke, trans_a=True)
    b_end = b[0:1] if rev else b[CHUNK - 1:CHUNK]
    return out, st * jnp.exp(b_end) + upd


def _gla_kernel(*refs, has_init, emit_state):
    it = iter(refs)
    qf_ref, kf_ref, vf_ref, gf_ref, qb_ref, kb_ref, vb_ref, gb_ref = (next(it) for _ in range(8))
    if has_init:
        s0f_ref, s0b_ref = next(it), next(it)
    cmsf_ref, lmf_ref, cmsb_ref, lmb_ref, km_ref, vm_ref, sm_ref = (next(it) for _ in range(7))
    of_ref, ob_ref = next(it), next(it)
    if emit_state:
        sff_ref, sfb_ref = next(it), next(it)
    stf_ref, stb_ref = next(it), next(it)
    ti = pl.program_id(1)
    nt = pl.num_programs(1)

    @pl.when(ti == 0)
    def _():
        if has_init:
            stf_ref[...] = s0f_ref[0]
            stb_ref[...] = s0b_ref[0]
        else:
            stf_ref[...] = jnp.zeros(stf_ref.shape, F32)
            stb_ref[...] = jnp.zeros(stb_ref.shape, F32)

    n_chunk = qf_ref.shape[1] // CHUNK
    stf = stf_ref[...]
    stb = stb_ref[...]
    for ci in range(n_chunk):
        rows = slice(ci * CHUNK, (ci + 1) * CHUNK)
        out, stf = _gla_chunk(qf_ref[0, rows, :], kf_ref[0, rows, :], vf_ref[0, rows, :], gf_ref[0, rows, :], stf,
                              cmsf_ref, lmf_ref, km_ref, vm_ref, sm_ref, False)
        of_ref[0, rows, :] = out
        cb = n_chunk - 1 - ci
        rows = slice(cb * CHUNK, (cb + 1) * CHUNK)
        out, stb = _gla_chunk(qb_ref[0, rows, :], kb_ref[0, rows, :], vb_ref[0, rows, :], gb_ref[0, rows, :], stb,
                              cmsb_ref, lmb_ref, km_ref, vm_ref, sm_ref, True)
        ob_ref[0, rows, :] = out
    stf_ref[...] = stf
    stb_ref[...] = stb

    if emit_state:
        @pl.when(ti == nt - 1)
        def _():
            for dst, st_ref in ((sff_ref, stf_ref), (sfb_ref, stb_ref)):
                blocks = st_ref[...].T
                for hd in range(H_A):
                    dst[0, hd] = blocks[hd * DK_A:(hd + 1) * DK_A, hd * DV_A:(hd + 1) * DV_A]


def _gla_scan(q3, k3, v3, gf3, gb3, init, consts, *, emit_state):
    bsz, s, _ = q3.shape
    tt = GLA_TILE
    nt = s // tt
    fmap = lambda b, t: (b, t, 0)
    rmap = lambda b, t: (b, nt - 1 - t, 0)
    bmap = lambda b, t: (b, 0, 0)
    const = lambda shape: pl.BlockSpec(shape, lambda b, t: (0,) * len(shape))
    tile = lambda w, m: pl.BlockSpec((1, tt, w), m)
    state = pl.BlockSpec((1, H_A * DV_A, H_A * DK_A), bmap)
    in_specs = [tile(256, fmap), tile(256, fmap), tile(512, fmap), tile(256, fmap),
                tile(256, rmap), tile(256, rmap), tile(512, rmap), tile(256, rmap)]
    args = [q3, k3, v3, gf3, q3, k3, v3, gb3]
    if init is not None:
        in_specs += [state, state]
        args += list(init)
    in_specs += [const(c.shape) for c in consts]
    args += list(consts)
    out_specs = [tile(W_A, fmap), tile(W_A, rmap)]
    out_shape = [jax.ShapeDtypeStruct((bsz, s, W_A), F32)] * 2
    if emit_state:
        out_specs += [pl.BlockSpec((1, H_A, DK_A, DV_A), lambda b, t: (b, 0, 0, 0))] * 2
        out_shape += [jax.ShapeDtypeStruct((bsz, H_A, DK_A, DV_A), F32)] * 2
    return pl.pallas_call(
        functools.partial(_gla_kernel, has_init=init is not None, emit_state=emit_state),
        grid=(bsz, nt),
        in_specs=in_specs,
        out_specs=out_specs,
        out_shape=out_shape,
        scratch_shapes=[pltpu.VMEM((H_A * DV_A, H_A * DK_A), F32)] * 2,
        compiler_params=pltpu.CompilerParams(dimension_semantics=("arbitrary", "arbitrary"),
                                             vmem_limit_bytes=VMEM_LIMIT),
        name="gla_lat" if init is not None else "gla_ctx",
    )(*args)


def _state_to_blocks(s0):
    st = jnp.swapaxes(s0, -1, -2)
    eye = jnp.eye(H_A, dtype=s0.dtype)
    full = st[:, :, :, None, :] * eye[None, :, None, :, None]
    return full.reshape(s0.shape[0], H_A * DV_A, H_A * DK_A)


def _online_step(carry, q, kt, vt):
    m, acc = carry
    s = _dot_nt(q, kt)
    m_new = jnp.maximum(m, jnp.max(s, axis=-1, keepdims=True))
    p = jnp.exp2(s - m_new)
    acc = jnp.exp2(m - m_new) * acc + _dot(p.astype(BF16), vt)
    return m_new, acc


def _key_rounds(n_keys, n_rounds, align):
    units = n_keys // align
    assert units * align == n_keys
    n_rounds = 1 if n_keys <= 2 * MXU_WIDTH else min(n_rounds, units)
    sizes = [(units // n_rounds + (1 if r < units % n_rounds else 0)) * align for r in range(n_rounds)]
    starts = np.concatenate([[0], np.cumsum(sizes)[:-1]])
    return [(int(s), int(z)) for s, z in zip(starts, sizes)]


def _attend(qs, k_ref, v_ref, ksls, vsls, n_rounds, align):
    tq = qs[0].shape[0]
    n = len(qs)
    width = vsls[0].stop - vsls[0].start
    init = (jnp.full((tq, 1), NEG_BIG, F32), jnp.zeros((tq, width), F32))
    carries = [init for _ in range(n)]
    for start, size in _key_rounds(k_ref.shape[1], n_rounds, align):
        rows = slice(start, start + size)
        carries = [_online_step(carries[i], qs[i], k_ref[0, rows, ksls[i]], v_ref[0, rows, vsls[i]])
                   for i in range(n)]
    return [c[1] for c in carries]


def _heads_per_step(n_heads, n_keys, at_least):
    return n_heads if n_keys <= 2 * MXU_WIDTH else at_least


def _mla_kernel(q_ref, k_ref, v_ref, o_ref):
    n_heads = q_ref.shape[2] // HEAD_PAD
    sls = [slice(hd * HEAD_PAD, (hd + 1) * HEAD_PAD) for hd in range(n_heads)]
    accs = _attend([q_ref[0, :, sl] for sl in sls], k_ref, v_ref, sls, sls, MLA_ROUNDS, MXU_WIDTH)
    outs = [acc / acc[:, V_B:V_B + 1] for acc in accs]
    lane = _lane_iota(outs[0].shape)
    pairs = [jnp.where(lane < V_B, outs[2 * p], pltpu.roll(outs[2 * p + 1], V_B, 1)) for p in range(n_heads // 2)]
    o_ref[0] = jnp.concatenate(pairs, axis=1).astype(o_ref.dtype)


def _mla_attention(q3, k3, v3):
    bsz, s, _ = q3.shape
    keys = k3.shape[1]
    tq = min(Q_TILE, s)
    nh = _heads_per_step(H_B, keys, 2)
    pair = nh * HEAD_PAD
    return pl.pallas_call(
        _mla_kernel,
        grid=(bsz, H_B // nh, s // tq),
        in_specs=[
            pl.BlockSpec((1, tq, pair), lambda b, h, i: (b, i, h)),
            pl.BlockSpec((1, keys, pair), lambda b, h, i: (b, 0, h)),
            pl.BlockSpec((1, keys, pair), lambda b, h, i: (b, 0, h)),
        ],
        out_specs=pl.BlockSpec((1, tq, nh * V_B), lambda b, h, i: (b, i, h)),
        out_shape=jax.ShapeDtypeStruct((bsz, s, W_B), BF16),
        compiler_params=pltpu.CompilerParams(dimension_semantics=("arbitrary",) * 3,
                                             vmem_limit_bytes=VMEM_LIMIT),
        name="mla_lat" if keys > s else "mla_ctx",
    )(q3, k3, v3)


def _diff_kernel(lam_ref, q_ref, k_ref, v_ref, o_ref, *, lam_init):
    lq = lam_ref[...]
    lam = (jnp.exp(jnp.sum(lq[0:1] * lq[1:2], axis=-1, keepdims=True))
           - jnp.exp(jnp.sum(lq[2:3] * lq[3:4], axis=-1, keepdims=True)) + lam_init)
    n_heads = q_ref.shape[2] // LANES
    lane = _lane_iota((q_ref.shape[1], LANES))
    qcs, ksls, vsls = [], [], []
    for hd in range(n_heads):
        q = q_ref[0, :, hd * LANES:(hd + 1) * LANES].astype(F32)
        qcs += [jnp.where(lane < D_C, q, 0.0).astype(BF16), jnp.where(lane >= D_C, q, 0.0).astype(BF16)]
        ksls += [slice(hd * LANES, (hd + 1) * LANES)] * 2
        vsls += [slice(2 * hd * LANES, 2 * (hd + 1) * LANES)] * 2
    accs = _attend(qcs, k_ref, v_ref, ksls, vsls, DIFF_ROUNDS, LANES)
    res = [acc[:, 0:LANES] / acc[:, LANES:LANES + 1] for acc in accs]
    o_ref[0] = jnp.concatenate([res[2 * hd] - lam * res[2 * hd + 1] for hd in range(n_heads)], axis=1)


def _diff_attention(lam_qk, q3, k3, v3, lam_init):
    bsz, s, _ = q3.shape
    keys = k3.shape[1]
    tq = min(Q_TILE, s)
    nh = _heads_per_step(H_C, keys, 1)
    return pl.pallas_call(
        functools.partial(_diff_kernel, lam_init=lam_init),
        grid=(bsz, H_C // nh, s // tq),
        in_specs=[
            pl.BlockSpec((4, D_C), lambda b, h, i: (0, 0)),
            pl.BlockSpec((1, tq, nh * LANES), lambda b, h, i: (b, i, h)),
            pl.BlockSpec((1, keys, nh * LANES), lambda b, h, i: (b, 0, h)),
            pl.BlockSpec((1, keys, 2 * nh * LANES), lambda b, h, i: (b, 0, h)),
        ],
        out_specs=pl.BlockSpec((1, tq, nh * LANES), lambda b, h, i: (b, i, h)),
        out_shape=jax.ShapeDtypeStruct((bsz, s, W_C), F32),
        compiler_params=pltpu.CompilerParams(dimension_semantics=("arbitrary",) * 3,
                                             vmem_limit_bytes=VMEM_LIMIT),
        name="diff_lat" if keys > s else "diff_ctx",
    )(lam_qk, q3, k3, v3)


def _merge_kernel(x_ref, mod_ref, g1_ref, wg_ref, of_ref, ob_ref, sr_ref, yb_ref, yc_ref, ggla_ref, gsub_ref,
                  woa_ref, wob_ref, woc_ref, wout_ref, g2_ref, w1_ref, w2_ref, o_ref, *, lam_init):
    x = x_ref[...]
    m = mod_ref[0]
    h = _rms(x, g1_ref[...]) * (1.0 + m[1:2]) + m[0:1]
    gates = _sigmoid(_dot(h.astype(BF16), wg_ref[...]))
    o = of_ref[...] + ob_ref[...]
    sr = sr_ref[...]
    yc = yc_ref[...]
    ya_t, yc_t = [], []
    for hd in range(H_A):
        sl = slice(hd * LANES, (hd + 1) * LANES)
        ya_t.append(_rms(o[:, sl], ggla_ref[...]) * sr[:, sl])
        yc_t.append(_rms(yc[:, sl], gsub_ref[...]) * (1.0 - lam_init))
    ya = jnp.concatenate(ya_t, axis=1).astype(BF16)
    ycn = jnp.concatenate(yc_t, axis=1).astype(BF16)
    merged = (gates[:, 0:D_MODEL] * _dot(ya, woa_ref[...])
              + gates[:, D_MODEL:2 * D_MODEL] * _dot(yb_ref[...], wob_ref[...])
              + gates[:, 2 * D_MODEL:3 * D_MODEL] * _dot(ycn, woc_ref[...]))
    x1 = x + m[2:3] * _dot(merged.astype(BF16), wout_ref[...])

    h2 = _rms(x1, g2_ref[...]) * (1.0 + m[4:5]) + m[3:4]
    a = jnp.maximum(_dot(h2.astype(BF16), w1_ref[...]), 0.0)
    o_ref[...] = x1 + m[5:6] * _dot((a * a).astype(BF16), w2_ref[...])


def _merge_mlp(x2, mod8, group0, lw, of2, ob2, sr2, yb2, yc2, lam_init, per_seq_groups):
    t = x2.shape[0]
    tm = TOKEN_TILE
    tiles_per_seq = LAT_SEQ // tm
    const = lambda shape: pl.BlockSpec(shape, lambda i: (0,) * len(shape), pipeline_mode=pl.Buffered(1))
    row = lambda w: pl.BlockSpec((tm, w), lambda i: (i, 0))
    if per_seq_groups:
        mod_spec = pl.BlockSpec((1, 8, D_MODEL), lambda i: (group0 + i // tiles_per_seq, 0, 0))
    else:
        mod_spec = pl.BlockSpec((1, 8, D_MODEL), lambda i: (group0, 0, 0))
    return pl.pallas_call(
        functools.partial(_merge_kernel, lam_init=lam_init),
        grid=(t // tm,),
        in_specs=[row(D_MODEL), mod_spec, const((1, D_MODEL)), const((D_MODEL, GATE_COLS)),
                  row(W_A), row(W_A), row(W_A), row(W_B), row(W_C), const((1, LANES)), const((1, LANES)),
                  const((W_A, D_MODEL)), const((W_B, D_MODEL)), const((W_C, D_MODEL)),
                  const((D_MODEL, D_MODEL)), const((1, D_MODEL)), const((D_MODEL, D_FF)), const((D_FF, D_MODEL))],
        out_specs=row(D_MODEL),
        out_shape=jax.ShapeDtypeStruct((t, D_MODEL), F32),
        compiler_params=pltpu.CompilerParams(dimension_semantics=("arbitrary",), vmem_limit_bytes=VMEM_LIMIT),
        name="merge_mlp",
    )(x2, mod8, lw["g1"], lw["w_gates"], of2, ob2, sr2, yb2, yc2, lw["ggla"], lw["gsub"],
      lw["woa"], lw["wob"], lw["woc"], lw["wout"], lw["g2"], lw["w1"], lw["w2"])


def _pad_heads(w, real, pad):
    k = w.shape[0]
    w = w.reshape(k, -1, real)
    return jnp.pad(w, ((0, 0), (0, 0), (0, pad - real))).reshape(k, -1)


def _layer_weights(l, p):
    widths = (H_A * DK_A, H_A * DK_A, W_A, W_A, 2 * GATE_RANK, Q_RANK, KV_RANK, ROPE_B, W_C, W_C, W_C, GATE_COLS)
    offs = np.concatenate([[0], np.cumsum(widths)])
    w_in = p["w_in"][l]
    aq, ak, av, ar, aa, qd, kvd, kr, dq, dk, dv, gates = (w_in[:, offs[i]:offs[i + 1]] for i in range(12))
    z = lambda n: jnp.zeros((D_MODEL, n), F32)
    misc = jnp.concatenate([aa, kr, z(LANES - 2 * GATE_RANK - ROPE_B)], axis=1)
    krp = jnp.concatenate([z(NOPE_B), kr, z(LANES - QK_B)], axis=1)
    w_in_a = jnp.concatenate([aq, ak, av, ar, qd, kvd, dq, dk, dv, misc, krp], axis=1).astype(BF16)
    wa2 = jnp.zeros((LANES, 512), F32)
    wa2 = wa2.at[0:GATE_RANK, 0:256].set(p["w_gla_a2"][l, 0])
    wa2 = wa2.at[GATE_RANK:2 * GATE_RANK, 256:512].set(p["w_gla_a2"][l, 1])
    pad_gain = lambda g: jnp.pad(g, (0, HEAD_PAD - QK_B)).reshape(1, HEAD_PAD)
    twice = lambda g: jnp.concatenate([g, g]).reshape(1, LANES)
    return dict(
        g1=p["g_norm1"][l].reshape(1, D_MODEL), g2=p["g_norm2"][l].reshape(1, D_MODEL),
        w_in_a=w_in_a, w_gates=gates.astype(BF16),
        wa2=wa2.astype(BF16), ba=p["b_gla_a"][l].reshape(1, 512),
        gqa=p["g_mla_qa"][l].reshape(1, Q_RANK), gkva=p["g_mla_kva"][l].reshape(1, KV_RANK),
        wuq=_pad_heads(p["w_mla_uq"][l], QK_B, HEAD_PAD).astype(BF16),
        wuk=_pad_heads(p["w_mla_uk"][l], NOPE_B, HEAD_PAD).astype(BF16),
        wuv=_pad_heads(p["w_mla_uv"][l], V_B, HEAD_PAD).astype(BF16),
        gq=pad_gain(p["g_mla_q"][l]), gk=pad_gain(p["g_mla_k"][l]),
        gdq=twice(p["g_diff_q"][l]), gdk=twice(p["g_diff_k"][l]),
        ggla=p["g_gla_out"][l].reshape(1, DV_A), gsub=p["g_diff_sub"][l].reshape(1, 2 * D_C),
        woa=p["w_o_gla"][l].astype(BF16), wob=p["w_o_mla"][l].astype(BF16), woc=p["w_o_diff"][l].astype(BF16),
        wout=p["w_out"][l].astype(BF16), w1=p["w_mlp1"][l].astype(BF16), w2=p["w_mlp2"][l].astype(BF16),
        lam_qk=p["lam_qk"][l],
    )


def _layer(x2, bsz, seq, mod8, group0, lw, lam_init, consts, tabs, cached):
    is_latent = cached is not None
    cache = None
    if is_latent:
        s0, ckv_c, krope_c, dk_c, dv_c = cached
        past = ckv_c.shape[1]
        krp_c = jnp.pad(krope_c, ((0, 0), (0, 0), (NOPE_B, LANES - QK_B)))
        cache = (ckv_c, krp_c, dk_c.reshape(bsz, past, W_C), dv_c.reshape(bsz, past, W_C))
    outs = _inproj(x2, bsz, mod8, group0, lw, tabs, cache, rope=is_latent, emit_state=not is_latent)
    gq, gk, gv, sr, gf, gb, mq, mk, mv, dq, dk, dv = outs[:12]
    r3 = lambda a: a.reshape(bsz, seq, a.shape[-1])

    init = (_state_to_blocks(s0[:, 0]), _state_to_blocks(s0[:, 1])) if is_latent else None
    gla = _gla_scan(r3(gq), r3(gk), r3(gv), r3(gf), r3(gb), init, consts, emit_state=not is_latent)
    of, ob = gla[0], gla[1]

    yb = _mla_attention(r3(mq), mk, mv)
    yc = _diff_attention(lw["lam_qk"], r3(dq), dk, dv, lam_init)

    x_out = _merge_mlp(x2, mod8, group0, lw, of.reshape(-1, W_A), ob.reshape(-1, W_A), sr, yb.reshape(-1, W_B),
                       yc.reshape(-1, W_C), lam_init, is_latent)

    new_state = None
    if not is_latent:
        ckv, misc, dk32, dv32 = outs[12:]
        gla_state = jnp.stack([gla[2], gla[3]], axis=1)
        new_state = (gla_state, ckv.reshape(bsz, seq, KV_RANK),
                     misc[:, 2 * GATE_RANK:2 * GATE_RANK + ROPE_B].reshape(bsz, seq, ROPE_B),
                     dk32.reshape(bsz, seq, H_C, 2, D_C), dv32.reshape(bsz, seq, H_C, 2 * D_C))
    return x_out, new_state


def kernel(x_prompt, x_sample, state_gla, cache_mla_ckv, cache_mla_krope, cache_diff_k, cache_diff_v, c, c_ctx, w_mod, b_mod, g_norm1, g_norm2, w_in, w_gla_a2, b_gla_a, g_gla_out, g_mla_qa, g_mla_kva, w_mla_uq, w_mla_uk, w_mla_uv, g_mla_q, g_mla_k, g_diff_q, g_diff_k, lam_qk, g_diff_sub, w_o_gla, w_o_mla, w_o_diff, w_out, w_mlp1, w_mlp2):
    params = dict(w_in=w_in, g_norm1=g_norm1, g_norm2=g_norm2, w_gla_a2=w_gla_a2, b_gla_a=b_gla_a,
                  g_gla_out=g_gla_out, g_mla_qa=g_mla_qa, g_mla_kva=g_mla_kva, w_mla_uq=w_mla_uq,
                  w_mla_uk=w_mla_uk, w_mla_uv=w_mla_uv, g_mla_q=g_mla_q, g_mla_k=g_mla_k, g_diff_q=g_diff_q,
                  g_diff_k=g_diff_k, lam_qk=lam_qk, g_diff_sub=g_diff_sub, w_o_gla=w_o_gla, w_o_mla=w_o_mla,
                  w_o_diff=w_o_diff, w_out=w_out, w_mlp1=w_mlp1, w_mlp2=w_mlp2)
    nb, ns, _ = x_prompt.shape
    db, ds, _ = x_sample.shape
    assert ds == LAT_SEQ and db + 1 <= 8

    cond8 = jnp.concatenate([c_ctx[None, :], c, jnp.zeros((8 - 1 - db, D_MODEL), F32)], axis=0)
    mod = _modulation(cond8, w_mod, b_mod).reshape(DEPTH, 8, 6, D_MODEL)
    mod = jnp.pad(mod, ((0, 0), (0, 0), (0, 2), (0, 0)))

    (fwd_c, head_masks), (bwd_c, _) = _gla_constants(False), _gla_constants(True)
    consts = fwd_c + bwd_c + head_masks
    tabs = (_rope_tables(ROPE_B, NOPE_B, LANES), _rope_tables(D_C, 0, D_C))

    yp = x_prompt.reshape(nb * ns, D_MODEL)
    ys = x_sample.reshape(db * ds, D_MODEL)
    states = []
    for l in range(DEPTH):
        lw = _layer_weights(l, params)
        lam_init = 0.8 - 0.6 * math.exp(-0.3 * l)
        yp, st = _layer(yp, nb, ns, mod[l], 0, lw, lam_init, consts, tabs, None)
        states.append(st)
        cached = (state_gla[:, l], cache_mla_ckv[:, l], cache_mla_krope[:, l], cache_diff_k[:, l],
                  cache_diff_v[:, l])
        ys, _ = _layer(ys, db, ds, mod[l], 1, lw, lam_init, consts, tabs, cached)
    stack = lambda i: jnp.stack([s[i] for s in states], axis=1)
    return (yp.reshape(nb, ns, D_MODEL), ys.reshape(db, ds, D_MODEL), stack(0), stack(1), stack(2), stack(3),
            stack(4))
```

```python
import functools
import math

import numpy as np
import jax
import jax.numpy as jnp
from jax import lax
from jax.experimental import pallas as pl
from jax.experimental.pallas import tpu as pltpu

F32 = jnp.float32
BF16 = jnp.bfloat16

D_MODEL = 1024
DEPTH = 2
GRID_W = 64
ROPE_BASE = 10000.0
EPS = 1e-6
H_A, DK_A, DV_A = 4, 64, 128
GATE_RANK = 16
GATE_TAU = 16.0
CHUNK = 64
W_A = H_A * DV_A
H_B, Q_RANK, KV_RANK = 8, 384, 256
NOPE_B, ROPE_B, V_B = 64, 32, 64
QK_B = NOPE_B + ROPE_B
W_B = H_B * V_B
H_C, D_C = 4, 64
W_C = H_C * 2 * D_C
D_FF = 4 * D_MODEL
GATE_COLS = 3 * D_MODEL
LAT_SEQ = 4096

LANES = 128
HEAD_PAD = LANES
MXU_WIDTH = 256
VMEM_LIMIT = 56 * 1024 * 1024

TOKEN_TILE = 256
Q_TILE = 1024
MLA_ROUNDS = 2
DIFF_ROUNDS = 2
GLA_TILE = 256

C_AQ, C_AK, C_AV, C_AR = 0, 256, 512, 1024
C_QD, C_KVD = 1536, 1920
C_DQ, C_DK, C_DV = 2176, 2688, 3200
C_MISC, C_KRP, C_KRQ = 3712, 3840, 3968
IN_COLS_A = 4096

NEG_BIG = -1e30
LOG2E = math.log2(math.e)


def _dot(a, b):
    return jnp.dot(a, b, preferred_element_type=F32)


def _dot_nt(a, b):
    return lax.dot_general(a, b, (((1,), (1,)), ((), ())), preferred_element_type=F32)


def _sigmoid(x):
    return 1.0 / (1.0 + jnp.exp(-x))


def _rms(x, g):
    ms = jnp.mean(x * x, axis=-1, keepdims=True)
    return x * lax.rsqrt(ms + EPS) * g


def _lane_iota(shape):
    return lax.broadcasted_iota(jnp.int32, shape, len(shape) - 1)


def _ones_lane0(shape):
    return jnp.where(_lane_iota(shape) % LANES == 0, 1.0, 0.0).astype(F32)


def _mod_kernel(c_ref, w_ref, b_ref, o_ref):
    c = c_ref[...]
    s = c * _sigmoid(c)
    o_ref[0] = _dot(s.astype(BF16), w_ref[0].astype(BF16)) + b_ref[0]


def _modulation(cond8, w_mod, b_mod):
    nt = 768
    return pl.pallas_call(
        _mod_kernel,
        grid=(DEPTH, 6 * D_MODEL // nt),
        in_specs=[
            pl.BlockSpec((8, D_MODEL), lambda l, j: (0, 0)),
            pl.BlockSpec((1, D_MODEL, nt), lambda l, j: (l, 0, j)),
            pl.BlockSpec((1, 1, nt), lambda l, j: (l, 0, j)),
        ],
        out_specs=pl.BlockSpec((1, 8, nt), lambda l, j: (l, 0, j)),
        out_shape=jax.ShapeDtypeStruct((DEPTH, 8, 6 * D_MODEL), F32),
        name="modulation",
    )(cond8, w_mod, b_mod.reshape(DEPTH, 1, 6 * D_MODEL))


def _rope_tables(n_rope, lane0, width):
    half = n_rope // 2
    nf = half // 2
    inv = ROPE_BASE ** (-np.arange(nf, dtype=np.float64) / nf)
    pos = np.arange(GRID_W, dtype=np.float64)[:, None]
    cr = np.zeros((GRID_W, LANES)); cc = np.ones((GRID_W, LANES))
    sr = np.zeros((GRID_W, LANES)); sc = np.zeros((GRID_W, LANES))
    for base in range(0, LANES, width):
        for d in range(n_rope):
            lane = base + lane0 + d
            within = d % half
            ang = pos[:, 0] * inv[within % nf]
            sign = -1.0 if within < nf else 1.0
            if d < half:
                cr[:, lane] = np.cos(ang); cc[:, lane] = 0.0
                sr[:, lane] = sign * np.sin(ang)
            else:
                cc[:, lane] = np.cos(ang)
                sc[:, lane] = sign * np.sin(ang)
    return tuple(jnp.asarray(t, F32) for t in (cr, cc, sr, sc))


def _tile_tables(tabs, row0, n_sub):
    cr_ref, cc_ref, sr_ref, sc_ref = tabs
    cs, ss = [], []
    for j in range(n_sub):
        r = (row0 + j) % GRID_W
        cs.append(cr_ref[pl.ds(r, 1), :] + cc_ref[...])
        ss.append(sr_ref[pl.ds(r, 1), :] + sc_ref[...])
    return jnp.concatenate(cs, axis=0), jnp.concatenate(ss, axis=0)


def _rotate(x, cos_t, sin_t, nf):
    lane = _lane_iota(x.shape)
    up = pltpu.roll(x, LANES - nf, 1)
    dn = pltpu.roll(x, nf, 1)
    partner = jnp.where((lane & nf) == 0, up, dn)
    return x * cos_t + partner * sin_t


def _inproj_kernel(*refs, rope, emit_state, tiles_per_seq, has_cache):
    it = iter(refs)
    if has_cache:
        ckvc_ref, krpc_ref, dkc_ref, dvc_ref = (next(it) for _ in range(4))
    x_ref, mod_ref, g1_ref, w_ref, wa2_ref, ba_ref = (next(it) for _ in range(6))
    gqa_ref, gkva_ref, wuq_ref, wuk_ref, wuv_ref, gq_ref, gk_ref = (next(it) for _ in range(7))
    gdq_ref, gdk_ref = next(it), next(it)
    if rope:
        tab_m = tuple(next(it) for _ in range(4))
        tab_d = tuple(next(it) for _ in range(4))
        wuqp_ref, gqp_ref, gkp_ref = (next(it) for _ in range(3))
    o_gq, o_gk, o_gv, o_sr, o_gf, o_gb = (next(it) for _ in range(6))
    o_mq, o_mk, o_mv, o_dq, o_dk, o_dv = (next(it) for _ in range(6))
    if emit_state:
        o_ckv, o_misc, o_dk32, o_dv32 = (next(it) for _ in range(4))

    def mla_keys(kn, krp, dst):
        for hd in range(H_B):
            sl = slice(hd * HEAD_PAD, (hd + 1) * HEAD_PAD)
            kt = kn[:, sl] + krp
            kt = kt * lax.rsqrt(jnp.sum(kt * kt, axis=-1, keepdims=True) * (1.0 / QK_B) + EPS) * gk_ref[...]
            dst[0, :, sl] = kt.astype(BF16)

    def with_ones(v):
        one_tile = _ones_lane0((v.shape[0], LANES))
        return jnp.concatenate(
            [t for hd in range(H_C) for t in (v[:, hd * LANES:(hd + 1) * LANES], one_tile)], axis=1).astype(BF16)

    def cache_rows():
        c = ckvc_ref[0].astype(BF16)
        vb = _dot(c, wuv_ref[...])
        o_mv[0] = jnp.where(_lane_iota(vb.shape) % HEAD_PAD == V_B, 1.0, vb).astype(BF16)
        mla_keys(_dot(c, wuk_ref[...]), krpc_ref[0], o_mk)
        o_dk[0] = dkc_ref[0].astype(BF16)
        o_dv[0] = with_ones(dvc_ref[0])

    def token_rows(seq_tile):
        tm = x_ref.shape[0]
        x = x_ref[...]
        m = mod_ref[0]
        h = _rms(x, g1_ref[...]) * (1.0 + m[1:2]) + m[0:1]
        y = _dot(h.astype(BF16), w_ref[...])

        o_gq[...] = y[:, C_AQ:C_AQ + 256] * (DK_A ** -0.5)
        o_gk[...] = y[:, C_AK:C_AK + 256]
        o_gv[...] = y[:, C_AV:C_AV + 512].astype(BF16)
        r = y[:, C_AR:C_AR + 512]
        o_sr[...] = r * _sigmoid(r)
        misc = y[:, C_MISC:C_MISC + LANES]
        logits = _dot(misc.astype(BF16), wa2_ref[...]) + ba_ref[...]
        logsig = jnp.minimum(logits, 0.0) - jnp.log(1.0 + jnp.exp(-jnp.abs(logits)))
        gate = logsig * (1.0 / GATE_TAU)
        o_gf[...] = gate[:, 0:256]
        o_gb[...] = gate[:, 256:512]

        if rope:
            row0 = seq_tile * (tm // GRID_W)
            cos_m, sin_m = _tile_tables(tab_m, row0, tm // GRID_W)
            cos_d, sin_d = _tile_tables(tab_d, row0, tm // GRID_W)

        qn = _rms(y[:, C_QD:C_QD + Q_RANK], gqa_ref[...])
        q8 = _dot(qn.astype(BF16), wuq_ref[...])
        ckv = _rms(y[:, C_KVD:C_KVD + KV_RANK], gkva_ref[...])
        kn = _dot(ckv.astype(BF16), wuk_ref[...])
        vb = _dot(ckv.astype(BF16), wuv_ref[...])
        o_mv[0] = jnp.where(_lane_iota(vb.shape) % HEAD_PAD == V_B, 1.0, vb).astype(BF16)
        krp = y[:, C_KRP:C_KRP + LANES]
        if rope:
            q8p = _dot(qn.astype(BF16), wuqp_ref[...])
            gq_sin = gqp_ref[...] * sin_m
            k_part = y[:, C_KRQ:C_KRQ + LANES] * (gkp_ref[...] * sin_m)
        for hd in range(H_B):
            sl = slice(hd * HEAD_PAD, (hd + 1) * HEAD_PAD)
            qt = q8[:, sl]
            rq = lax.rsqrt(jnp.sum(qt * qt, axis=-1, keepdims=True) * (1.0 / QK_B) + EPS)
            qt = qt * rq * gq_ref[...]
            kt = kn[:, sl] + krp
            rk = lax.rsqrt(jnp.sum(kt * kt, axis=-1, keepdims=True) * (1.0 / QK_B) + EPS)
            kt = kt * rk * gk_ref[...]
            if rope:
                qt = qt * cos_m + (q8p[:, sl] * gq_sin) * rq
                kt = kt * cos_m + k_part * rk
            o_mq[:, sl] = (qt * (QK_B ** -0.5 * LOG2E)).astype(BF16)
            o_mk[0, :, sl] = kt.astype(BF16)

        lane = _lane_iota((tm, LANES))
        lo = lane < D_C
        for hd in range(H_C):
            sl = slice(hd * LANES, (hd + 1) * LANES)
            for src, g_ref, dst, dst32, scale in (
                (C_DQ, gdq_ref, o_dq, None, D_C ** -0.5 * LOG2E),
                (C_DK, gdk_ref, o_dk.at[0], o_dk32 if emit_state else None, 1.0),
            ):
                t = y[:, src + hd * LANES:src + (hd + 1) * LANES]
                sq = t * t
                s_lo = jnp.sum(jnp.where(lo, sq, 0.0), axis=-1, keepdims=True)
                s_hi = jnp.sum(jnp.where(lo, 0.0, sq), axis=-1, keepdims=True)
                inv = jnp.where(lo, lax.rsqrt(s_lo * (1.0 / D_C) + EPS), lax.rsqrt(s_hi * (1.0 / D_C) + EPS))
                t = t * inv * g_ref[...]
                if rope:
                    t = _rotate(t, cos_d, sin_d, D_C // 4)
                if dst32 is not None:
                    dst32[:, sl] = t
                dst[:, sl] = (t * scale).astype(BF16)
        dv = y[:, C_DV:C_DV + W_C]
        o_dv[0] = with_ones(dv)
        if emit_state:
            o_dv32[...] = dv
            o_ckv[...] = ckv
            o_misc[...] = misc

    if not has_cache:
        token_rows(pl.program_id(0) % tiles_per_seq)
        return
    step = pl.program_id(0) % (tiles_per_seq + 1)
    pl.when(step == 0)(cache_rows)
    pl.when(step > 0)(lambda: token_rows(step - 1))


def _inproj(x2, bsz, mod8, group0, lw, tabs, cache, *, rope, emit_state):
    t = x2.shape[0]
    tm = TOKEN_TILE
    seq = t // bsz
    tps = seq // tm
    has_cache = cache is not None
    past = cache[0].shape[1] if has_cache else 0
    assert past in (0, tm)
    lead = 1 if has_cache else 0
    req = lambda i: i // (tps + lead)
    tile = lambda i: jnp.maximum(i % (tps + lead) - lead, 0)
    kv_rows = lambda w: pl.BlockSpec((1, tm, w), lambda i: (req(i), i % (tps + lead), 0))
    const = lambda shape: pl.BlockSpec(shape, lambda i: (0,) * len(shape))
    row = lambda w: pl.BlockSpec((tm, w), lambda i: (req(i) * tps + tile(i), 0))
    if rope:
        mod_spec = pl.BlockSpec((1, 8, D_MODEL), lambda i: (group0 + req(i), 0, 0))
    else:
        mod_spec = pl.BlockSpec((1, 8, D_MODEL), lambda i: (group0, 0, 0))
    cache_specs = [pl.BlockSpec((1, past, c.shape[-1]), lambda i: (req(i), 0, 0)) for c in cache] if has_cache else []
    in_specs = cache_specs + [
        row(D_MODEL), mod_spec, const((1, D_MODEL)), const((D_MODEL, IN_COLS_A)),
        const((LANES, 512)), const((1, 512)),
        const((1, Q_RANK)), const((1, KV_RANK)), const((Q_RANK, H_B * HEAD_PAD)),
        const((KV_RANK, H_B * HEAD_PAD)), const((KV_RANK, H_B * HEAD_PAD)), const((1, LANES)), const((1, LANES)),
        const((1, LANES)), const((1, LANES)),
    ]
    args = (list(cache) if has_cache else []) + [
        x2, mod8, lw["g1"], lw["w_in_a"], lw["wa2"], lw["ba"], lw["gqa"], lw["gkva"], lw["wuq"],
        lw["wuk"], lw["wuv"], lw["gq"], lw["gk"], lw["gdq"], lw["gdk"]]
    if rope:
        in_specs += [const((GRID_W, LANES))] * 8 + [const((Q_RANK, H_B * HEAD_PAD)), const((1, LANES)), const((1, LANES))]
        args += list(tabs[0]) + list(tabs[1]) + [lw["wuq_part"], lw["gq_part"], lw["gk_part"]]
    flat, keyed = "flat", "keyed"
    outs = [(256, F32, flat), (256, F32, flat), (512, BF16, flat), (512, F32, flat), (256, F32, flat),
            (256, F32, flat), (H_B * HEAD_PAD, BF16, flat), (H_B * HEAD_PAD, BF16, keyed),
            (H_B * HEAD_PAD, BF16, keyed), (W_C, BF16, flat), (W_C, BF16, keyed), (2 * W_C, BF16, keyed)]
    if emit_state:
        outs += [(KV_RANK, F32, flat), (LANES, F32, flat), (W_C, F32, flat), (W_C, F32, flat)]
    return pl.pallas_call(
        functools.partial(_inproj_kernel, rope=rope, emit_state=emit_state, tiles_per_seq=tps,
                          has_cache=has_cache),
        grid=(bsz * (tps + lead),),
        in_specs=in_specs,
        out_specs=[row(w) if kind == flat else kv_rows(w) for w, _, kind in outs],
        out_shape=[jax.ShapeDtypeStruct((t, w) if kind == flat else (bsz, past + seq, w), dt)
                   for w, dt, kind in outs],
        compiler_params=pltpu.CompilerParams(dimension_semantics=("arbitrary",), vmem_limit_bytes=VMEM_LIMIT),
        name="inproj_lat" if rope else "inproj_ctx",
    )(*args)


def _gla_constants(rev):
    c = CHUNK
    idx = np.arange(c)
    t, u = idx[:, None], idx[None, :]
    mats = [(u <= t)]
    masks = [np.eye(c, dtype=bool)]
    hs = c // 2
    while hs >= 1:
        blk = idx // (2 * hs)
        lower = (idx % (2 * hs)) < hs
        p = blk * 2 * hs + hs - 1
        m_low = lower[:, None] & (u > t) & (u <= p[:, None])
        m_up = (~lower)[:, None] & (u > p[:, None]) & (u <= t)
        mats.append(m_low | m_up)
        masks.append((blk[:, None] == blk[None, :]) & (~lower)[:, None] & lower[None, :])
        hs //= 2
    mats.append(u > t)
    mats = [m.astype(np.float32) for m in mats]
    masks = [m.astype(np.float32) for m in masks]
    if rev:
        mats = [m[::-1, ::-1] for m in mats]
        masks = [m[::-1, ::-1] for m in masks]
    cm = np.concatenate(mats, axis=0)
    cms = np.concatenate([cm, cm], axis=1)
    lm = np.stack([np.tile(m, (1, H_A)) for m in masks])
    same = lambda n_row, n_col: (np.arange(H_A * n_row)[:, None] // n_row
                                 == np.arange(H_A * n_col)[None, :] // n_col).astype(np.float32)
    km = same(c, DK_A)
    vm = same(c, DV_A)
    sm = same(DV_A, DK_A)
    return (jnp.asarray(cms, BF16), jnp.asarray(lm, F32)), (jnp.asarray(km, BF16), jnp.asarray(vm, BF16),
                                                           jnp.asarray(sm, BF16))


N_LEVEL = 6


def _gla_chunk(q, k, v16, g, st, cms_ref, lm_ref, km_ref, vm_ref, sm_ref, rev):
    def per_head_rows(x):
        return jnp.concatenate([x.astype(BF16)] * H_A, axis=0) * km_ref[...]

    g_hi = g.astype(BF16)
    g_lo = (g - g_hi.astype(F32)).astype(BF16)
    sums = _dot(cms_ref[...], jnp.concatenate([g_hi, g_lo], axis=0))
    b = sums[0:CHUNK]
    att = lm_ref[0] * _dot_nt(q.astype(BF16), per_head_rows(k))
    for lv in range(1, N_LEVEL + 1):
        e = jnp.exp(sums[lv * CHUNK:(lv + 1) * CHUNK])
        att = att + lm_ref[lv] * _dot_nt((q * e).astype(BF16), per_head_rows(k * e))
    v_rows = jnp.concatenate([v16] * H_A, axis=0) * vm_ref[...]
    qe = (q * jnp.exp(b)).astype(BF16)
    out = _dot(att.astype(BF16), v_rows) + _dot_nt(qe, st.astype(BF16) * sm_ref[...])
    ke = (k * jnp.exp(sums[(N_LEVEL + 1) * CHUNK:(N_LEVEL + 2) * CHUNK])).astype(BF16)
    upd = pl.dot(v16, ke, trans_a=True)
    b_end = b[0:1] if rev else b[CHUNK - 1:CHUNK]
    return out, st * jnp.exp(b_end) + upd


def _gla_kernel(*refs, has_init, emit_state):
    it = iter(refs)
    qf_ref, kf_ref, vf_ref, gf_ref, qb_ref, kb_ref, vb_ref, gb_ref = (next(it) for _ in range(8))
    if has_init:
        s0f_ref, s0b_ref = next(it), next(it)
    cmsf_ref, lmf_ref, cmsb_ref, lmb_ref, km_ref, vm_ref, sm_ref = (next(it) for _ in range(7))
    of_ref, ob_ref = next(it), next(it)
    if emit_state:
        sff_ref, sfb_ref = next(it), next(it)
    stf_ref, stb_ref = next(it), next(it)
    ti = pl.program_id(1)
    nt = pl.num_programs(1)

    @pl.when(ti == 0)
    def _():
        if has_init:
            stf_ref[...] = s0f_ref[0]
            stb_ref[...] = s0b_ref[0]
        else:
            stf_ref[...] = jnp.zeros(stf_ref.shape, F32)
            stb_ref[...] = jnp.zeros(stb_ref.shape, F32)

    n_chunk = qf_ref.shape[1] // CHUNK
    stf = stf_ref[...]
    stb = stb_ref[...]
    for ci in range(n_chunk):
        rows = slice(ci * CHUNK, (ci + 1) * CHUNK)
        out, stf = _gla_chunk(qf_ref[0, rows, :], kf_ref[0, rows, :], vf_ref[0, rows, :], gf_ref[0, rows, :], stf,
                              cmsf_ref, lmf_ref, km_ref, vm_ref, sm_ref, False)
        of_ref[0, rows, :] = out
        cb = n_chunk - 1 - ci
        rows = slice(cb * CHUNK, (cb + 1) * CHUNK)
        out, stb = _gla_chunk(qb_ref[0, rows, :], kb_ref[0, rows, :], vb_ref[0, rows, :], gb_ref[0, rows, :], stb,
                              cmsb_ref, lmb_ref, km_ref, vm_ref, sm_ref, True)
        ob_ref[0, rows, :] = out
    stf_ref[...] = stf
    stb_ref[...] = stb

    if emit_state:
        @pl.when(ti == nt - 1)
        def _():
            for dst, st_ref in ((sff_ref, stf_ref), (sfb_ref, stb_ref)):
                blocks = st_ref[...].T
                for hd in range(H_A):
                    dst[0, hd] = blocks[hd * DK_A:(hd + 1) * DK_A, hd * DV_A:(hd + 1) * DV_A]


def _gla_scan(q3, k3, v3, gf3, gb3, init, consts, *, emit_state):
    bsz, s, _ = q3.shape
    tt = GLA_TILE
    nt = s // tt
    fmap = lambda b, t: (b, t, 0)
    rmap = lambda b, t: (b, nt - 1 - t, 0)
    bmap = lambda b, t: (b, 0, 0)
    const = lambda shape: pl.BlockSpec(shape, lambda b, t: (0,) * len(shape))
    tile = lambda w, m: pl.BlockSpec((1, tt, w), m)
    state = pl.BlockSpec((1, H_A * DV_A, H_A * DK_A), bmap)
    in_specs = [tile(256, fmap), tile(256, fmap), tile(512, fmap), tile(256, fmap),
                tile(256, rmap), tile(256, rmap), tile(512, rmap), tile(256, rmap)]
    args = [q3, k3, v3, gf3, q3, k3, v3, gb3]
    if init is not None:
        in_specs += [state, state]
        args += list(init)
    in_specs += [const(c.shape) for c in consts]
    args += list(consts)
    out_specs = [tile(W_A, fmap), tile(W_A, rmap)]
    out_shape = [jax.ShapeDtypeStruct((bsz, s, W_A), F32)] * 2
    if emit_state:
        out_specs += [pl.BlockSpec((1, H_A, DK_A, DV_A), lambda b, t: (b, 0, 0, 0))] * 2
        out_shape += [jax.ShapeDtypeStruct((bsz, H_A, DK_A, DV_A), F32)] * 2
    return pl.pallas_call(
        functools.partial(_gla_kernel, has_init=init is not None, emit_state=emit_state),
        grid=(bsz, nt),
        in_specs=in_specs,
        out_specs=out_specs,
        out_shape=out_shape,
        scratch_shapes=[pltpu.VMEM((H_A * DV_A, H_A * DK_A), F32)] * 2,
        compiler_params=pltpu.CompilerParams(dimension_semantics=("arbitrary", "arbitrary"),
                                             vmem_limit_bytes=VMEM_LIMIT),
        name="gla_lat" if init is not None else "gla_ctx",
    )(*args)


def _state_to_blocks(s0):
    st = jnp.swapaxes(s0, -1, -2)
    eye = jnp.eye(H_A, dtype=s0.dtype)
    full = st[:, :, :, None, :] * eye[None, :, None, :, None]
    return full.reshape(s0.shape[0], H_A * DV_A, H_A * DK_A)


def _online_step(carry, q, kt, vt):
    m, acc = carry
    s = _dot_nt(q, kt)
    m_new = jnp.maximum(m, jnp.max(s, axis=-1, keepdims=True))
    p = jnp.exp2(s - m_new)
    acc = jnp.exp2(m - m_new) * acc + _dot(p.astype(BF16), vt)
    return m_new, acc


def _key_rounds(n_keys, n_rounds, align):
    units = n_keys // align
    assert units * align == n_keys
    n_rounds = 1 if n_keys <= 2 * MXU_WIDTH else min(n_rounds, units)
    sizes = [(units // n_rounds + (1 if r < units % n_rounds else 0)) * align for r in range(n_rounds)]
    starts = np.concatenate([[0], np.cumsum(sizes)[:-1]])
    return [(int(s), int(z)) for s, z in zip(starts, sizes)]


def _attend(qs, k_ref, v_ref, ksls, vsls, n_rounds, align):
    tq = qs[0].shape[0]
    n = len(qs)
    width = vsls[0].stop - vsls[0].start
    init = (jnp.full((tq, 1), NEG_BIG, F32), jnp.zeros((tq, width), F32))
    carries = [init for _ in range(n)]
    for start, size in _key_rounds(k_ref.shape[1], n_rounds, align):
        rows = slice(start, start + size)
        carries = [_online_step(carries[i], qs[i], k_ref[0, rows, ksls[i]], v_ref[0, rows, vsls[i]])
                   for i in range(n)]
    return [c[1] for c in carries]


def _heads_per_step(n_heads, n_keys, at_least):
    return n_heads if n_keys <= 2 * MXU_WIDTH else at_least


def _mla_kernel(q_ref, k_ref, v_ref, o_ref):
    n_heads = q_ref.shape[2] // HEAD_PAD
    sls = [slice(hd * HEAD_PAD, (hd + 1) * HEAD_PAD) for hd in range(n_heads)]
    accs = _attend([q_ref[0, :, sl] for sl in sls], k_ref, v_ref, sls, sls, MLA_ROUNDS, MXU_WIDTH)
    outs = [acc / acc[:, V_B:V_B + 1] for acc in accs]
    lane = _lane_iota(outs[0].shape)
    pairs = [jnp.where(lane < V_B, outs[2 * p], pltpu.roll(outs[2 * p + 1], V_B, 1)) for p in range(n_heads // 2)]
    o_ref[0] = jnp.concatenate(pairs, axis=1).astype(o_ref.dtype)


def _mla_attention(q3, k3, v3):
    bsz, s, _ = q3.shape
    keys = k3.shape[1]
    tq = min(Q_TILE, s)
    nh = _heads_per_step(H_B, keys, 2)
    pair = nh * HEAD_PAD
    return pl.pallas_call(
        _mla_kernel,
        grid=(bsz, H_B // nh, s // tq),
        in_specs=[
            pl.BlockSpec((1, tq, pair), lambda b, h, i: (b, i, h)),
            pl.BlockSpec((1, keys, pair), lambda b, h, i: (b, 0, h)),
            pl.BlockSpec((1, keys, pair), lambda b, h, i: (b, 0, h)),
        ],
        out_specs=pl.BlockSpec((1, tq, nh * V_B), lambda b, h, i: (b, i, h)),
        out_shape=jax.ShapeDtypeStruct((bsz, s, W_B), BF16),
        compiler_params=pltpu.CompilerParams(dimension_semantics=("arbitrary",) * 3,
                                             vmem_limit_bytes=VMEM_LIMIT),
        name="mla_lat" if keys > s else "mla_ctx",
    )(q3, k3, v3)


def _diff_kernel(lam_ref, q_ref, k_ref, v_ref, o_ref, *, lam_init):
    lq = lam_ref[...]
    lam = (jnp.exp(jnp.sum(lq[0:1] * lq[1:2], axis=-1, keepdims=True))
           - jnp.exp(jnp.sum(lq[2:3] * lq[3:4], axis=-1, keepdims=True)) + lam_init)
    n_heads = q_ref.shape[2] // LANES
    lane = _lane_iota((q_ref.shape[1], LANES))
    qcs, ksls, vsls = [], [], []
    for hd in range(n_heads):
        q = q_ref[0, :, hd * LANES:(hd + 1) * LANES].astype(F32)
        qcs += [jnp.where(lane < D_C, q, 0.0).astype(BF16), jnp.where(lane >= D_C, q, 0.0).astype(BF16)]
        ksls += [slice(hd * LANES, (hd + 1) * LANES)] * 2
        vsls += [slice(2 * hd * LANES, 2 * (hd + 1) * LANES)] * 2
    accs = _attend(qcs, k_ref, v_ref, ksls, vsls, DIFF_ROUNDS, LANES)
    res = [acc[:, 0:LANES] / acc[:, LANES:LANES + 1] for acc in accs]
    o_ref[0] = jnp.concatenate([res[2 * hd] - lam * res[2 * hd + 1] for hd in range(n_heads)], axis=1)


def _diff_attention(lam_qk, q3, k3, v3, lam_init):
    bsz, s, _ = q3.shape
    keys = k3.shape[1]
    tq = min(Q_TILE, s)
    nh = _heads_per_step(H_C, keys, 1)
    return pl.pallas_call(
        functools.partial(_diff_kernel, lam_init=lam_init),
        grid=(bsz, H_C // nh, s // tq),
        in_specs=[
            pl.BlockSpec((4, D_C), lambda b, h, i: (0, 0)),
            pl.BlockSpec((1, tq, nh * LANES), lambda b, h, i: (b, i, h)),
            pl.BlockSpec((1, keys, nh * LANES), lambda b, h, i: (b, 0, h)),
            pl.BlockSpec((1, keys, 2 * nh * LANES), lambda b, h, i: (b, 0, h)),
        ],
        out_specs=pl.BlockSpec((1, tq, nh * LANES), lambda b, h, i: (b, i, h)),
        out_shape=jax.ShapeDtypeStruct((bsz, s, W_C), F32),
        compiler_params=pltpu.CompilerParams(dimension_semantics=("arbitrary",) * 3,
                                             vmem_limit_bytes=VMEM_LIMIT),
        name="diff_lat" if keys > s else "diff_ctx",
    )(lam_qk, q3, k3, v3)


def _merge_kernel(x_ref, mod_ref, g1_ref, wg_ref, of_ref, ob_ref, sr_ref, yb_ref, yc_ref, ggla_ref, gsub_ref,
                  woa_ref, wob_ref, woc_ref, wout_ref, g2_ref, w1_ref, w2_ref, o_ref, *, lam_init):
    x = x_ref[...]
    m = mod_ref[0]
    h = _rms(x, g1_ref[...]) * (1.0 + m[1:2]) + m[0:1]
    gates = _sigmoid(_dot(h.astype(BF16), wg_ref[...]))
    o = of_ref[...] + ob_ref[...]
    sr = sr_ref[...]
    yc = yc_ref[...]
    ya_t, yc_t = [], []
    for hd in range(H_A):
        sl = slice(hd * LANES, (hd + 1) * LANES)
        ya_t.append(_rms(o[:, sl], ggla_ref[...]) * sr[:, sl])
        yc_t.append(_rms(yc[:, sl], gsub_ref[...]) * (1.0 - lam_init))
    ya = jnp.concatenate(ya_t, axis=1).astype(BF16)
    ycn = jnp.concatenate(yc_t, axis=1).astype(BF16)
    merged = (gates[:, 0:D_MODEL] * _dot(ya, woa_ref[...])
              + gates[:, D_MODEL:2 * D_MODEL] * _dot(yb_ref[...], wob_ref[...])
              + gates[:, 2 * D_MODEL:3 * D_MODEL] * _dot(ycn, woc_ref[...]))
    x1 = x + m[2:3] * _dot(merged.astype(BF16), wout_ref[...])

    h2 = _rms(x1, g2_ref[...]) * (1.0 + m[4:5]) + m[3:4]
    a = jnp.maximum(_dot(h2.astype(BF16), w1_ref[...]), 0.0)
    o_ref[...] = x1 + m[5:6] * _dot((a * a).astype(BF16), w2_ref[...])


def _merge_mlp(x2, mod8, group0, lw, of2, ob2, sr2, yb2, yc2, lam_init, per_seq_groups):
    t = x2.shape[0]
    tm = TOKEN_TILE
    tiles_per_seq = LAT_SEQ // tm
    const = lambda shape: pl.BlockSpec(shape, lambda i: (0,) * len(shape), pipeline_mode=pl.Buffered(1))
    row = lambda w: pl.BlockSpec((tm, w), lambda i: (i, 0))
    if per_seq_groups:
        mod_spec = pl.BlockSpec((1, 8, D_MODEL), lambda i: (group0 + i // tiles_per_seq, 0, 0))
    else:
        mod_spec = pl.BlockSpec((1, 8, D_MODEL), lambda i: (group0, 0, 0))
    return pl.pallas_call(
        functools.partial(_merge_kernel, lam_init=lam_init),
        grid=(t // tm,),
        in_specs=[row(D_MODEL), mod_spec, const((1, D_MODEL)), const((D_MODEL, GATE_COLS)),
                  row(W_A), row(W_A), row(W_A), row(W_B), row(W_C), const((1, LANES)), const((1, LANES)),
                  const((W_A, D_MODEL)), const((W_B, D_MODEL)), const((W_C, D_MODEL)),
                  const((D_MODEL, D_MODEL)), const((1, D_MODEL)), const((D_MODEL, D_FF)), const((D_FF, D_MODEL))],
        out_specs=row(D_MODEL),
        out_shape=jax.ShapeDtypeStruct((t, D_MODEL), F32),
        compiler_params=pltpu.CompilerParams(dimension_semantics=("arbitrary",), vmem_limit_bytes=VMEM_LIMIT),
        name="merge_mlp",
    )(x2, mod8, lw["g1"], lw["w_gates"], of2, ob2, sr2, yb2, yc2, lw["ggla"], lw["gsub"],
      lw["woa"], lw["wob"], lw["woc"], lw["wout"], lw["g2"], lw["w1"], lw["w2"])


def _pad_heads(w, real, pad):
    k = w.shape[0]
    w = w.reshape(k, -1, real)
    return jnp.pad(w, ((0, 0), (0, 0), (0, pad - real))).reshape(k, -1)


def _layer_weights(l, p):
    widths = (H_A * DK_A, H_A * DK_A, W_A, W_A, 2 * GATE_RANK, Q_RANK, KV_RANK, ROPE_B, W_C, W_C, W_C, GATE_COLS)
    offs = np.concatenate([[0], np.cumsum(widths)])
    w_in = p["w_in"][l]
    aq, ak, av, ar, aa, qd, kvd, kr, dq, dk, dv, gates = (w_in[:, offs[i]:offs[i + 1]] for i in range(12))
    z = lambda n: jnp.zeros((D_MODEL, n), F32)
    misc = jnp.concatenate([aa, kr, z(LANES - 2 * GATE_RANK - ROPE_B)], axis=1)
    krp = jnp.concatenate([z(NOPE_B), kr, z(LANES - QK_B)], axis=1)
    part = np.arange(ROPE_B) ^ (ROPE_B // 4)
    krq = jnp.concatenate([z(NOPE_B), kr[:, part], z(LANES - QK_B)], axis=1)
    w_in_a = jnp.concatenate([aq, ak, av, ar, qd, kvd, dq, dk, dv, misc, krp, krq], axis=1).astype(BF16)
    uq = p["w_mla_uq"][l].reshape(Q_RANK, H_B, QK_B)
    uq_part = jnp.concatenate([jnp.zeros((Q_RANK, H_B, NOPE_B), F32), uq[:, :, NOPE_B + part]], axis=2)
    gain_part = lambda g: jnp.pad(g[NOPE_B + part], (NOPE_B, LANES - QK_B)).reshape(1, LANES)
    wa2 = jnp.zeros((LANES, 512), F32)
    wa2 = wa2.at[0:GATE_RANK, 0:256].set(p["w_gla_a2"][l, 0])
    wa2 = wa2.at[GATE_RANK:2 * GATE_RANK, 256:512].set(p["w_gla_a2"][l, 1])
    pad_gain = lambda g: jnp.pad(g, (0, HEAD_PAD - QK_B)).reshape(1, HEAD_PAD)
    twice = lambda g: jnp.concatenate([g, g]).reshape(1, LANES)
    return dict(
        g1=p["g_norm1"][l].reshape(1, D_MODEL), g2=p["g_norm2"][l].reshape(1, D_MODEL),
        w_in_a=w_in_a, w_gates=gates.astype(BF16),
        wa2=wa2.astype(BF16), ba=p["b_gla_a"][l].reshape(1, 512),
        gqa=p["g_mla_qa"][l].reshape(1, Q_RANK), gkva=p["g_mla_kva"][l].reshape(1, KV_RANK),
        wuq=_pad_heads(p["w_mla_uq"][l], QK_B, HEAD_PAD).astype(BF16),
        wuk=_pad_heads(p["w_mla_uk"][l], NOPE_B, HEAD_PAD).astype(BF16),
        wuv=_pad_heads(p["w_mla_uv"][l], V_B, HEAD_PAD).astype(BF16),
        gq=pad_gain(p["g_mla_q"][l]), gk=pad_gain(p["g_mla_k"][l]),
        wuq_part=_pad_heads(uq_part.reshape(Q_RANK, H_B * QK_B), QK_B, HEAD_PAD).astype(BF16),
        gq_part=gain_part(p["g_mla_q"][l]), gk_part=gain_part(p["g_mla_k"][l]),
        gdq=twice(p["g_diff_q"][l]), gdk=twice(p["g_diff_k"][l]),
        ggla=p["g_gla_out"][l].reshape(1, DV_A), gsub=p["g_diff_sub"][l].reshape(1, 2 * D_C),
        woa=p["w_o_gla"][l].astype(BF16), wob=p["w_o_mla"][l].astype(BF16), woc=p["w_o_diff"][l].astype(BF16),
        wout=p["w_out"][l].astype(BF16), w1=p["w_mlp1"][l].astype(BF16), w2=p["w_mlp2"][l].astype(BF16),
        lam_qk=p["lam_qk"][l],
    )


def _layer(x2, bsz, seq, mod8, group0, lw, lam_init, consts, tabs, cached):
    is_latent = cached is not None
    cache = None
    if is_latent:
        s0, ckv_c, krope_c, dk_c, dv_c = cached
        past = ckv_c.shape[1]
        krp_c = jnp.pad(krope_c, ((0, 0), (0, 0), (NOPE_B, LANES - QK_B)))
        cache = (ckv_c, krp_c, dk_c.reshape(bsz, past, W_C), dv_c.reshape(bsz, past, W_C))
    outs = _inproj(x2, bsz, mod8, group0, lw, tabs, cache, rope=is_latent, emit_state=not is_latent)
    gq, gk, gv, sr, gf, gb, mq, mk, mv, dq, dk, dv = outs[:12]
    r3 = lambda a: a.reshape(bsz, seq, a.shape[-1])

    init = (_state_to_blocks(s0[:, 0]), _state_to_blocks(s0[:, 1])) if is_latent else None
    gla = _gla_scan(r3(gq), r3(gk), r3(gv), r3(gf), r3(gb), init, consts, emit_state=not is_latent)
    of, ob = gla[0], gla[1]

    yb = _mla_attention(r3(mq), mk, mv)
    yc = _diff_attention(lw["lam_qk"], r3(dq), dk, dv, lam_init)

    x_out = _merge_mlp(x2, mod8, group0, lw, of.reshape(-1, W_A), ob.reshape(-1, W_A), sr, yb.reshape(-1, W_B),
                       yc.reshape(-1, W_C), lam_init, is_latent)

    new_state = None
    if not is_latent:
        ckv, misc, dk32, dv32 = outs[12:]
        gla_state = jnp.stack([gla[2], gla[3]], axis=1)
        new_state = (gla_state, ckv.reshape(bsz, seq, KV_RANK),
                     misc[:, 2 * GATE_RANK:2 * GATE_RANK + ROPE_B].reshape(bsz, seq, ROPE_B),
                     dk32.reshape(bsz, seq, H_C, 2, D_C), dv32.reshape(bsz, seq, H_C, 2 * D_C))
    return x_out, new_state


def kernel(x_prompt, x_sample, state_gla, cache_mla_ckv, cache_mla_krope, cache_diff_k, cache_diff_v, c, c_ctx, w_mod, b_mod, g_norm1, g_norm2, w_in, w_gla_a2, b_gla_a, g_gla_out, g_mla_qa, g_mla_kva, w_mla_uq, w_mla_uk, w_mla_uv, g_mla_q, g_mla_k, g_diff_q, g_diff_k, lam_qk, g_diff_sub, w_o_gla, w_o_mla, w_o_diff, w_out, w_mlp1, w_mlp2):
    params = dict(w_in=w_in, g_norm1=g_norm1, g_norm2=g_norm2, w_gla_a2=w_gla_a2, b_gla_a=b_gla_a,
                  g_gla_out=g_gla_out, g_mla_qa=g_mla_qa, g_mla_kva=g_mla_kva, w_mla_uq=w_mla_uq,
                  w_mla_uk=w_mla_uk, w_mla_uv=w_mla_uv, g_mla_q=g_mla_q, g_mla_k=g_mla_k, g_diff_q=g_diff_q,
                  g_diff_k=g_diff_k, lam_qk=lam_qk, g_diff_sub=g_diff_sub, w_o_gla=w_o_gla, w_o_mla=w_o_mla,
                  w_o_diff=w_o_diff, w_out=w_out, w_mlp1=w_mlp1, w_mlp2=w_mlp2)
    nb, ns, _ = x_prompt.shape
    db, ds, _ = x_sample.shape
    assert ds == LAT_SEQ and db + 1 <= 8

    cond8 = jnp.concatenate([c_ctx[None, :], c, jnp.zeros((8 - 1 - db, D_MODEL), F32)], axis=0)
    mod = _modulation(cond8, w_mod, b_mod).reshape(DEPTH, 8, 6, D_MODEL)
    mod = jnp.pad(mod, ((0, 0), (0, 0), (0, 2), (0, 0)))

    (fwd_c, head_masks), (bwd_c, _) = _gla_constants(False), _gla_constants(True)
    consts = fwd_c + bwd_c + head_masks
    tabs = (_rope_tables(ROPE_B, NOPE_B, LANES), _rope_tables(D_C, 0, D_C))

    yp = x_prompt.reshape(nb * ns, D_MODEL)
    ys = x_sample.reshape(db * ds, D_MODEL)
    states = []
    for l in range(DEPTH):
        lw = _layer_weights(l, params)
        lam_init = 0.8 - 0.6 * math.exp(-0.3 * l)
        yp, st = _layer(yp, nb, ns, mod[l], 0, lw, lam_init, consts, tabs, None)
        states.append(st)
        cached = (state_gla[:, l], cache_mla_ckv[:, l], cache_mla_krope[:, l], cache_diff_k[:, l],
                  cache_diff_v[:, l])
        ys, _ = _layer(ys, db, ds, mod[l], 1, lw, lam_init, consts, tabs, cached)
    stack = lambda i: jnp.stack([s[i] for s in states], axis=1)
    return (yp.reshape(nb, ns, D_MODEL), ys.reshape(db, ds, D_MODEL), stack(0), stack(1), stack(2), stack(3),
            stack(4))
```

```python
import functools
import math

import numpy as np
import jax
import jax.numpy as jnp
from jax import lax
from jax.experimental import pallas as pl
from jax.experimental.pallas import tpu as pltpu

F32 = jnp.float32
BF16 = jnp.bfloat16

D_MODEL = 1024
DEPTH = 2
GRID_W = 64
ROPE_BASE = 10000.0
EPS = 1e-6
H_A, DK_A, DV_A = 4, 64, 128
GATE_RANK = 16
GATE_TAU = 16.0
CHUNK = 64
W_A = H_A * DV_A
H_B, Q_RANK, KV_RANK = 8, 384, 256
NOPE_B, ROPE_B, V_B = 64, 32, 64
QK_B = NOPE_B + ROPE_B
W_B = H_B * V_B
H_C, D_C = 4, 64
W_C = H_C * 2 * D_C
D_FF = 4 * D_MODEL
GATE_COLS = 3 * D_MODEL
LAT_SEQ = 4096

LANES = 128
HEAD_PAD = LANES
MXU_WIDTH = 256
VMEM_LIMIT = 56 * 1024 * 1024

TOKEN_TILE = 256
Q_TILE = 1024
MLA_ROUNDS = 2
DIFF_ROUNDS = 2
GLA_TILE = 256

C_AQ, C_AK, C_AV, C_AR = 0, 256, 512, 1024
C_QD, C_KVD = 1536, 1920
C_DQ, C_DK, C_DV = 2176, 2688, 3200
C_MISC, C_KRP, C_KRQ = 3712, 3840, 3968
IN_COLS_A = 4096

NEG_BIG = -1e30
LOG2E = math.log2(math.e)


def _dot(a, b):
    return jnp.dot(a, b, preferred_element_type=F32)


def _dot_nt(a, b):
    return lax.dot_general(a, b, (((1,), (1,)), ((), ())), preferred_element_type=F32)


def _sigmoid(x):
    return 1.0 / (1.0 + jnp.exp(-x))


def _rms(x, g):
    ms = jnp.mean(x * x, axis=-1, keepdims=True)
    return x * lax.rsqrt(ms + EPS) * g


def _lane_iota(shape):
    return lax.broadcasted_iota(jnp.int32, shape, len(shape) - 1)


def _ones_lane0(shape):
    return jnp.where(_lane_iota(shape) % LANES == 0, 1.0, 0.0).astype(F32)


def _mod_kernel(c_ref, w_ref, b_ref, o_ref):
    c = c_ref[...]
    s = c * _sigmoid(c)
    o_ref[0] = _dot(s.astype(BF16), w_ref[0].astype(BF16)) + b_ref[0]


def _modulation(cond8, w_mod, b_mod):
    nt = 768
    return pl.pallas_call(
        _mod_kernel,
        grid=(DEPTH, 6 * D_MODEL // nt),
        in_specs=[
            pl.BlockSpec((8, D_MODEL), lambda l, j: (0, 0)),
            pl.BlockSpec((1, D_MODEL, nt), lambda l, j: (l, 0, j)),
            pl.BlockSpec((1, 1, nt), lambda l, j: (l, 0, j)),
        ],
        out_specs=pl.BlockSpec((1, 8, nt), lambda l, j: (l, 0, j)),
        out_shape=jax.ShapeDtypeStruct((DEPTH, 8, 6 * D_MODEL), F32),
        name="modulation",
    )(cond8, w_mod, b_mod.reshape(DEPTH, 1, 6 * D_MODEL))


def _rope_tables(n_rope, lane0, width):
    half = n_rope // 2
    nf = half // 2
    inv = ROPE_BASE ** (-np.arange(nf, dtype=np.float64) / nf)
    pos = np.arange(GRID_W, dtype=np.float64)[:, None]
    cr = np.zeros((GRID_W, LANES)); cc = np.ones((GRID_W, LANES))
    sr = np.zeros((GRID_W, LANES)); sc = np.zeros((GRID_W, LANES))
    for base in range(0, LANES, width):
        for d in range(n_rope):
            lane = base + lane0 + d
            within = d % half
            ang = pos[:, 0] * inv[within % nf]
            sign = -1.0 if within < nf else 1.0
            if d < half:
                cr[:, lane] = np.cos(ang); cc[:, lane] = 0.0
                sr[:, lane] = sign * np.sin(ang)
            else:
                cc[:, lane] = np.cos(ang)
                sc[:, lane] = sign * np.sin(ang)
    return tuple(jnp.asarray(t, F32) for t in (cr, cc, sr, sc))


def _tile_tables(tabs, row0, n_sub):
    cr_ref, cc_ref, sr_ref, sc_ref = tabs
    cs, ss = [], []
    for j in range(n_sub):
        r = (row0 + j) % GRID_W
        cs.append(cr_ref[pl.ds(r, 1), :] + cc_ref[...])
        ss.append(sr_ref[pl.ds(r, 1), :] + sc_ref[...])
    return jnp.concatenate(cs, axis=0), jnp.concatenate(ss, axis=0)


def _rotate(x, cos_t, sin_t, nf):
    lane = _lane_iota(x.shape)
    up = pltpu.roll(x, LANES - nf, 1)
    dn = pltpu.roll(x, nf, 1)
    partner = jnp.where((lane & nf) == 0, up, dn)
    return x * cos_t + partner * sin_t


def _inproj_kernel(*refs, rope, emit_state, tiles_per_seq, has_cache):
    it = iter(refs)
    if has_cache:
        ckvc_ref, krpc_ref, dkc_ref, dvc_ref = (next(it) for _ in range(4))
    x_ref, mod_ref, g1_ref, w_ref, wa2_ref, ba_ref = (next(it) for _ in range(6))
    gqa_ref, gkva_ref, wuq_ref, wuk_ref, wuv_ref, gq_ref, gk_ref = (next(it) for _ in range(7))
    gdq_ref, gdk_ref = next(it), next(it)
    if rope:
        tab_m = tuple(next(it) for _ in range(4))
        tab_d = tuple(next(it) for _ in range(4))
        wuqp_ref, gqp_ref, gkp_ref = (next(it) for _ in range(3))
    o_gq, o_gk, o_gv, o_sr, o_gf, o_gb = (next(it) for _ in range(6))
    o_mq, o_mk, o_mv, o_dq, o_dk, o_dv = (next(it) for _ in range(6))
    if emit_state:
        o_ckv, o_misc, o_dk32, o_dv32 = (next(it) for _ in range(4))

    def mla_keys(kn, krp, dst):
        for hd in range(H_B):
            sl = slice(hd * HEAD_PAD, (hd + 1) * HEAD_PAD)
            kt = kn[:, sl] + krp
            kt = kt * lax.rsqrt(jnp.sum(kt * kt, axis=-1, keepdims=True) * (1.0 / QK_B) + EPS) * gk_ref[...]
            dst[0, :, sl] = kt.astype(BF16)

    def with_ones(v):
        one_tile = _ones_lane0((v.shape[0], LANES))
        return jnp.concatenate(
            [t for hd in range(H_C) for t in (v[:, hd * LANES:(hd + 1) * LANES], one_tile)], axis=1).astype(BF16)

    def cache_rows():
        c = ckvc_ref[0].astype(BF16)
        vb = _dot(c, wuv_ref[...])
        o_mv[0] = jnp.where(_lane_iota(vb.shape) % HEAD_PAD == V_B, 1.0, vb).astype(BF16)
        mla_keys(_dot(c, wuk_ref[...]), krpc_ref[0], o_mk)
        o_dk[0] = dkc_ref[0].astype(BF16)
        o_dv[0] = with_ones(dvc_ref[0])

    def token_rows(seq_tile):
        tm = x_ref.shape[0]
        x = x_ref[...]
        m = mod_ref[0]
        h = _rms(x, g1_ref[...]) * (1.0 + m[1:2]) + m[0:1]
        y = _dot(h.astype(BF16), w_ref[...])

        o_gq[...] = y[:, C_AQ:C_AQ + 256] * (DK_A ** -0.5)
        o_gk[...] = y[:, C_AK:C_AK + 256]
        o_gv[...] = y[:, C_AV:C_AV + 512].astype(BF16)
        r = y[:, C_AR:C_AR + 512]
        o_sr[...] = r * _sigmoid(r)
        misc = y[:, C_MISC:C_MISC + LANES]
        logits = _dot(misc.astype(BF16), wa2_ref[...]) + ba_ref[...]
        logsig = jnp.minimum(logits, 0.0) - jnp.log(1.0 + jnp.exp(-jnp.abs(logits)))
        gate = logsig * (1.0 / GATE_TAU)
        o_gf[...] = gate[:, 0:256]
        o_gb[...] = gate[:, 256:512]

        if rope:
            row0 = seq_tile * (tm // GRID_W)
            cos_m, sin_m = _tile_tables(tab_m, row0, tm // GRID_W)
            cos_d, sin_d = _tile_tables(tab_d, row0, tm // GRID_W)

        qn = _rms(y[:, C_QD:C_QD + Q_RANK], gqa_ref[...])
        q8 = _dot(qn.astype(BF16), wuq_ref[...])
        ckv = _rms(y[:, C_KVD:C_KVD + KV_RANK], gkva_ref[...])
        kn = _dot(ckv.astype(BF16), wuk_ref[...])
        vb = _dot(ckv.astype(BF16), wuv_ref[...])
        o_mv[0] = jnp.where(_lane_iota(vb.shape) % HEAD_PAD == V_B, 1.0, vb).astype(BF16)
        krp = y[:, C_KRP:C_KRP + LANES]
        if rope:
            q8p = _dot(qn.astype(BF16), wuqp_ref[...])
            gq_sin = gqp_ref[...] * sin_m
            k_part = y[:, C_KRQ:C_KRQ + LANES] * (gkp_ref[...] * sin_m)
        for hd in range(H_B):
            sl = slice(hd * HEAD_PAD, (hd + 1) * HEAD_PAD)
            qt = q8[:, sl]
            rq = lax.rsqrt(jnp.sum(qt * qt, axis=-1, keepdims=True) * (1.0 / QK_B) + EPS)
            qt = qt * rq * gq_ref[...]
            kt = kn[:, sl] + krp
            rk = lax.rsqrt(jnp.sum(kt * kt, axis=-1, keepdims=True) * (1.0 / QK_B) + EPS)
            kt = kt * rk * gk_ref[...]
            if rope:
                qt = qt * cos_m + (q8p[:, sl] * gq_sin) * rq
                kt = kt * cos_m + k_part * rk
            o_mq[:, sl] = (qt * (QK_B ** -0.5 * LOG2E)).astype(BF16)
            o_mk[0, :, sl] = kt.astype(BF16)

        lane = _lane_iota((tm, LANES))
        lo = lane < D_C
        for hd in range(H_C):
            sl = slice(hd * LANES, (hd + 1) * LANES)
            for src, g_ref, dst, dst32, scale in (
                (C_DQ, gdq_ref, o_dq, None, D_C ** -0.5 * LOG2E),
                (C_DK, gdk_ref, o_dk.at[0], o_dk32 if emit_state else None, 1.0),
            ):
                t = y[:, src + hd * LANES:src + (hd + 1) * LANES]
                sq = t * t
                s_lo = jnp.sum(jnp.where(lo, sq, 0.0), axis=-1, keepdims=True)
                s_hi = jnp.sum(jnp.where(lo, 0.0, sq), axis=-1, keepdims=True)
                inv = jnp.where(lo, lax.rsqrt(s_lo * (1.0 / D_C) + EPS), lax.rsqrt(s_hi * (1.0 / D_C) + EPS))
                t = t * inv * g_ref[...]
                if rope:
                    t = _rotate(t, cos_d, sin_d, D_C // 4)
                if dst32 is not None:
                    dst32[:, sl] = t
                dst[:, sl] = (t * scale).astype(BF16)
        dv = y[:, C_DV:C_DV + W_C]
        o_dv[0] = with_ones(dv)
        if emit_state:
            o_dv32[...] = dv
            o_ckv[...] = ckv
            o_misc[...] = misc

    if not has_cache:
        token_rows(pl.program_id(0) % tiles_per_seq)
        return
    step = pl.program_id(0) % (tiles_per_seq + 1)
    pl.when(step == 0)(cache_rows)
    pl.when(step > 0)(lambda: token_rows(step - 1))


def _inproj(x2, bsz, mod8, group0, lw, tabs, cache, *, rope, emit_state):
    t = x2.shape[0]
    tm = TOKEN_TILE
    seq = t // bsz
    tps = seq // tm
    has_cache = cache is not None
    past = cache[0].shape[1] if has_cache else 0
    assert past in (0, tm)
    lead = 1 if has_cache else 0
    req = lambda i: i // (tps + lead)
    tile = lambda i: jnp.maximum(i % (tps + lead) - lead, 0)
    kv_rows = lambda w: pl.BlockSpec((1, tm, w), lambda i: (req(i), i % (tps + lead), 0))
    const = lambda shape: pl.BlockSpec(shape, lambda i: (0,) * len(shape))
    row = lambda w: pl.BlockSpec((tm, w), lambda i: (req(i) * tps + tile(i), 0))
    if rope:
        mod_spec = pl.BlockSpec((1, 8, D_MODEL), lambda i: (group0 + req(i), 0, 0))
    else:
        mod_spec = pl.BlockSpec((1, 8, D_MODEL), lambda i: (group0, 0, 0))
    cache_specs = [pl.BlockSpec((1, past, c.shape[-1]), lambda i: (req(i), 0, 0)) for c in cache] if has_cache else []
    in_specs = cache_specs + [
        row(D_MODEL), mod_spec, const((1, D_MODEL)), const((D_MODEL, IN_COLS_A)),
        const((LANES, 512)), const((1, 512)),
        const((1, Q_RANK)), const((1, KV_RANK)), const((Q_RANK, H_B * HEAD_PAD)),
        const((KV_RANK, H_B * HEAD_PAD)), const((KV_RANK, H_B * HEAD_PAD)), const((1, LANES)), const((1, LANES)),
        const((1, LANES)), const((1, LANES)),
    ]
    args = (list(cache) if has_cache else []) + [
        x2, mod8, lw["g1"], lw["w_in_a"], lw["wa2"], lw["ba"], lw["gqa"], lw["gkva"], lw["wuq"],
        lw["wuk"], lw["wuv"], lw["gq"], lw["gk"], lw["gdq"], lw["gdk"]]
    if rope:
        in_specs += [const((GRID_W, LANES))] * 8 + [const((Q_RANK, H_B * HEAD_PAD)), const((1, LANES)), const((1, LANES))]
        args += list(tabs[0]) + list(tabs[1]) + [lw["wuq_part"], lw["gq_part"], lw["gk_part"]]
    flat, keyed = "flat", "keyed"
    outs = [(256, F32, flat), (256, F32, flat), (512, BF16, flat), (512, F32, flat), (256, F32, flat),
            (256, F32, flat), (H_B * HEAD_PAD, BF16, flat), (H_B * HEAD_PAD, BF16, keyed),
            (H_B * HEAD_PAD, BF16, keyed), (W_C, BF16, flat), (W_C, BF16, keyed), (2 * W_C, BF16, keyed)]
    if emit_state:
        outs += [(KV_RANK, F32, flat), (LANES, F32, flat), (W_C, F32, flat), (W_C, F32, flat)]
    return pl.pallas_call(
        functools.partial(_inproj_kernel, rope=rope, emit_state=emit_state, tiles_per_seq=tps,
                          has_cache=has_cache),
        grid=(bsz * (tps + lead),),
        in_specs=in_specs,
        out_specs=[row(w) if kind == flat else kv_rows(w) for w, _, kind in outs],
        out_shape=[jax.ShapeDtypeStruct((t, w) if kind == flat else (bsz, past + seq, w), dt)
                   for w, dt, kind in outs],
        compiler_params=pltpu.CompilerParams(dimension_semantics=("arbitrary",), vmem_limit_bytes=VMEM_LIMIT),
        name="inproj_lat" if rope else "inproj_ctx",
    )(*args)


def _gla_constants():
    c = CHUNK
    idx = np.arange(c)
    t, u = idx[:, None], idx[None, :]
    mats = [(u <= t)]
    masks = [2.0 * np.eye(c)]
    hs = c // 2
    while hs >= 1:
        blk = idx // (2 * hs)
        lower = (idx % (2 * hs)) < hs
        p = blk * 2 * hs + hs - 1
        m_low = lower[:, None] & (u > t) & (u <= p[:, None])
        m_up = (~lower)[:, None] & (u > p[:, None]) & (u <= t)
        mats.append(m_low | m_up)
        masks.append(((blk[:, None] == blk[None, :]) & (lower[:, None] != lower[None, :])).astype(np.float64))
        hs //= 2
    mats.append(u > t)
    mats = [m.astype(np.float32) for m in mats]

    def stacked(ms):
        cm = np.concatenate(ms, axis=0)
        return jnp.asarray(np.concatenate([cm, cm], axis=1), BF16)

    cms_f = stacked(mats)
    mats_b = [m[::-1, ::-1] for m in mats]
    cms_b = stacked(mats_b)
    cms2_b = stacked([mats_b[0], mats_b[-1]])
    lm = jnp.asarray(np.stack([np.tile(m, (1, H_A)) for m in masks]), F32)
    same = lambda n_row, n_col: (np.arange(H_A * n_row)[:, None] // n_row
                                 == np.arange(H_A * n_col)[None, :] // n_col).astype(np.float32)
    km = same(c, DK_A)
    vm = same(c, DV_A)
    sm = same(DV_A, DK_A)
    return (cms_f, cms_b, cms2_b, lm, jnp.asarray(km, BF16), jnp.asarray(vm, BF16), jnp.asarray(sm, BF16))


N_LEVEL = 6


def _gate_sums(cms_ref, g):
    g_hi = g.astype(BF16)
    g_lo = (g - g_hi.astype(F32)).astype(BF16)
    return _dot(cms_ref[...], jnp.concatenate([g_hi, g_lo], axis=0))


def _gla_intra(q, k, v16, sums_f, sums_b, upper, lm_ref, km_ref, vm_ref):
    def per_head_rows(x):
        return jnp.concatenate([x.astype(BF16)] * H_A, axis=0) * km_ref[...]

    att = lm_ref[0] * _dot_nt(q.astype(BF16), per_head_rows(k))
    for lv in range(1, N_LEVEL + 1):
        rows = slice(lv * CHUNK, (lv + 1) * CHUNK)
        ef, eb = jnp.exp(sums_f[rows]), jnp.exp(sums_b[rows])
        qs = jnp.where(upper[lv - 1], ef, eb)
        ks = jnp.where(upper[lv - 1], eb, ef)
        att = att + lm_ref[lv] * _dot_nt((q * qs).astype(BF16), per_head_rows(k * ks))
    v_rows = jnp.concatenate([v16] * H_A, axis=0) * vm_ref[...]
    return _dot(att.astype(BF16), v_rows)


def _gla_step(q, k, v16, b, to_end, b_total, st, sm_ref):
    qe = (q * jnp.exp(b)).astype(BF16)
    out = _dot_nt(qe, st.astype(BF16) * sm_ref[...])
    ke = (k * jnp.exp(to_end)).astype(BF16)
    upd = pl.dot(v16, ke, trans_a=True)
    return out, st * jnp.exp(b_total) + upd


def _gla_kernel(*refs, has_init, emit_state):
    it = iter(refs)
    qf_ref, kf_ref, vf_ref, gf_ref, gbf_ref, qb_ref, kb_ref, vb_ref, gb_ref = (next(it) for _ in range(9))
    if has_init:
        s0f_ref, s0b_ref = next(it), next(it)
    cmsf_ref, cmsb_ref, cms2b_ref, lm_ref, km_ref, vm_ref, sm_ref = (next(it) for _ in range(7))
    of_ref, ob_ref = next(it), next(it)
    if emit_state:
        sff_ref, sfb_ref = next(it), next(it)
    stf_ref, stb_ref = next(it), next(it)
    ti = pl.program_id(1)
    nt = pl.num_programs(1)

    @pl.when(ti == 0)
    def _():
        if has_init:
            stf_ref[...] = s0f_ref[0]
            stb_ref[...] = s0b_ref[0]
        else:
            stf_ref[...] = jnp.zeros(stf_ref.shape, F32)
            stb_ref[...] = jnp.zeros(stb_ref.shape, F32)

    row = lax.broadcasted_iota(jnp.int32, (CHUNK, H_A * DK_A), 0)
    upper = [(row & (CHUNK >> lv)) != 0 for lv in range(1, N_LEVEL + 1)]
    end = slice((N_LEVEL + 1) * CHUNK, (N_LEVEL + 2) * CHUNK)
    n_chunk = qf_ref.shape[1] // CHUNK
    stf = stf_ref[...]
    stb = stb_ref[...]
    for ci in range(n_chunk):
        rows = slice(ci * CHUNK, (ci + 1) * CHUNK)
        q, k, v16 = qf_ref[0, rows, :], kf_ref[0, rows, :], vf_ref[0, rows, :]
        sums_f = _gate_sums(cmsf_ref, gf_ref[0, rows, :])
        sums_b = _gate_sums(cmsb_ref, gbf_ref[0, rows, :])
        intra = _gla_intra(q, k, v16, sums_f, sums_b, upper, lm_ref, km_ref, vm_ref)
        b = sums_f[0:CHUNK]
        inter, stf = _gla_step(q, k, v16, b, sums_f[end], b[CHUNK - 1:CHUNK], stf, sm_ref)
        of_ref[0, rows, :] = intra + inter
        cb = n_chunk - 1 - ci
        rows = slice(cb * CHUNK, (cb + 1) * CHUNK)
        q, k, v16 = qb_ref[0, rows, :], kb_ref[0, rows, :], vb_ref[0, rows, :]
        sums = _gate_sums(cms2b_ref, gb_ref[0, rows, :])
        b = sums[0:CHUNK]
        inter, stb = _gla_step(q, k, v16, b, sums[CHUNK:2 * CHUNK], b[0:1], stb, sm_ref)
        ob_ref[0, rows, :] = inter
    stf_ref[...] = stf
    stb_ref[...] = stb

    if emit_state:
        @pl.when(ti == nt - 1)
        def _():
            for dst, st_ref in ((sff_ref, stf_ref), (sfb_ref, stb_ref)):
                blocks = st_ref[...].T
                for hd in range(H_A):
                    dst[0, hd] = blocks[hd * DK_A:(hd + 1) * DK_A, hd * DV_A:(hd + 1) * DV_A]


def _gla_scan(q3, k3, v3, gf3, gb3, init, consts, *, emit_state):
    bsz, s, _ = q3.shape
    tt = GLA_TILE
    nt = s // tt
    fmap = lambda b, t: (b, t, 0)
    rmap = lambda b, t: (b, nt - 1 - t, 0)
    bmap = lambda b, t: (b, 0, 0)
    const = lambda shape: pl.BlockSpec(shape, lambda b, t: (0,) * len(shape))
    tile = lambda w, m: pl.BlockSpec((1, tt, w), m)
    state = pl.BlockSpec((1, H_A * DV_A, H_A * DK_A), bmap)
    in_specs = [tile(256, fmap), tile(256, fmap), tile(512, fmap), tile(256, fmap), tile(256, fmap),
                tile(256, rmap), tile(256, rmap), tile(512, rmap), tile(256, rmap)]
    args = [q3, k3, v3, gf3, gb3, q3, k3, v3, gb3]
    if init is not None:
        in_specs += [state, state]
        args += list(init)
    in_specs += [const(c.shape) for c in consts]
    args += list(consts)
    out_specs = [tile(W_A, fmap), tile(W_A, rmap)]
    out_shape = [jax.ShapeDtypeStruct((bsz, s, W_A), F32)] * 2
    if emit_state:
        out_specs += [pl.BlockSpec((1, H_A, DK_A, DV_A), lambda b, t: (b, 0, 0, 0))] * 2
        out_shape += [jax.ShapeDtypeStruct((bsz, H_A, DK_A, DV_A), F32)] * 2
    return pl.pallas_call(
        functools.partial(_gla_kernel, has_init=init is not None, emit_state=emit_state),
        grid=(bsz, nt),
        in_specs=in_specs,
        out_specs=out_specs,
        out_shape=out_shape,
        scratch_shapes=[pltpu.VMEM((H_A * DV_A, H_A * DK_A), F32)] * 2,
        compiler_params=pltpu.CompilerParams(dimension_semantics=("arbitrary", "arbitrary"),
                                             vmem_limit_bytes=VMEM_LIMIT),
        name="gla_lat" if init is not None else "gla_ctx",
    )(*args)


def _state_to_blocks(s0):
    st = jnp.swapaxes(s0, -1, -2)
    eye = jnp.eye(H_A, dtype=s0.dtype)
    full = st[:, :, :, None, :] * eye[None, :, None, :, None]
    return full.reshape(s0.shape[0], H_A * DV_A, H_A * DK_A)


def _online_step(carry, q, kt, vt):
    m, acc = carry
    s = _dot_nt(q, kt)
    m_new = jnp.maximum(m, jnp.max(s, axis=-1, keepdims=True))
    p = jnp.exp2(s - m_new)
    acc = jnp.exp2(m - m_new) * acc + _dot(p.astype(BF16), vt)
    return m_new, acc


def _key_rounds(n_keys, n_rounds, align):
    units = n_keys // align
    assert units * align == n_keys
    n_rounds = 1 if n_keys <= 2 * MXU_WIDTH else min(n_rounds, units)
    sizes = [(units // n_rounds + (1 if r < units % n_rounds else 0)) * align for r in range(n_rounds)]
    starts = np.concatenate([[0], np.cumsum(sizes)[:-1]])
    return [(int(s), int(z)) for s, z in zip(starts, sizes)]


def _attend(qs, k_ref, v_ref, ksls, vsls, n_rounds, align):
    tq = qs[0].shape[0]
    n = len(qs)
    width = vsls[0].stop - vsls[0].start
    init = (jnp.full((tq, 1), NEG_BIG, F32), jnp.zeros((tq, width), F32))
    carries = [init for _ in range(n)]
    for start, size in _key_rounds(k_ref.shape[1], n_rounds, align):
        rows = slice(start, start + size)
        carries = [_online_step(carries[i], qs[i], k_ref[0, rows, ksls[i]], v_ref[0, rows, vsls[i]])
                   for i in range(n)]
    return [c[1] for c in carries]


def _heads_per_step(n_heads, n_keys, at_least):
    return n_heads if n_keys <= 2 * MXU_WIDTH else at_least


def _mla_kernel(q_ref, k_ref, v_ref, o_ref):
    n_heads = q_ref.shape[2] // HEAD_PAD
    sls = [slice(hd * HEAD_PAD, (hd + 1) * HEAD_PAD) for hd in range(n_heads)]
    accs = _attend([q_ref[0, :, sl] for sl in sls], k_ref, v_ref, sls, sls, MLA_ROUNDS, MXU_WIDTH)
    outs = [acc / acc[:, V_B:V_B + 1] for acc in accs]
    lane = _lane_iota(outs[0].shape)
    pairs = [jnp.where(lane < V_B, outs[2 * p], pltpu.roll(outs[2 * p + 1], V_B, 1)) for p in range(n_heads // 2)]
    o_ref[0] = jnp.concatenate(pairs, axis=1).astype(o_ref.dtype)


def _mla_attention(q3, k3, v3):
    bsz, s, _ = q3.shape
    keys = k3.shape[1]
    tq = min(Q_TILE, s)
    nh = _heads_per_step(H_B, keys, 2)
    pair = nh * HEAD_PAD
    return pl.pallas_call(
        _mla_kernel,
        grid=(bsz, H_B // nh, s // tq),
        in_specs=[
            pl.BlockSpec((1, tq, pair), lambda b, h, i: (b, i, h)),
            pl.BlockSpec((1, keys, pair), lambda b, h, i: (b, 0, h)),
            pl.BlockSpec((1, keys, pair), lambda b, h, i: (b, 0, h)),
        ],
        out_specs=pl.BlockSpec((1, tq, nh * V_B), lambda b, h, i: (b, i, h)),
        out_shape=jax.ShapeDtypeStruct((bsz, s, W_B), BF16),
        compiler_params=pltpu.CompilerParams(dimension_semantics=("arbitrary",) * 3,
                                             vmem_limit_bytes=VMEM_LIMIT),
        name="mla_lat" if keys > s else "mla_ctx",
    )(q3, k3, v3)


def _diff_kernel(lam_ref, q_ref, k_ref, v_ref, o_ref, *, lam_init):
    lq = lam_ref[...]
    lam = (jnp.exp(jnp.sum(lq[0:1] * lq[1:2], axis=-1, keepdims=True))
           - jnp.exp(jnp.sum(lq[2:3] * lq[3:4], axis=-1, keepdims=True)) + lam_init)
    n_heads = q_ref.shape[2] // LANES
    lane = _lane_iota((q_ref.shape[1], LANES))
    qcs, ksls, vsls = [], [], []
    for hd in range(n_heads):
        q = q_ref[0, :, hd * LANES:(hd + 1) * LANES].astype(F32)
        qcs += [jnp.where(lane < D_C, q, 0.0).astype(BF16), jnp.where(lane >= D_C, q, 0.0).astype(BF16)]
        ksls += [slice(hd * LANES, (hd + 1) * LANES)] * 2
        vsls += [slice(2 * hd * LANES, 2 * (hd + 1) * LANES)] * 2
    accs = _attend(qcs, k_ref, v_ref, ksls, vsls, DIFF_ROUNDS, LANES)
    res = [acc[:, 0:LANES] / acc[:, LANES:LANES + 1] for acc in accs]
    o_ref[0] = jnp.concatenate([res[2 * hd] - lam * res[2 * hd + 1] for hd in range(n_heads)], axis=1)


def _diff_attention(lam_qk, q3, k3, v3, lam_init):
    bsz, s, _ = q3.shape
    keys = k3.shape[1]
    tq = min(Q_TILE, s)
    nh = _heads_per_step(H_C, keys, 1)
    return pl.pallas_call(
        functools.partial(_diff_kernel, lam_init=lam_init),
        grid=(bsz, H_C // nh, s // tq),
        in_specs=[
            pl.BlockSpec((4, D_C), lambda b, h, i: (0, 0)),
            pl.BlockSpec((1, tq, nh * LANES), lambda b, h, i: (b, i, h)),
            pl.BlockSpec((1, keys, nh * LANES), lambda b, h, i: (b, 0, h)),
            pl.BlockSpec((1, keys, 2 * nh * LANES), lambda b, h, i: (b, 0, h)),
        ],
        out_specs=pl.BlockSpec((1, tq, nh * LANES), lambda b, h, i: (b, i, h)),
        out_shape=jax.ShapeDtypeStruct((bsz, s, W_C), F32),
        compiler_params=pltpu.CompilerParams(dimension_semantics=("arbitrary",) * 3,
                                             vmem_limit_bytes=VMEM_LIMIT),
        name="diff_lat" if keys > s else "diff_ctx",
    )(lam_qk, q3, k3, v3)


def _merge_kernel(x_ref, mod_ref, g1_ref, wg_ref, of_ref, ob_ref, sr_ref, yb_ref, yc_ref, ggla_ref, gsub_ref,
                  woa_ref, wob_ref, woc_ref, wout_ref, g2_ref, w1_ref, w2_ref, o_ref, *, lam_init):
    x = x_ref[...]
    m = mod_ref[0]
    h = _rms(x, g1_ref[...]) * (1.0 + m[1:2]) + m[0:1]
    gates = _sigmoid(_dot(h.astype(BF16), wg_ref[...]))
    o = of_ref[...] + ob_ref[...]
    sr = sr_ref[...]
    yc = yc_ref[...]
    ya_t, yc_t = [], []
    for hd in range(H_A):
        sl = slice(hd * LANES, (hd + 1) * LANES)
        ya_t.append(_rms(o[:, sl], ggla_ref[...]) * sr[:, sl])
        yc_t.append(_rms(yc[:, sl], gsub_ref[...]) * (1.0 - lam_init))
    ya = jnp.concatenate(ya_t, axis=1).astype(BF16)
    ycn = jnp.concatenate(yc_t, axis=1).astype(BF16)
    merged = (gates[:, 0:D_MODEL] * _dot(ya, woa_ref[...])
              + gates[:, D_MODEL:2 * D_MODEL] * _dot(yb_ref[...], wob_ref[...])
              + gates[:, 2 * D_MODEL:3 * D_MODEL] * _dot(ycn, woc_ref[...]))
    x1 = x + m[2:3] * _dot(merged.astype(BF16), wout_ref[...])

    h2 = _rms(x1, g2_ref[...]) * (1.0 + m[4:5]) + m[3:4]
    a = jnp.maximum(_dot(h2.astype(BF16), w1_ref[...]), 0.0)
    o_ref[...] = x1 + m[5:6] * _dot((a * a).astype(BF16), w2_ref[...])


def _merge_mlp(x2, mod8, group0, lw, of2, ob2, sr2, yb2, yc2, lam_init, per_seq_groups):
    t = x2.shape[0]
    tm = TOKEN_TILE
    tiles_per_seq = LAT_SEQ // tm
    const = lambda shape: pl.BlockSpec(shape, lambda i: (0,) * len(shape), pipeline_mode=pl.Buffered(1))
    row = lambda w: pl.BlockSpec((tm, w), lambda i: (i, 0))
    if per_seq_groups:
        mod_spec = pl.BlockSpec((1, 8, D_MODEL), lambda i: (group0 + i // tiles_per_seq, 0, 0))
    else:
        mod_spec = pl.BlockSpec((1, 8, D_MODEL), lambda i: (group0, 0, 0))
    return pl.pallas_call(
        functools.partial(_merge_kernel, lam_init=lam_init),
        grid=(t // tm,),
        in_specs=[row(D_MODEL), mod_spec, const((1, D_MODEL)), const((D_MODEL, GATE_COLS)),
                  row(W_A), row(W_A), row(W_A), row(W_B), row(W_C), const((1, LANES)), const((1, LANES)),
                  const((W_A, D_MODEL)), const((W_B, D_MODEL)), const((W_C, D_MODEL)),
                  const((D_MODEL, D_MODEL)), const((1, D_MODEL)), const((D_MODEL, D_FF)), const((D_FF, D_MODEL))],
        out_specs=row(D_MODEL),
        out_shape=jax.ShapeDtypeStruct((t, D_MODEL), F32),
        compiler_params=pltpu.CompilerParams(dimension_semantics=("arbitrary",), vmem_limit_bytes=VMEM_LIMIT),
        name="merge_mlp",
    )(x2, mod8, lw["g1"], lw["w_gates"], of2, ob2, sr2, yb2, yc2, lw["ggla"], lw["gsub"],
      lw["woa"], lw["wob"], lw["woc"], lw["wout"], lw["g2"], lw["w1"], lw["w2"])


def _pad_heads(w, real, pad):
    k = w.shape[0]
    w = w.reshape(k, -1, real)
    return jnp.pad(w, ((0, 0), (0, 0), (0, pad - real))).reshape(k, -1)


def _layer_weights(l, p):
    widths = (H_A * DK_A, H_A * DK_A, W_A, W_A, 2 * GATE_RANK, Q_RANK, KV_RANK, ROPE_B, W_C, W_C, W_C, GATE_COLS)
    offs = np.concatenate([[0], np.cumsum(widths)])
    w_in = p["w_in"][l]
    aq, ak, av, ar, aa, qd, kvd, kr, dq, dk, dv, gates = (w_in[:, offs[i]:offs[i + 1]] for i in range(12))
    z = lambda n: jnp.zeros((D_MODEL, n), F32)
    misc = jnp.concatenate([aa, kr, z(LANES - 2 * GATE_RANK - ROPE_B)], axis=1)
    krp = jnp.concatenate([z(NOPE_B), kr, z(LANES - QK_B)], axis=1)
    part = np.arange(ROPE_B) ^ (ROPE_B // 4)
    krq = jnp.concatenate([z(NOPE_B), kr[:, part], z(LANES - QK_B)], axis=1)
    w_in_a = jnp.concatenate([aq, ak, av, ar, qd, kvd, dq, dk, dv, misc, krp, krq], axis=1).astype(BF16)
    uq = p["w_mla_uq"][l].reshape(Q_RANK, H_B, QK_B)
    uq_part = jnp.concatenate([jnp.zeros((Q_RANK, H_B, NOPE_B), F32), uq[:, :, NOPE_B + part]], axis=2)
    gain_part = lambda g: jnp.pad(g[NOPE_B + part], (NOPE_B, LANES - QK_B)).reshape(1, LANES)
    wa2 = jnp.zeros((LANES, 512), F32)
    wa2 = wa2.at[0:GATE_RANK, 0:256].set(p["w_gla_a2"][l, 0])
    wa2 = wa2.at[GATE_RANK:2 * GATE_RANK, 256:512].set(p["w_gla_a2"][l, 1])
    pad_gain = lambda g: jnp.pad(g, (0, HEAD_PAD - QK_B)).reshape(1, HEAD_PAD)
    twice = lambda g: jnp.concatenate([g, g]).reshape(1, LANES)
    return dict(
        g1=p["g_norm1"][l].reshape(1, D_MODEL), g2=p["g_norm2"][l].reshape(1, D_MODEL),
        w_in_a=w_in_a, w_gates=gates.astype(BF16),
        wa2=wa2.astype(BF16), ba=p["b_gla_a"][l].reshape(1, 512),
        gqa=p["g_mla_qa"][l].reshape(1, Q_RANK), gkva=p["g_mla_kva"][l].reshape(1, KV_RANK),
        wuq=_pad_heads(p["w_mla_uq"][l], QK_B, HEAD_PAD).astype(BF16),
        wuk=_pad_heads(p["w_mla_uk"][l], NOPE_B, HEAD_PAD).astype(BF16),
        wuv=_pad_heads(p["w_mla_uv"][l], V_B, HEAD_PAD).astype(BF16),
        gq=pad_gain(p["g_mla_q"][l]), gk=pad_gain(p["g_mla_k"][l]),
        wuq_part=_pad_heads(uq_part.reshape(Q_RANK, H_B * QK_B), QK_B, HEAD_PAD).astype(BF16),
        gq_part=gain_part(p["g_mla_q"][l]), gk_part=gain_part(p["g_mla_k"][l]),
        gdq=twice(p["g_diff_q"][l]), gdk=twice(p["g_diff_k"][l]),
        ggla=p["g_gla_out"][l].reshape(1, DV_A), gsub=p["g_diff_sub"][l].reshape(1, 2 * D_C),
        woa=p["w_o_gla"][l].astype(BF16), wob=p["w_o_mla"][l].astype(BF16), woc=p["w_o_diff"][l].astype(BF16),
        wout=p["w_out"][l].astype(BF16), w1=p["w_mlp1"][l].astype(BF16), w2=p["w_mlp2"][l].astype(BF16),
        lam_qk=p["lam_qk"][l],
    )


def _layer(x2, bsz, seq, mod8, group0, lw, lam_init, consts, tabs, cached):
    is_latent = cached is not None
    cache = None
    if is_latent:
        s0, ckv_c, krope_c, dk_c, dv_c = cached
        past = ckv_c.shape[1]
        krp_c = jnp.pad(krope_c, ((0, 0), (0, 0), (NOPE_B, LANES - QK_B)))
        cache = (ckv_c, krp_c, dk_c.reshape(bsz, past, W_C), dv_c.reshape(bsz, past, W_C))
    outs = _inproj(x2, bsz, mod8, group0, lw, tabs, cache, rope=is_latent, emit_state=not is_latent)
    gq, gk, gv, sr, gf, gb, mq, mk, mv, dq, dk, dv = outs[:12]
    r3 = lambda a: a.reshape(bsz, seq, a.shape[-1])

    init = (_state_to_blocks(s0[:, 0]), _state_to_blocks(s0[:, 1])) if is_latent else None
    gla = _gla_scan(r3(gq), r3(gk), r3(gv), r3(gf), r3(gb), init, consts, emit_state=not is_latent)
    of, ob = gla[0], gla[1]

    yb = _mla_attention(r3(mq), mk, mv)
    yc = _diff_attention(lw["lam_qk"], r3(dq), dk, dv, lam_init)

    x_out = _merge_mlp(x2, mod8, group0, lw, of.reshape(-1, W_A), ob.reshape(-1, W_A), sr, yb.reshape(-1, W_B),
                       yc.reshape(-1, W_C), lam_init, is_latent)

    new_state = None
    if not is_latent:
        ckv, misc, dk32, dv32 = outs[12:]
        gla_state = jnp.stack([gla[2], gla[3]], axis=1)
        new_state = (gla_state, ckv.reshape(bsz, seq, KV_RANK),
                     misc[:, 2 * GATE_RANK:2 * GATE_RANK + ROPE_B].reshape(bsz, seq, ROPE_B),
                     dk32.reshape(bsz, seq, H_C, 2, D_C), dv32.reshape(bsz, seq, H_C, 2 * D_C))
    return x_out, new_state


def kernel(x_prompt, x_sample, state_gla, cache_mla_ckv, cache_mla_krope, cache_diff_k, cache_diff_v, c, c_ctx, w_mod, b_mod, g_norm1, g_norm2, w_in, w_gla_a2, b_gla_a, g_gla_out, g_mla_qa, g_mla_kva, w_mla_uq, w_mla_uk, w_mla_uv, g_mla_q, g_mla_k, g_diff_q, g_diff_k, lam_qk, g_diff_sub, w_o_gla, w_o_mla, w_o_diff, w_out, w_mlp1, w_mlp2):
    params = dict(w_in=w_in, g_norm1=g_norm1, g_norm2=g_norm2, w_gla_a2=w_gla_a2, b_gla_a=b_gla_a,
                  g_gla_out=g_gla_out, g_mla_qa=g_mla_qa, g_mla_kva=g_mla_kva, w_mla_uq=w_mla_uq,
                  w_mla_uk=w_mla_uk, w_mla_uv=w_mla_uv, g_mla_q=g_mla_q, g_mla_k=g_mla_k, g_diff_q=g_diff_q,
                  g_diff_k=g_diff_k, lam_qk=lam_qk, g_diff_sub=g_diff_sub, w_o_gla=w_o_gla, w_o_mla=w_o_mla,
                  w_o_diff=w_o_diff, w_out=w_out, w_mlp1=w_mlp1, w_mlp2=w_mlp2)
    nb, ns, _ = x_prompt.shape
    db, ds, _ = x_sample.shape
    assert ds == LAT_SEQ and db + 1 <= 8

    cond8 = jnp.concatenate([c_ctx[None, :], c, jnp.zeros((8 - 1 - db, D_MODEL), F32)], axis=0)
    mod = _modulation(cond8, w_mod, b_mod).reshape(DEPTH, 8, 6, D_MODEL)
    mod = jnp.pad(mod, ((0, 0), (0, 0), (0, 2), (0, 0)))

    consts = _gla_constants()
    tabs = (_rope_tables(ROPE_B, NOPE_B, LANES), _rope_tables(D_C, 0, D_C))

    yp = x_prompt.reshape(nb * ns, D_MODEL)
    ys = x_sample.reshape(db * ds, D_MODEL)
    states = []
    for l in range(DEPTH):
        lw = _layer_weights(l, params)
        lam_init = 0.8 - 0.6 * math.exp(-0.3 * l)
        yp, st = _layer(yp, nb, ns, mod[l], 0, lw, lam_init, consts, tabs, None)
        states.append(st)
        cached = (state_gla[:, l], cache_mla_ckv[:, l], cache_mla_krope[:, l], cache_diff_k[:, l],
                  cache_diff_v[:, l])
        ys, _ = _layer(ys, db, ds, mod[l], 1, lw, lam_init, consts, tabs, cached)
    stack = lambda i: jnp.stack([s[i] for s in states], axis=1)
    return (yp.reshape(nb, ns, D_MODEL), ys.reshape(db, ds, D_MODEL), stack(0), stack(1), stack(2), stack(3),
            stack(4))
```

```python
import functools
import math

import numpy as np
import jax
import jax.numpy as jnp
from jax import lax
from jax.experimental import pallas as pl
from jax.experimental.pallas import tpu as pltpu

F32 = jnp.float32
BF16 = jnp.bfloat16

D_MODEL = 1024
DEPTH = 2
GRID_W = 64
ROPE_BASE = 10000.0
EPS = 1e-6
H_A, DK_A, DV_A = 4, 64, 128
GATE_RANK = 16
GATE_TAU = 16.0
CHUNK = 64
W_A = H_A * DV_A
H_B, Q_RANK, KV_RANK = 8, 384, 256
NOPE_B, ROPE_B, V_B = 64, 32, 64
QK_B = NOPE_B + ROPE_B
W_B = H_B * V_B
H_C, D_C = 4, 64
W_C = H_C * 2 * D_C
D_FF = 4 * D_MODEL
GATE_COLS = 3 * D_MODEL
LAT_SEQ = 4096

LANES = 128
HEAD_PAD = LANES
MXU_WIDTH = 256
VMEM_LIMIT = 56 * 1024 * 1024

TOKEN_TILE = 256
MERGE_TILE = 512
MOD_TILE = 1536
Q_TILE = 1024
MLA_ROUNDS = 2
DIFF_ROUNDS = 2
GLA_TILE = 256

C_AQ, C_AK, C_AV, C_AR = 0, 256, 512, 1024
C_QD, C_KVD = 1536, 1920
C_DQ, C_DK, C_DV = 2176, 2688, 3200
C_MISC, C_KRP, C_KRQ = 3712, 3840, 3968
IN_COLS_A = 4096

NEG_BIG = -1e30
LOG2E = math.log2(math.e)


def _dot(a, b):
    return jnp.dot(a, b, preferred_element_type=F32)


def _dot_nt(a, b):
    return lax.dot_general(a, b, (((1,), (1,)), ((), ())), preferred_element_type=F32)


def _sigmoid(x):
    return 1.0 / (1.0 + jnp.exp(-x))


def _rms(x, g):
    ms = jnp.mean(x * x, axis=-1, keepdims=True)
    return x * lax.rsqrt(ms + EPS) * g


def _lane_iota(shape):
    return lax.broadcasted_iota(jnp.int32, shape, len(shape) - 1)


def _ones_lane0(shape):
    return jnp.where(_lane_iota(shape) % LANES == 0, 1.0, 0.0).astype(F32)


def _mod_kernel(c_ref, w_ref, b_ref, o_ref):
    c = c_ref[...]
    s = c * _sigmoid(c)
    o_ref[0] = _dot(s.astype(BF16), w_ref[0].astype(BF16)) + b_ref[0]


def _modulation(cond8, w_mod, b_mod):
    nt = MOD_TILE
    return pl.pallas_call(
        _mod_kernel,
        grid=(DEPTH, 6 * D_MODEL // nt),
        in_specs=[
            pl.BlockSpec((8, D_MODEL), lambda l, j: (0, 0)),
            pl.BlockSpec((1, D_MODEL, nt), lambda l, j: (l, 0, j)),
            pl.BlockSpec((1, 1, nt), lambda l, j: (l, 0, j)),
        ],
        out_specs=pl.BlockSpec((1, 8, nt), lambda l, j: (l, 0, j)),
        out_shape=jax.ShapeDtypeStruct((DEPTH, 8, 6 * D_MODEL), F32),
        compiler_params=pltpu.CompilerParams(vmem_limit_bytes=VMEM_LIMIT),
        name="modulation",
    )(cond8, w_mod, b_mod.reshape(DEPTH, 1, 6 * D_MODEL))


def _rope_tables(n_rope, lane0, width):
    half = n_rope // 2
    nf = half // 2
    inv = ROPE_BASE ** (-np.arange(nf, dtype=np.float64) / nf)
    pos = np.arange(GRID_W, dtype=np.float64)[:, None]
    cr = np.zeros((GRID_W, LANES)); cc = np.ones((GRID_W, LANES))
    sr = np.zeros((GRID_W, LANES)); sc = np.zeros((GRID_W, LANES))
    for base in range(0, LANES, width):
        for d in range(n_rope):
            lane = base + lane0 + d
            within = d % half
            ang = pos[:, 0] * inv[within % nf]
            sign = -1.0 if within < nf else 1.0
            if d < half:
                cr[:, lane] = np.cos(ang); cc[:, lane] = 0.0
                sr[:, lane] = sign * np.sin(ang)
            else:
                cc[:, lane] = np.cos(ang)
                sc[:, lane] = sign * np.sin(ang)
    return tuple(jnp.asarray(t, F32) for t in (cr, cc, sr, sc))


def _tile_tables(tabs, row0, n_sub):
    cr_ref, cc_ref, sr_ref, sc_ref = tabs
    cs, ss = [], []
    for j in range(n_sub):
        r = (row0 + j) % GRID_W
        cs.append(cr_ref[pl.ds(r, 1), :] + cc_ref[...])
        ss.append(sr_ref[pl.ds(r, 1), :] + sc_ref[...])
    return jnp.concatenate(cs, axis=0), jnp.concatenate(ss, axis=0)


def _rotate(x, cos_t, sin_t, nf):
    lane = _lane_iota(x.shape)
    up = pltpu.roll(x, LANES - nf, 1)
    dn = pltpu.roll(x, nf, 1)
    partner = jnp.where((lane & nf) == 0, up, dn)
    return x * cos_t + partner * sin_t


def _inproj_kernel(*refs, rope, emit_state, tiles_per_seq, has_cache):
    it = iter(refs)
    if has_cache:
        ckvc_ref, krpc_ref, dkc_ref, dvc_ref = (next(it) for _ in range(4))
    x_ref, mod_ref, g1_ref, w_ref, wa2_ref, ba_ref = (next(it) for _ in range(6))
    gqa_ref, gkva_ref, wuq_ref, wuk_ref, wuv_ref, gq_ref, gk_ref = (next(it) for _ in range(7))
    gdq_ref, gdk_ref = next(it), next(it)
    if rope:
        tab_m = tuple(next(it) for _ in range(4))
        tab_d = tuple(next(it) for _ in range(4))
        wuqp_ref, gqp_ref, gkp_ref = (next(it) for _ in range(3))
    o_gq, o_gk, o_gv, o_sr, o_gf, o_gb = (next(it) for _ in range(6))
    o_mq, o_mk, o_mv, o_dq, o_dk, o_dv = (next(it) for _ in range(6))
    if emit_state:
        o_ckv, o_misc, o_dk32, o_dv32 = (next(it) for _ in range(4))

    def mla_keys(kn, krp, dst):
        for hd in range(H_B):
            sl = slice(hd * HEAD_PAD, (hd + 1) * HEAD_PAD)
            kt = kn[:, sl] + krp
            kt = kt * lax.rsqrt(jnp.sum(kt * kt, axis=-1, keepdims=True) * (1.0 / QK_B) + EPS) * gk_ref[...]
            dst[0, :, sl] = kt.astype(BF16)

    def with_ones(v):
        one_tile = _ones_lane0((v.shape[0], LANES))
        return jnp.concatenate(
            [t for hd in range(H_C) for t in (v[:, hd * LANES:(hd + 1) * LANES], one_tile)], axis=1).astype(BF16)

    def cache_rows():
        c = ckvc_ref[0].astype(BF16)
        vb = _dot(c, wuv_ref[...])
        o_mv[0] = jnp.where(_lane_iota(vb.shape) % HEAD_PAD == V_B, 1.0, vb).astype(BF16)
        mla_keys(_dot(c, wuk_ref[...]), krpc_ref[0], o_mk)
        o_dk[0] = dkc_ref[0].astype(BF16)
        o_dv[0] = with_ones(dvc_ref[0])

    def token_rows(seq_tile):
        tm = x_ref.shape[0]
        x = x_ref[...]
        m = mod_ref[0]
        h = _rms(x, g1_ref[...]) * (1.0 + m[1:2]) + m[0:1]
        y = _dot(h.astype(BF16), w_ref[...])

        o_gq[...] = y[:, C_AQ:C_AQ + 256] * (DK_A ** -0.5)
        o_gk[...] = y[:, C_AK:C_AK + 256]
        o_gv[...] = y[:, C_AV:C_AV + 512].astype(BF16)
        r = y[:, C_AR:C_AR + 512]
        o_sr[...] = r * _sigmoid(r)
        misc = y[:, C_MISC:C_MISC + LANES]
        logits = _dot(misc.astype(BF16), wa2_ref[...]) + ba_ref[...]
        logsig = jnp.minimum(logits, 0.0) - jnp.log(1.0 + jnp.exp(-jnp.abs(logits)))
        gate = logsig * (1.0 / GATE_TAU)
        o_gf[...] = gate[:, 0:256]
        o_gb[...] = gate[:, 256:512]

        if rope:
            row0 = seq_tile * (tm // GRID_W)
            cos_m, sin_m = _tile_tables(tab_m, row0, tm // GRID_W)
            cos_d, sin_d = _tile_tables(tab_d, row0, tm // GRID_W)

        qn = _rms(y[:, C_QD:C_QD + Q_RANK], gqa_ref[...])
        q8 = _dot(qn.astype(BF16), wuq_ref[...])
        ckv = _rms(y[:, C_KVD:C_KVD + KV_RANK], gkva_ref[...])
        kn = _dot(ckv.astype(BF16), wuk_ref[...])
        vb = _dot(ckv.astype(BF16), wuv_ref[...])
        o_mv[0] = jnp.where(_lane_iota(vb.shape) % HEAD_PAD == V_B, 1.0, vb).astype(BF16)
        krp = y[:, C_KRP:C_KRP + LANES]
        if rope:
            q8p = _dot(qn.astype(BF16), wuqp_ref[...])
            gq_sin = gqp_ref[...] * sin_m
            k_part = y[:, C_KRQ:C_KRQ + LANES] * (gkp_ref[...] * sin_m)
        for hd in range(H_B):
            sl = slice(hd * HEAD_PAD, (hd + 1) * HEAD_PAD)
            qt = q8[:, sl]
            rq = lax.rsqrt(jnp.sum(qt * qt, axis=-1, keepdims=True) * (1.0 / QK_B) + EPS)
            qt = qt * rq * gq_ref[...]
            kt = kn[:, sl] + krp
            rk = lax.rsqrt(jnp.sum(kt * kt, axis=-1, keepdims=True) * (1.0 / QK_B) + EPS)
            kt = kt * rk * gk_ref[...]
            if rope:
                qt = qt * cos_m + (q8p[:, sl] * gq_sin) * rq
                kt = kt * cos_m + k_part * rk
            o_mq[:, sl] = (qt * (QK_B ** -0.5 * LOG2E)).astype(BF16)
            o_mk[0, :, sl] = kt.astype(BF16)

        lane = _lane_iota((tm, LANES))
        lo = lane < D_C
        for hd in range(H_C):
            sl = slice(hd * LANES, (hd + 1) * LANES)
            for src, g_ref, dst, dst32, scale in (
                (C_DQ, gdq_ref, o_dq, None, D_C ** -0.5 * LOG2E),
                (C_DK, gdk_ref, o_dk.at[0], o_dk32 if emit_state else None, 1.0),
            ):
                t = y[:, src + hd * LANES:src + (hd + 1) * LANES]
                sq = t * t
                s_lo = jnp.sum(jnp.where(lo, sq, 0.0), axis=-1, keepdims=True)
                s_hi = jnp.sum(jnp.where(lo, 0.0, sq), axis=-1, keepdims=True)
                inv = jnp.where(lo, lax.rsqrt(s_lo * (1.0 / D_C) + EPS), lax.rsqrt(s_hi * (1.0 / D_C) + EPS))
                t = t * inv * g_ref[...]
                if rope:
                    t = _rotate(t, cos_d, sin_d, D_C // 4)
                if dst32 is not None:
                    dst32[:, sl] = t
                dst[:, sl] = (t * scale).astype(BF16)
        dv = y[:, C_DV:C_DV + W_C]
        o_dv[0] = with_ones(dv)
        if emit_state:
            o_dv32[...] = dv
            o_ckv[...] = ckv
            o_misc[...] = misc

    if not has_cache:
        token_rows(pl.program_id(0) % tiles_per_seq)
        return
    step = pl.program_id(0) % (tiles_per_seq + 1)
    pl.when(step == 0)(cache_rows)
    pl.when(step > 0)(lambda: token_rows(step - 1))


def _inproj(x2, bsz, mod8, group0, lw, tabs, cache, *, rope, emit_state):
    t = x2.shape[0]
    tm = TOKEN_TILE
    seq = t // bsz
    tps = seq // tm
    has_cache = cache is not None
    past = cache[0].shape[1] if has_cache else 0
    assert past in (0, tm)
    lead = 1 if has_cache else 0
    req = lambda i: i // (tps + lead)
    tile = lambda i: jnp.maximum(i % (tps + lead) - lead, 0)
    kv_rows = lambda w: pl.BlockSpec((1, tm, w), lambda i: (req(i), i % (tps + lead), 0))
    const = lambda shape: pl.BlockSpec(shape, lambda i: (0,) * len(shape))
    row = lambda w: pl.BlockSpec((tm, w), lambda i: (req(i) * tps + tile(i), 0))
    if rope:
        mod_spec = pl.BlockSpec((1, 8, D_MODEL), lambda i: (group0 + req(i), 0, 0))
    else:
        mod_spec = pl.BlockSpec((1, 8, D_MODEL), lambda i: (group0, 0, 0))
    cache_specs = [pl.BlockSpec((1, past, c.shape[-1]), lambda i: (req(i), 0, 0)) for c in cache] if has_cache else []
    in_specs = cache_specs + [
        row(D_MODEL), mod_spec, const((1, D_MODEL)), const((D_MODEL, IN_COLS_A)),
        const((LANES, 512)), const((1, 512)),
        const((1, Q_RANK)), const((1, KV_RANK)), const((Q_RANK, H_B * HEAD_PAD)),
        const((KV_RANK, H_B * HEAD_PAD)), const((KV_RANK, H_B * HEAD_PAD)), const((1, LANES)), const((1, LANES)),
        const((1, LANES)), const((1, LANES)),
    ]
    args = (list(cache) if has_cache else []) + [
        x2, mod8, lw["g1"], lw["w_in_a"], lw["wa2"], lw["ba"], lw["gqa"], lw["gkva"], lw["wuq"],
        lw["wuk"], lw["wuv"], lw["gq"], lw["gk"], lw["gdq"], lw["gdk"]]
    if rope:
        in_specs += [const((GRID_W, LANES))] * 8 + [const((Q_RANK, H_B * HEAD_PAD)), const((1, LANES)), const((1, LANES))]
        args += list(tabs[0]) + list(tabs[1]) + [lw["wuq_part"], lw["gq_part"], lw["gk_part"]]
    flat, keyed = "flat", "keyed"
    outs = [(256, F32, flat), (256, F32, flat), (512, BF16, flat), (512, F32, flat), (256, F32, flat),
            (256, F32, flat), (H_B * HEAD_PAD, BF16, flat), (H_B * HEAD_PAD, BF16, keyed),
            (H_B * HEAD_PAD, BF16, keyed), (W_C, BF16, flat), (W_C, BF16, keyed), (2 * W_C, BF16, keyed)]
    if emit_state:
        outs += [(KV_RANK, F32, flat), (LANES, F32, flat), (W_C, F32, flat), (W_C, F32, flat)]
    return pl.pallas_call(
        functools.partial(_inproj_kernel, rope=rope, emit_state=emit_state, tiles_per_seq=tps,
                          has_cache=has_cache),
        grid=(bsz * (tps + lead),),
        in_specs=in_specs,
        out_specs=[row(w) if kind == flat else kv_rows(w) for w, _, kind in outs],
        out_shape=[jax.ShapeDtypeStruct((t, w) if kind == flat else (bsz, past + seq, w), dt)
                   for w, dt, kind in outs],
        compiler_params=pltpu.CompilerParams(dimension_semantics=("arbitrary",), vmem_limit_bytes=VMEM_LIMIT),
        name="inproj_lat" if rope else "inproj_ctx",
    )(*args)


def _gla_constants():
    c = CHUNK
    idx = np.arange(c)
    t, u = idx[:, None], idx[None, :]
    mats = [(u <= t)]
    masks = [2.0 * np.eye(c)]
    hs = c // 2
    while hs >= 1:
        blk = idx // (2 * hs)
        lower = (idx % (2 * hs)) < hs
        p = blk * 2 * hs + hs - 1
        m_low = lower[:, None] & (u > t) & (u <= p[:, None])
        m_up = (~lower)[:, None] & (u > p[:, None]) & (u <= t)
        mats.append(m_low | m_up)
        masks.append(((blk[:, None] == blk[None, :]) & (lower[:, None] != lower[None, :])).astype(np.float64))
        hs //= 2
    mats.append(u > t)
    mats = [m.astype(np.float32) for m in mats]

    def stacked(ms):
        cm = np.concatenate(ms, axis=0)
        return jnp.asarray(np.concatenate([cm, cm], axis=1), BF16)

    cms_f = stacked(mats)
    mats_b = [m[::-1, ::-1] for m in mats]
    cms_b = stacked(mats_b)
    cms2_b = stacked([mats_b[0], mats_b[-1]])
    lm = jnp.asarray(np.stack([np.tile(m, (1, H_A)) for m in masks]), F32)
    same = lambda n_row, n_col: (np.arange(H_A * n_row)[:, None] // n_row
                                 == np.arange(H_A * n_col)[None, :] // n_col).astype(np.float32)
    km = same(c, DK_A)
    vm = same(c, DV_A)
    sm = same(DV_A, DK_A)
    return (cms_f, cms_b, cms2_b, lm, jnp.asarray(km, BF16), jnp.asarray(vm, BF16), jnp.asarray(sm, BF16))


N_LEVEL = 6


def _gate_sums(cms_ref, g):
    g_hi = g.astype(BF16)
    g_lo = (g - g_hi.astype(F32)).astype(BF16)
    return _dot(cms_ref[...], jnp.concatenate([g_hi, g_lo], axis=0))


def _gla_intra(q, k, v16, sums_f, sums_b, upper, lm_ref, km_ref, vm_ref):
    def per_head_rows(x):
        return jnp.concatenate([x.astype(BF16)] * H_A, axis=0) * km_ref[...]

    att = lm_ref[0] * _dot_nt(q.astype(BF16), per_head_rows(k))
    for lv in range(1, N_LEVEL + 1):
        rows = slice(lv * CHUNK, (lv + 1) * CHUNK)
        ef, eb = jnp.exp(sums_f[rows]), jnp.exp(sums_b[rows])
        qs = jnp.where(upper[lv - 1], ef, eb)
        ks = jnp.where(upper[lv - 1], eb, ef)
        att = att + lm_ref[lv] * _dot_nt((q * qs).astype(BF16), per_head_rows(k * ks))
    v_rows = jnp.concatenate([v16] * H_A, axis=0) * vm_ref[...]
    return _dot(att.astype(BF16), v_rows)


def _gla_step(q, k, v16, b, to_end, b_total, st, sm_ref):
    qe = (q * jnp.exp(b)).astype(BF16)
    out = _dot_nt(qe, st.astype(BF16) * sm_ref[...])
    ke = (k * jnp.exp(to_end)).astype(BF16)
    upd = pl.dot(v16, ke, trans_a=True)
    return out, st * jnp.exp(b_total) + upd


def _gla_kernel(*refs, has_init, emit_state):
    it = iter(refs)
    qf_ref, kf_ref, vf_ref, gf_ref, gbf_ref, qb_ref, kb_ref, vb_ref, gb_ref = (next(it) for _ in range(9))
    if has_init:
        s0f_ref, s0b_ref = next(it), next(it)
    cmsf_ref, cmsb_ref, cms2b_ref, lm_ref, km_ref, vm_ref, sm_ref = (next(it) for _ in range(7))
    of_ref, ob_ref = next(it), next(it)
    if emit_state:
        sff_ref, sfb_ref = next(it), next(it)
    stf_ref, stb_ref = next(it), next(it)
    ti = pl.program_id(1)
    nt = pl.num_programs(1)

    @pl.when(ti == 0)
    def _():
        if has_init:
            stf_ref[...] = s0f_ref[0]
            stb_ref[...] = s0b_ref[0]
        else:
            stf_ref[...] = jnp.zeros(stf_ref.shape, F32)
            stb_ref[...] = jnp.zeros(stb_ref.shape, F32)

    row = lax.broadcasted_iota(jnp.int32, (CHUNK, H_A * DK_A), 0)
    upper = [(row & (CHUNK >> lv)) != 0 for lv in range(1, N_LEVEL + 1)]
    end = slice((N_LEVEL + 1) * CHUNK, (N_LEVEL + 2) * CHUNK)
    n_chunk = qf_ref.shape[1] // CHUNK
    stf = stf_ref[...]
    stb = stb_ref[...]
    for ci in range(n_chunk):
        rows = slice(ci * CHUNK, (ci + 1) * CHUNK)
        q, k, v16 = qf_ref[0, rows, :], kf_ref[0, rows, :], vf_ref[0, rows, :]
        sums_f = _gate_sums(cmsf_ref, gf_ref[0, rows, :])
        sums_b = _gate_sums(cmsb_ref, gbf_ref[0, rows, :])
        intra = _gla_intra(q, k, v16, sums_f, sums_b, upper, lm_ref, km_ref, vm_ref)
        b = sums_f[0:CHUNK]
        inter, stf = _gla_step(q, k, v16, b, sums_f[end], b[CHUNK - 1:CHUNK], stf, sm_ref)
        of_ref[0, rows, :] = intra + inter
        cb = n_chunk - 1 - ci
        rows = slice(cb * CHUNK, (cb + 1) * CHUNK)
        q, k, v16 = qb_ref[0, rows, :], kb_ref[0, rows, :], vb_ref[0, rows, :]
        sums = _gate_sums(cms2b_ref, gb_ref[0, rows, :])
        b = sums[0:CHUNK]
        inter, stb = _gla_step(q, k, v16, b, sums[CHUNK:2 * CHUNK], b[0:1], stb, sm_ref)
        ob_ref[0, rows, :] = inter
    stf_ref[...] = stf
    stb_ref[...] = stb

    if emit_state:
        @pl.when(ti == nt - 1)
        def _():
            for dst, st_ref in ((sff_ref, stf_ref), (sfb_ref, stb_ref)):
                blocks = st_ref[...].T
                for hd in range(H_A):
                    dst[0, hd] = blocks[hd * DK_A:(hd + 1) * DK_A, hd * DV_A:(hd + 1) * DV_A]


def _gla_scan(q3, k3, v3, gf3, gb3, init, consts, *, emit_state):
    bsz, s, _ = q3.shape
    tt = GLA_TILE
    nt = s // tt
    fmap = lambda b, t: (b, t, 0)
    rmap = lambda b, t: (b, nt - 1 - t, 0)
    bmap = lambda b, t: (b, 0, 0)
    const = lambda shape: pl.BlockSpec(shape, lambda b, t: (0,) * len(shape))
    tile = lambda w, m: pl.BlockSpec((1, tt, w), m)
    state = pl.BlockSpec((1, H_A * DV_A, H_A * DK_A), bmap)
    in_specs = [tile(256, fmap), tile(256, fmap), tile(512, fmap), tile(256, fmap), tile(256, fmap),
                tile(256, rmap), tile(256, rmap), tile(512, rmap), tile(256, rmap)]
    args = [q3, k3, v3, gf3, gb3, q3, k3, v3, gb3]
    if init is not None:
        in_specs += [state, state]
        args += list(init)
    in_specs += [const(c.shape) for c in consts]
    args += list(consts)
    out_specs = [tile(W_A, fmap), tile(W_A, rmap)]
    out_shape = [jax.ShapeDtypeStruct((bsz, s, W_A), F32)] * 2
    if emit_state:
        out_specs += [pl.BlockSpec((1, H_A, DK_A, DV_A), lambda b, t: (b, 0, 0, 0))] * 2
        out_shape += [jax.ShapeDtypeStruct((bsz, H_A, DK_A, DV_A), F32)] * 2
    return pl.pallas_call(
        functools.partial(_gla_kernel, has_init=init is not None, emit_state=emit_state),
        grid=(bsz, nt),
        in_specs=in_specs,
        out_specs=out_specs,
        out_shape=out_shape,
        scratch_shapes=[pltpu.VMEM((H_A * DV_A, H_A * DK_A), F32)] * 2,
        compiler_params=pltpu.CompilerParams(dimension_semantics=("arbitrary", "arbitrary"),
                                             vmem_limit_bytes=VMEM_LIMIT),
        name="gla_lat" if init is not None else "gla_ctx",
    )(*args)


def _state_to_blocks(s0):
    st = jnp.swapaxes(s0, -1, -2)
    eye = jnp.eye(H_A, dtype=s0.dtype)
    full = st[:, :, :, None, :] * eye[None, :, None, :, None]
    return full.reshape(s0.shape[0], H_A * DV_A, H_A * DK_A)


def _online_step(carry, q, kt, vt):
    m, acc = carry
    s = _dot_nt(q, kt)
    m_new = jnp.maximum(m, jnp.max(s, axis=-1, keepdims=True))
    p = jnp.exp2(s - m_new)
    acc = jnp.exp2(m - m_new) * acc + _dot(p.astype(BF16), vt)
    return m_new, acc


def _key_rounds(n_keys, n_rounds, align):
    units = n_keys // align
    assert units * align == n_keys
    n_rounds = 1 if n_keys <= 2 * MXU_WIDTH else min(n_rounds, units)
    sizes = [(units // n_rounds + (1 if r < units % n_rounds else 0)) * align for r in range(n_rounds)]
    starts = np.concatenate([[0], np.cumsum(sizes)[:-1]])
    return [(int(s), int(z)) for s, z in zip(starts, sizes)]


def _attend(qs, k_ref, v_ref, ksls, vsls, n_rounds, align):
    tq = qs[0].shape[0]
    n = len(qs)
    width = vsls[0].stop - vsls[0].start
    init = (jnp.full((tq, 1), NEG_BIG, F32), jnp.zeros((tq, width), F32))
    carries = [init for _ in range(n)]
    for start, size in _key_rounds(k_ref.shape[1], n_rounds, align):
        rows = slice(start, start + size)
        carries = [_online_step(carries[i], qs[i], k_ref[0, rows, ksls[i]], v_ref[0, rows, vsls[i]])
                   for i in range(n)]
    return [c[1] for c in carries]


def _heads_per_step(n_heads, n_keys, at_least):
    return n_heads if n_keys <= 2 * MXU_WIDTH else at_least


def _mla_kernel(q_ref, k_ref, v_ref, o_ref):
    n_heads = q_ref.shape[2] // HEAD_PAD
    sls = [slice(hd * HEAD_PAD, (hd + 1) * HEAD_PAD) for hd in range(n_heads)]
    accs = _attend([q_ref[0, :, sl] for sl in sls], k_ref, v_ref, sls, sls, MLA_ROUNDS, MXU_WIDTH)
    outs = [acc / acc[:, V_B:V_B + 1] for acc in accs]
    lane = _lane_iota(outs[0].shape)
    pairs = [jnp.where(lane < V_B, outs[2 * p], pltpu.roll(outs[2 * p + 1], V_B, 1)) for p in range(n_heads // 2)]
    o_ref[0] = jnp.concatenate(pairs, axis=1).astype(o_ref.dtype)


def _mla_attention(q3, k3, v3):
    bsz, s, _ = q3.shape
    keys = k3.shape[1]
    tq = min(Q_TILE, s)
    nh = _heads_per_step(H_B, keys, 2)
    pair = nh * HEAD_PAD
    return pl.pallas_call(
        _mla_kernel,
        grid=(bsz, H_B // nh, s // tq),
        in_specs=[
            pl.BlockSpec((1, tq, pair), lambda b, h, i: (b, i, h)),
            pl.BlockSpec((1, keys, pair), lambda b, h, i: (b, 0, h)),
            pl.BlockSpec((1, keys, pair), lambda b, h, i: (b, 0, h)),
        ],
        out_specs=pl.BlockSpec((1, tq, nh * V_B), lambda b, h, i: (b, i, h)),
        out_shape=jax.ShapeDtypeStruct((bsz, s, W_B), BF16),
        compiler_params=pltpu.CompilerParams(dimension_semantics=("arbitrary",) * 3,
                                             vmem_limit_bytes=VMEM_LIMIT),
        name="mla_lat" if keys > s else "mla_ctx",
    )(q3, k3, v3)


def _diff_kernel(lam_ref, q_ref, k_ref, v_ref, o_ref, *, lam_init):
    lq = lam_ref[...]
    lam = (jnp.exp(jnp.sum(lq[0:1] * lq[1:2], axis=-1, keepdims=True))
           - jnp.exp(jnp.sum(lq[2:3] * lq[3:4], axis=-1, keepdims=True)) + lam_init)
    n_heads = q_ref.shape[2] // LANES
    lane = _lane_iota((q_ref.shape[1], LANES))
    qcs, ksls, vsls = [], [], []
    for hd in range(n_heads):
        q = q_ref[0, :, hd * LANES:(hd + 1) * LANES].astype(F32)
        qcs += [jnp.where(lane < D_C, q, 0.0).astype(BF16), jnp.where(lane >= D_C, q, 0.0).astype(BF16)]
        ksls += [slice(hd * LANES, (hd + 1) * LANES)] * 2
        vsls += [slice(2 * hd * LANES, 2 * (hd + 1) * LANES)] * 2
    accs = _attend(qcs, k_ref, v_ref, ksls, vsls, DIFF_ROUNDS, LANES)
    res = [acc[:, 0:LANES] / acc[:, LANES:LANES + 1] for acc in accs]
    o_ref[0] = jnp.concatenate([res[2 * hd] - lam * res[2 * hd + 1] for hd in range(n_heads)], axis=1)


def _diff_attention(lam_qk, q3, k3, v3, lam_init):
    bsz, s, _ = q3.shape
    keys = k3.shape[1]
    tq = min(Q_TILE, s)
    nh = _heads_per_step(H_C, keys, 1)
    return pl.pallas_call(
        functools.partial(_diff_kernel, lam_init=lam_init),
        grid=(bsz, H_C // nh, s // tq),
        in_specs=[
            pl.BlockSpec((4, D_C), lambda b, h, i: (0, 0)),
            pl.BlockSpec((1, tq, nh * LANES), lambda b, h, i: (b, i, h)),
            pl.BlockSpec((1, keys, nh * LANES), lambda b, h, i: (b, 0, h)),
            pl.BlockSpec((1, keys, 2 * nh * LANES), lambda b, h, i: (b, 0, h)),
        ],
        out_specs=pl.BlockSpec((1, tq, nh * LANES), lambda b, h, i: (b, i, h)),
        out_shape=jax.ShapeDtypeStruct((bsz, s, W_C), F32),
        compiler_params=pltpu.CompilerParams(dimension_semantics=("arbitrary",) * 3,
                                             vmem_limit_bytes=VMEM_LIMIT),
        name="diff_lat" if keys > s else "diff_ctx",
    )(lam_qk, q3, k3, v3)


def _merge_kernel(x_ref, mod_ref, g1_ref, wg_ref, of_ref, ob_ref, sr_ref, yb_ref, yc_ref, ggla_ref, gsub_ref,
                  woa_ref, wob_ref, woc_ref, wout_ref, g2_ref, w1_ref, w2_ref, o_ref, *, lam_init):
    x = x_ref[...]
    m = mod_ref[0]
    h = _rms(x, g1_ref[...]) * (1.0 + m[1:2]) + m[0:1]
    gates = _sigmoid(_dot(h.astype(BF16), wg_ref[...]))
    o = of_ref[...] + ob_ref[...]
    sr = sr_ref[...]
    yc = yc_ref[...]
    ya_t, yc_t = [], []
    for hd in range(H_A):
        sl = slice(hd * LANES, (hd + 1) * LANES)
        ya_t.append(_rms(o[:, sl], ggla_ref[...]) * sr[:, sl])
        yc_t.append(_rms(yc[:, sl], gsub_ref[...]) * (1.0 - lam_init))
    ya = jnp.concatenate(ya_t, axis=1).astype(BF16)
    ycn = jnp.concatenate(yc_t, axis=1).astype(BF16)
    merged = (gates[:, 0:D_MODEL] * _dot(ya, woa_ref[...])
              + gates[:, D_MODEL:2 * D_MODEL] * _dot(yb_ref[...], wob_ref[...])
              + gates[:, 2 * D_MODEL:3 * D_MODEL] * _dot(ycn, woc_ref[...]))
    x1 = x + m[2:3] * _dot(merged.astype(BF16), wout_ref[...])

    h2 = _rms(x1, g2_ref[...]) * (1.0 + m[4:5]) + m[3:4]
    a = jnp.maximum(_dot(h2.astype(BF16), w1_ref[...]), 0.0)
    o_ref[...] = x1 + m[5:6] * _dot((a * a).astype(BF16), w2_ref[...])


def _merge_mlp(x2, mod8, group0, lw, of2, ob2, sr2, yb2, yc2, lam_init, per_seq_groups):
    t = x2.shape[0]
    tm = MERGE_TILE
    tiles_per_seq = LAT_SEQ // tm
    const = lambda shape: pl.BlockSpec(shape, lambda i: (0,) * len(shape), pipeline_mode=pl.Buffered(1))
    row = lambda w: pl.BlockSpec((tm, w), lambda i: (i, 0))
    if per_seq_groups:
        mod_spec = pl.BlockSpec((1, 8, D_MODEL), lambda i: (group0 + i // tiles_per_seq, 0, 0))
    else:
        mod_spec = pl.BlockSpec((1, 8, D_MODEL), lambda i: (group0, 0, 0))
    return pl.pallas_call(
        functools.partial(_merge_kernel, lam_init=lam_init),
        grid=(t // tm,),
        in_specs=[row(D_MODEL), mod_spec, const((1, D_MODEL)), const((D_MODEL, GATE_COLS)),
                  row(W_A), row(W_A), row(W_A), row(W_B), row(W_C), const((1, LANES)), const((1, LANES)),
                  const((W_A, D_MODEL)), const((W_B, D_MODEL)), const((W_C, D_MODEL)),
                  const((D_MODEL, D_MODEL)), const((1, D_MODEL)), const((D_MODEL, D_FF)), const((D_FF, D_MODEL))],
        out_specs=row(D_MODEL),
        out_shape=jax.ShapeDtypeStruct((t, D_MODEL), F32),
        compiler_params=pltpu.CompilerParams(dimension_semantics=("arbitrary",), vmem_limit_bytes=VMEM_LIMIT),
        name="merge_mlp",
    )(x2, mod8, lw["g1"], lw["w_gates"], of2, ob2, sr2, yb2, yc2, lw["ggla"], lw["gsub"],
      lw["woa"], lw["wob"], lw["woc"], lw["wout"], lw["g2"], lw["w1"], lw["w2"])


def _pad_heads(w, real, pad):
    k = w.shape[0]
    w = w.reshape(k, -1, real)
    return jnp.pad(w, ((0, 0), (0, 0), (0, pad - real))).reshape(k, -1)


def _layer_weights(l, p):
    widths = (H_A * DK_A, H_A * DK_A, W_A, W_A, 2 * GATE_RANK, Q_RANK, KV_RANK, ROPE_B, W_C, W_C, W_C, GATE_COLS)
    offs = np.concatenate([[0], np.cumsum(widths)])
    w_in = p["w_in"][l]
    aq, ak, av, ar, aa, qd, kvd, kr, dq, dk, dv, gates = (w_in[:, offs[i]:offs[i + 1]] for i in range(12))
    z = lambda n: jnp.zeros((D_MODEL, n), F32)
    misc = jnp.concatenate([aa, kr, z(LANES - 2 * GATE_RANK - ROPE_B)], axis=1)
    krp = jnp.concatenate([z(NOPE_B), kr, z(LANES - QK_B)], axis=1)
    part = np.arange(ROPE_B) ^ (ROPE_B // 4)
    krq = jnp.concatenate([z(NOPE_B), kr[:, part], z(LANES - QK_B)], axis=1)
    w_in_a = jnp.concatenate([aq, ak, av, ar, qd, kvd, dq, dk, dv, misc, krp, krq], axis=1).astype(BF16)
    uq = p["w_mla_uq"][l].reshape(Q_RANK, H_B, QK_B)
    uq_part = jnp.concatenate([jnp.zeros((Q_RANK, H_B, NOPE_B), F32), uq[:, :, NOPE_B + part]], axis=2)
    gain_part = lambda g: jnp.pad(g[NOPE_B + part], (NOPE_B, LANES - QK_B)).reshape(1, LANES)
    wa2 = jnp.zeros((LANES, 512), F32)
    wa2 = wa2.at[0:GATE_RANK, 0:256].set(p["w_gla_a2"][l, 0])
    wa2 = wa2.at[GATE_RANK:2 * GATE_RANK, 256:512].set(p["w_gla_a2"][l, 1])
    pad_gain = lambda g: jnp.pad(g, (0, HEAD_PAD - QK_B)).reshape(1, HEAD_PAD)
    twice = lambda g: jnp.concatenate([g, g]).reshape(1, LANES)
    return dict(
        g1=p["g_norm1"][l].reshape(1, D_MODEL), g2=p["g_norm2"][l].reshape(1, D_MODEL),
        w_in_a=w_in_a, w_gates=gates.astype(BF16),
        wa2=wa2.astype(BF16), ba=p["b_gla_a"][l].reshape(1, 512),
        gqa=p["g_mla_qa"][l].reshape(1, Q_RANK), gkva=p["g_mla_kva"][l].reshape(1, KV_RANK),
        wuq=_pad_heads(p["w_mla_uq"][l], QK_B, HEAD_PAD).astype(BF16),
        wuk=_pad_heads(p["w_mla_uk"][l], NOPE_B, HEAD_PAD).astype(BF16),
        wuv=_pad_heads(p["w_mla_uv"][l], V_B, HEAD_PAD).astype(BF16),
        gq=pad_gain(p["g_mla_q"][l]), gk=pad_gain(p["g_mla_k"][l]),
        wuq_part=_pad_heads(uq_part.reshape(Q_RANK, H_B * QK_B), QK_B, HEAD_PAD).astype(BF16),
        gq_part=gain_part(p["g_mla_q"][l]), gk_part=gain_part(p["g_mla_k"][l]),
        gdq=twice(p["g_diff_q"][l]), gdk=twice(p["g_diff_k"][l]),
        ggla=p["g_gla_out"][l].reshape(1, DV_A), gsub=p["g_diff_sub"][l].reshape(1, 2 * D_C),
        woa=p["w_o_gla"][l].astype(BF16), wob=p["w_o_mla"][l].astype(BF16), woc=p["w_o_diff"][l].astype(BF16),
        wout=p["w_out"][l].astype(BF16), w1=p["w_mlp1"][l].astype(BF16), w2=p["w_mlp2"][l].astype(BF16),
        lam_qk=p["lam_qk"][l],
    )


def _layer(x2, bsz, seq, mod8, group0, lw, lam_init, consts, tabs, cached):
    is_latent = cached is not None
    cache = None
    if is_latent:
        s0, ckv_c, krope_c, dk_c, dv_c = cached
        past = ckv_c.shape[1]
        krp_c = jnp.pad(krope_c, ((0, 0), (0, 0), (NOPE_B, LANES - QK_B)))
        cache = (ckv_c, krp_c, dk_c.reshape(bsz, past, W_C), dv_c.reshape(bsz, past, W_C))
    outs = _inproj(x2, bsz, mod8, group0, lw, tabs, cache, rope=is_latent, emit_state=not is_latent)
    gq, gk, gv, sr, gf, gb, mq, mk, mv, dq, dk, dv = outs[:12]
    r3 = lambda a: a.reshape(bsz, seq, a.shape[-1])

    init = (_state_to_blocks(s0[:, 0]), _state_to_blocks(s0[:, 1])) if is_latent else None
    gla = _gla_scan(r3(gq), r3(gk), r3(gv), r3(gf), r3(gb), init, consts, emit_state=not is_latent)
    of, ob = gla[0], gla[1]

    yb = _mla_attention(r3(mq), mk, mv)
    yc = _diff_attention(lw["lam_qk"], r3(dq), dk, dv, lam_init)

    x_out = _merge_mlp(x2, mod8, group0, lw, of.reshape(-1, W_A), ob.reshape(-1, W_A), sr, yb.reshape(-1, W_B),
                       yc.reshape(-1, W_C), lam_init, is_latent)

    new_state = None
    if not is_latent:
        ckv, misc, dk32, dv32 = outs[12:]
        gla_state = jnp.stack([gla[2], gla[3]], axis=1)
        new_state = (gla_state, ckv.reshape(bsz, seq, KV_RANK),
                     misc[:, 2 * GATE_RANK:2 * GATE_RANK + ROPE_B].reshape(bsz, seq, ROPE_B),
                     dk32.reshape(bsz, seq, H_C, 2, D_C), dv32.reshape(bsz, seq, H_C, 2 * D_C))
    return x_out, new_state


def kernel(x_prompt, x_sample, state_gla, cache_mla_ckv, cache_mla_krope, cache_diff_k, cache_diff_v, c, c_ctx, w_mod, b_mod, g_norm1, g_norm2, w_in, w_gla_a2, b_gla_a, g_gla_out, g_mla_qa, g_mla_kva, w_mla_uq, w_mla_uk, w_mla_uv, g_mla_q, g_mla_k, g_diff_q, g_diff_k, lam_qk, g_diff_sub, w_o_gla, w_o_mla, w_o_diff, w_out, w_mlp1, w_mlp2):
    params = dict(w_in=w_in, g_norm1=g_norm1, g_norm2=g_norm2, w_gla_a2=w_gla_a2, b_gla_a=b_gla_a,
                  g_gla_out=g_gla_out, g_mla_qa=g_mla_qa, g_mla_kva=g_mla_kva, w_mla_uq=w_mla_uq,
                  w_mla_uk=w_mla_uk, w_mla_uv=w_mla_uv, g_mla_q=g_mla_q, g_mla_k=g_mla_k, g_diff_q=g_diff_q,
                  g_diff_k=g_diff_k, lam_qk=lam_qk, g_diff_sub=g_diff_sub, w_o_gla=w_o_gla, w_o_mla=w_o_mla,
                  w_o_diff=w_o_diff, w_out=w_out, w_mlp1=w_mlp1, w_mlp2=w_mlp2)
    nb, ns, _ = x_prompt.shape
    db, ds, _ = x_sample.shape
    assert ds == LAT_SEQ and db + 1 <= 8

    cond8 = jnp.concatenate([c_ctx[None, :], c, jnp.zeros((8 - 1 - db, D_MODEL), F32)], axis=0)
    mod = _modulation(cond8, w_mod, b_mod).reshape(DEPTH, 8, 6, D_MODEL)
    mod = jnp.pad(mod, ((0, 0), (0, 0), (0, 2), (0, 0)))

    consts = _gla_constants()
    tabs = (_rope_tables(ROPE_B, NOPE_B, LANES), _rope_tables(D_C, 0, D_C))

    yp = x_prompt.reshape(nb * ns, D_MODEL)
    ys = x_sample.reshape(db * ds, D_MODEL)
    states = []
    for l in range(DEPTH):
        lw = _layer_weights(l, params)
        lam_init = 0.8 - 0.6 * math.exp(-0.3 * l)
        yp, st = _layer(yp, nb, ns, mod[l], 0, lw, lam_init, consts, tabs, None)
        states.append(st)
        cached = (state_gla[:, l], cache_mla_ckv[:, l], cache_mla_krope[:, l], cache_diff_k[:, l],
                  cache_diff_v[:, l])
        ys, _ = _layer(ys, db, ds, mod[l], 1, lw, lam_init, consts, tabs, cached)
    stack = lambda i: jnp.stack([s[i] for s in states], axis=1)
    return (yp.reshape(nb, ns, D_MODEL), ys.reshape(db, ds, D_MODEL), stack(0), stack(1), stack(2), stack(3),
            stack(4))
```

```python
import functools
import math

import numpy as np
import jax
import jax.numpy as jnp
from jax import lax
from jax.experimental import pallas as pl
from jax.experimental.pallas import tpu as pltpu

F32 = jnp.float32
BF16 = jnp.bfloat16

D_MODEL = 1024
DEPTH = 2
GRID_W = 64
ROPE_BASE = 10000.0
EPS = 1e-6
H_A, DK_A, DV_A = 4, 64, 128
GATE_RANK = 16
GATE_TAU = 16.0
CHUNK = 64
W_A = H_A * DV_A
H_B, Q_RANK, KV_RANK = 8, 384, 256
NOPE_B, ROPE_B, V_B = 64, 32, 64
QK_B = NOPE_B + ROPE_B
W_B = H_B * V_B
H_C, D_C = 4, 64
W_C = H_C * 2 * D_C
D_FF = 4 * D_MODEL
GATE_COLS = 3 * D_MODEL
LAT_SEQ = 4096

LANES = 128
HEAD_PAD = LANES
MXU_WIDTH = 256
VMEM_LIMIT = 56 * 1024 * 1024

TOKEN_TILE = 256
MERGE_TILE = 512
FF_PARTS = 2
MOD_TILE = 1536
Q_TILE = 1024
MLA_ROUNDS = 2
DIFF_ROUNDS = 2
GLA_TILE = 256

C_AQ, C_AK, C_AV, C_AR = 0, 256, 512, 1024
C_QD, C_KVD = 1536, 1920
C_DQ, C_DK, C_DV = 2176, 2688, 3200
C_MISC, C_KRP, C_KRQ = 3712, 3840, 3968
IN_COLS_A = 4096

NEG_BIG = -1e30
LOG2E = math.log2(math.e)


def _dot(a, b):
    return jnp.dot(a, b, preferred_element_type=F32)


def _dot_nt(a, b):
    return lax.dot_general(a, b, (((1,), (1,)), ((), ())), preferred_element_type=F32)


def _sigmoid(x):
    return 1.0 / (1.0 + jnp.exp(-x))


def _rms(x, g):
    ms = jnp.mean(x * x, axis=-1, keepdims=True)
    return x * lax.rsqrt(ms + EPS) * g


def _lane_iota(shape):
    return lax.broadcasted_iota(jnp.int32, shape, len(shape) - 1)


def _ones_lane0(shape):
    return jnp.where(_lane_iota(shape) % LANES == 0, 1.0, 0.0).astype(F32)


def _mod_kernel(c_ref, w_ref, b_ref, o_ref):
    c = c_ref[...]
    s = c * _sigmoid(c)
    o_ref[0] = _dot(s.astype(BF16), w_ref[0].astype(BF16)) + b_ref[0]


def _modulation(cond8, w_mod, b_mod):
    nt = MOD_TILE
    return pl.pallas_call(
        _mod_kernel,
        grid=(DEPTH, 6 * D_MODEL // nt),
        in_specs=[
            pl.BlockSpec((8, D_MODEL), lambda l, j: (0, 0)),
            pl.BlockSpec((1, D_MODEL, nt), lambda l, j: (l, 0, j)),
            pl.BlockSpec((1, 1, nt), lambda l, j: (l, 0, j)),
        ],
        out_specs=pl.BlockSpec((1, 8, nt), lambda l, j: (l, 0, j)),
        out_shape=jax.ShapeDtypeStruct((DEPTH, 8, 6 * D_MODEL), F32),
        compiler_params=pltpu.CompilerParams(vmem_limit_bytes=VMEM_LIMIT),
        name="modulation",
    )(cond8, w_mod, b_mod.reshape(DEPTH, 1, 6 * D_MODEL))


def _rope_tables(n_rope, lane0, width):
    half = n_rope // 2
    nf = half // 2
    inv = ROPE_BASE ** (-np.arange(nf, dtype=np.float64) / nf)
    pos = np.arange(GRID_W, dtype=np.float64)[:, None]
    cr = np.zeros((GRID_W, LANES)); cc = np.ones((GRID_W, LANES))
    sr = np.zeros((GRID_W, LANES)); sc = np.zeros((GRID_W, LANES))
    for base in range(0, LANES, width):
        for d in range(n_rope):
            lane = base + lane0 + d
            within = d % half
            ang = pos[:, 0] * inv[within % nf]
            sign = -1.0 if within < nf else 1.0
            if d < half:
                cr[:, lane] = np.cos(ang); cc[:, lane] = 0.0
                sr[:, lane] = sign * np.sin(ang)
            else:
                cc[:, lane] = np.cos(ang)
                sc[:, lane] = sign * np.sin(ang)
    return tuple(jnp.asarray(t, F32) for t in (cr, cc, sr, sc))


def _tile_tables(tabs, row0, n_sub):
    cr_ref, cc_ref, sr_ref, sc_ref = tabs
    cs, ss = [], []
    for j in range(n_sub):
        r = (row0 + j) % GRID_W
        cs.append(cr_ref[pl.ds(r, 1), :] + cc_ref[...])
        ss.append(sr_ref[pl.ds(r, 1), :] + sc_ref[...])
    return jnp.concatenate(cs, axis=0), jnp.concatenate(ss, axis=0)


def _rotate(x, cos_t, sin_t, nf):
    lane = _lane_iota(x.shape)
    up = pltpu.roll(x, LANES - nf, 1)
    dn = pltpu.roll(x, nf, 1)
    partner = jnp.where((lane & nf) == 0, up, dn)
    return x * cos_t + partner * sin_t


def _inproj_kernel(*refs, rope, emit_state, tiles_per_seq, has_cache):
    it = iter(refs)
    if has_cache:
        ckvc_ref, krpc_ref, dkc_ref, dvc_ref = (next(it) for _ in range(4))
    x_ref, mod_ref, g1_ref, w_ref, wa2_ref, ba_ref = (next(it) for _ in range(6))
    gqa_ref, gkva_ref, wuq_ref, wuk_ref, wuv_ref, gq_ref, gk_ref = (next(it) for _ in range(7))
    gdq_ref, gdk_ref = next(it), next(it)
    if rope:
        tab_m = tuple(next(it) for _ in range(4))
        tab_d = tuple(next(it) for _ in range(4))
        wuqp_ref, gqp_ref, gkp_ref = (next(it) for _ in range(3))
    o_gq, o_gk, o_gv, o_sr, o_gf, o_gb = (next(it) for _ in range(6))
    o_mq, o_mk, o_mv, o_dq, o_dk, o_dv = (next(it) for _ in range(6))
    if emit_state:
        o_ckv, o_misc, o_dk32, o_dv32 = (next(it) for _ in range(4))

    def mla_keys(kn, krp, dst):
        for hd in range(H_B):
            sl = slice(hd * HEAD_PAD, (hd + 1) * HEAD_PAD)
            kt = kn[:, sl] + krp
            kt = kt * lax.rsqrt(jnp.sum(kt * kt, axis=-1, keepdims=True) * (1.0 / QK_B) + EPS) * gk_ref[...]
            dst[0, :, sl] = kt.astype(BF16)

    def with_ones(v):
        one_tile = _ones_lane0((v.shape[0], LANES))
        return jnp.concatenate(
            [t for hd in range(H_C) for t in (v[:, hd * LANES:(hd + 1) * LANES], one_tile)], axis=1).astype(BF16)

    def cache_rows():
        c = ckvc_ref[0].astype(BF16)
        vb = _dot(c, wuv_ref[...])
        o_mv[0] = jnp.where(_lane_iota(vb.shape) % HEAD_PAD == V_B, 1.0, vb).astype(BF16)
        mla_keys(_dot(c, wuk_ref[...]), krpc_ref[0], o_mk)
        o_dk[0] = dkc_ref[0].astype(BF16)
        o_dv[0] = with_ones(dvc_ref[0])

    def token_rows(seq_tile):
        tm = x_ref.shape[0]
        x = x_ref[...]
        m = mod_ref[0]
        h = _rms(x, g1_ref[...]) * (1.0 + m[1:2]) + m[0:1]
        y = _dot(h.astype(BF16), w_ref[...])

        o_gq[...] = y[:, C_AQ:C_AQ + 256] * (DK_A ** -0.5)
        o_gk[...] = y[:, C_AK:C_AK + 256]
        o_gv[...] = y[:, C_AV:C_AV + 512].astype(BF16)
        r = y[:, C_AR:C_AR + 512]
        o_sr[...] = r * _sigmoid(r)
        misc = y[:, C_MISC:C_MISC + LANES]
        logits = _dot(misc.astype(BF16), wa2_ref[...]) + ba_ref[...]
        logsig = jnp.minimum(logits, 0.0) - jnp.log(1.0 + jnp.exp(-jnp.abs(logits)))
        gate = logsig * (1.0 / GATE_TAU)
        o_gf[...] = gate[:, 0:256]
        o_gb[...] = gate[:, 256:512]

        if rope:
            row0 = seq_tile * (tm // GRID_W)
            cos_m, sin_m = _tile_tables(tab_m, row0, tm // GRID_W)
            cos_d, sin_d = _tile_tables(tab_d, row0, tm // GRID_W)

        qn = _rms(y[:, C_QD:C_QD + Q_RANK], gqa_ref[...])
        q8 = _dot(qn.astype(BF16), wuq_ref[...])
        ckv = _rms(y[:, C_KVD:C_KVD + KV_RANK], gkva_ref[...])
        kn = _dot(ckv.astype(BF16), wuk_ref[...])
        vb = _dot(ckv.astype(BF16), wuv_ref[...])
        o_mv[0] = jnp.where(_lane_iota(vb.shape) % HEAD_PAD == V_B, 1.0, vb).astype(BF16)
        krp = y[:, C_KRP:C_KRP + LANES]
        if rope:
            q8p = _dot(qn.astype(BF16), wuqp_ref[...])
            gq_sin = gqp_ref[...] * sin_m
            k_part = y[:, C_KRQ:C_KRQ + LANES] * (gkp_ref[...] * sin_m)
        for hd in range(H_B):
            sl = slice(hd * HEAD_PAD, (hd + 1) * HEAD_PAD)
            qt = q8[:, sl]
            rq = lax.rsqrt(jnp.sum(qt * qt, axis=-1, keepdims=True) * (1.0 / QK_B) + EPS)
            qt = qt * rq * gq_ref[...]
            kt = kn[:, sl] + krp
            rk = lax.rsqrt(jnp.sum(kt * kt, axis=-1, keepdims=True) * (1.0 / QK_B) + EPS)
            kt = kt * rk * gk_ref[...]
            if rope:
                qt = qt * cos_m + (q8p[:, sl] * gq_sin) * rq
                kt = kt * cos_m + k_part * rk
            o_mq[:, sl] = (qt * (QK_B ** -0.5 * LOG2E)).astype(BF16)
            o_mk[0, :, sl] = kt.astype(BF16)

        lane = _lane_iota((tm, LANES))
        lo = lane < D_C
        for hd in range(H_C):
            sl = slice(hd * LANES, (hd + 1) * LANES)
            for src, g_ref, dst, dst32, scale in (
                (C_DQ, gdq_ref, o_dq, None, D_C ** -0.5 * LOG2E),
                (C_DK, gdk_ref, o_dk.at[0], o_dk32 if emit_state else None, 1.0),
            ):
                t = y[:, src + hd * LANES:src + (hd + 1) * LANES]
                sq = t * t
                s_lo = jnp.sum(jnp.where(lo, sq, 0.0), axis=-1, keepdims=True)
                s_hi = jnp.sum(jnp.where(lo, 0.0, sq), axis=-1, keepdims=True)
                inv = jnp.where(lo, lax.rsqrt(s_lo * (1.0 / D_C) + EPS), lax.rsqrt(s_hi * (1.0 / D_C) + EPS))
                t = t * inv * g_ref[...]
                if rope:
                    t = _rotate(t, cos_d, sin_d, D_C // 4)
                if dst32 is not None:
                    dst32[:, sl] = t
                dst[:, sl] = (t * scale).astype(BF16)
        dv = y[:, C_DV:C_DV + W_C]
        o_dv[0] = with_ones(dv)
        if emit_state:
            o_dv32[...] = dv
            o_ckv[...] = ckv
            o_misc[...] = misc

    if not has_cache:
        token_rows(pl.program_id(0) % tiles_per_seq)
        return
    step = pl.program_id(0) % (tiles_per_seq + 1)
    pl.when(step == 0)(cache_rows)
    pl.when(step > 0)(lambda: token_rows(step - 1))


def _inproj(x2, bsz, mod8, group0, lw, tabs, cache, *, rope, emit_state):
    t = x2.shape[0]
    tm = TOKEN_TILE
    seq = t // bsz
    tps = seq // tm
    has_cache = cache is not None
    past = cache[0].shape[1] if has_cache else 0
    assert past in (0, tm)
    lead = 1 if has_cache else 0
    req = lambda i: i // (tps + lead)
    tile = lambda i: jnp.maximum(i % (tps + lead) - lead, 0)
    kv_rows = lambda w: pl.BlockSpec((1, tm, w), lambda i: (req(i), i % (tps + lead), 0))
    const = lambda shape: pl.BlockSpec(shape, lambda i: (0,) * len(shape))
    row = lambda w: pl.BlockSpec((tm, w), lambda i: (req(i) * tps + tile(i), 0))
    if rope:
        mod_spec = pl.BlockSpec((1, 8, D_MODEL), lambda i: (group0 + req(i), 0, 0))
    else:
        mod_spec = pl.BlockSpec((1, 8, D_MODEL), lambda i: (group0, 0, 0))
    cache_specs = [pl.BlockSpec((1, past, c.shape[-1]), lambda i: (req(i), 0, 0)) for c in cache] if has_cache else []
    in_specs = cache_specs + [
        row(D_MODEL), mod_spec, const((1, D_MODEL)), const((D_MODEL, IN_COLS_A)),
        const((LANES, 512)), const((1, 512)),
        const((1, Q_RANK)), const((1, KV_RANK)), const((Q_RANK, H_B * HEAD_PAD)),
        const((KV_RANK, H_B * HEAD_PAD)), const((KV_RANK, H_B * HEAD_PAD)), const((1, LANES)), const((1, LANES)),
        const((1, LANES)), const((1, LANES)),
    ]
    args = (list(cache) if has_cache else []) + [
        x2, mod8, lw["g1"], lw["w_in_a"], lw["wa2"], lw["ba"], lw["gqa"], lw["gkva"], lw["wuq"],
        lw["wuk"], lw["wuv"], lw["gq"], lw["gk"], lw["gdq"], lw["gdk"]]
    if rope:
        in_specs += [const((GRID_W, LANES))] * 8 + [const((Q_RANK, H_B * HEAD_PAD)), const((1, LANES)), const((1, LANES))]
        args += list(tabs[0]) + list(tabs[1]) + [lw["wuq_part"], lw["gq_part"], lw["gk_part"]]
    flat, keyed = "flat", "keyed"
    outs = [(256, F32, flat), (256, F32, flat), (512, BF16, flat), (512, F32, flat), (256, F32, flat),
            (256, F32, flat), (H_B * HEAD_PAD, BF16, flat), (H_B * HEAD_PAD, BF16, keyed),
            (H_B * HEAD_PAD, BF16, keyed), (W_C, BF16, flat), (W_C, BF16, keyed), (2 * W_C, BF16, keyed)]
    if emit_state:
        outs += [(KV_RANK, F32, flat), (LANES, F32, flat), (W_C, F32, flat), (W_C, F32, flat)]
    return pl.pallas_call(
        functools.partial(_inproj_kernel, rope=rope, emit_state=emit_state, tiles_per_seq=tps,
                          has_cache=has_cache),
        grid=(bsz * (tps + lead),),
        in_specs=in_specs,
        out_specs=[row(w) if kind == flat else kv_rows(w) for w, _, kind in outs],
        out_shape=[jax.ShapeDtypeStruct((t, w) if kind == flat else (bsz, past + seq, w), dt)
                   for w, dt, kind in outs],
        compiler_params=pltpu.CompilerParams(dimension_semantics=("arbitrary",), vmem_limit_bytes=VMEM_LIMIT),
        name="inproj_lat" if rope else "inproj_ctx",
    )(*args)


def _gla_constants():
    c = CHUNK
    idx = np.arange(c)
    t, u = idx[:, None], idx[None, :]
    mats = [(u <= t)]
    masks = [2.0 * np.eye(c)]
    hs = c // 2
    while hs >= 1:
        blk = idx // (2 * hs)
        lower = (idx % (2 * hs)) < hs
        p = blk * 2 * hs + hs - 1
        m_low = lower[:, None] & (u > t) & (u <= p[:, None])
        m_up = (~lower)[:, None] & (u > p[:, None]) & (u <= t)
        mats.append(m_low | m_up)
        masks.append(((blk[:, None] == blk[None, :]) & (lower[:, None] != lower[None, :])).astype(np.float64))
        hs //= 2
    mats.append(u > t)
    mats = [m.astype(np.float32) for m in mats]

    def stacked(ms):
        cm = np.concatenate(ms, axis=0)
        return jnp.asarray(np.concatenate([cm, cm], axis=1), BF16)

    cms_f = stacked(mats)
    mats_b = [m[::-1, ::-1] for m in mats]
    cms_b = stacked(mats_b)
    cms2_b = stacked([mats_b[0], mats_b[-1]])
    lm = jnp.asarray(np.stack([np.tile(m, (1, H_A)) for m in masks]), F32)
    same = lambda n_row, n_col: (np.arange(H_A * n_row)[:, None] // n_row
                                 == np.arange(H_A * n_col)[None, :] // n_col).astype(np.float32)
    km = same(c, DK_A)
    vm = same(c, DV_A)
    sm = same(DV_A, DK_A)
    return (cms_f, cms_b, cms2_b, lm, jnp.asarray(km, BF16), jnp.asarray(vm, BF16), jnp.asarray(sm, BF16))


N_LEVEL = 6


def _gate_sums(cms_ref, g):
    g_hi = g.astype(BF16)
    g_lo = (g - g_hi.astype(F32)).astype(BF16)
    return _dot(cms_ref[...], jnp.concatenate([g_hi, g_lo], axis=0))


def _gla_intra(q, k, v16, sums_f, sums_b, upper, lm_ref, km_ref, vm_ref):
    def per_head_rows(x):
        return jnp.concatenate([x.astype(BF16)] * H_A, axis=0) * km_ref[...]

    att = lm_ref[0] * _dot_nt(q.astype(BF16), per_head_rows(k))
    for lv in range(1, N_LEVEL + 1):
        rows = slice(lv * CHUNK, (lv + 1) * CHUNK)
        ef, eb = jnp.exp(sums_f[rows]), jnp.exp(sums_b[rows])
        qs = jnp.where(upper[lv - 1], ef, eb)
        ks = jnp.where(upper[lv - 1], eb, ef)
        att = att + lm_ref[lv] * _dot_nt((q * qs).astype(BF16), per_head_rows(k * ks))
    v_rows = jnp.concatenate([v16] * H_A, axis=0) * vm_ref[...]
    return _dot(att.astype(BF16), v_rows)


def _gla_step(q, k, v16, b, to_end, b_total, st, sm_ref):
    qe = (q * jnp.exp(b)).astype(BF16)
    out = _dot_nt(qe, st.astype(BF16) * sm_ref[...])
    ke = (k * jnp.exp(to_end)).astype(BF16)
    upd = pl.dot(v16, ke, trans_a=True)
    return out, st * jnp.exp(b_total) + upd


def _gla_kernel(*refs, has_init, emit_state):
    it = iter(refs)
    qf_ref, kf_ref, vf_ref, gf_ref, gbf_ref, qb_ref, kb_ref, vb_ref, gb_ref = (next(it) for _ in range(9))
    if has_init:
        s0f_ref, s0b_ref = next(it), next(it)
    cmsf_ref, cmsb_ref, cms2b_ref, lm_ref, km_ref, vm_ref, sm_ref = (next(it) for _ in range(7))
    of_ref, ob_ref = next(it), next(it)
    if emit_state:
        sff_ref, sfb_ref = next(it), next(it)
    stf_ref, stb_ref = next(it), next(it)
    ti = pl.program_id(1)
    nt = pl.num_programs(1)

    @pl.when(ti == 0)
    def _():
        if has_init:
            stf_ref[...] = s0f_ref[0]
            stb_ref[...] = s0b_ref[0]
        else:
            stf_ref[...] = jnp.zeros(stf_ref.shape, F32)
            stb_ref[...] = jnp.zeros(stb_ref.shape, F32)

    row = lax.broadcasted_iota(jnp.int32, (CHUNK, H_A * DK_A), 0)
    upper = [(row & (CHUNK >> lv)) != 0 for lv in range(1, N_LEVEL + 1)]
    end = slice((N_LEVEL + 1) * CHUNK, (N_LEVEL + 2) * CHUNK)
    n_chunk = qf_ref.shape[1] // CHUNK
    stf = stf_ref[...]
    stb = stb_ref[...]
    for ci in range(n_chunk):
        rows = slice(ci * CHUNK, (ci + 1) * CHUNK)
        q, k, v16 = qf_ref[0, rows, :], kf_ref[0, rows, :], vf_ref[0, rows, :]
        sums_f = _gate_sums(cmsf_ref, gf_ref[0, rows, :])
        sums_b = _gate_sums(cmsb_ref, gbf_ref[0, rows, :])
        intra = _gla_intra(q, k, v16, sums_f, sums_b, upper, lm_ref, km_ref, vm_ref)
        b = sums_f[0:CHUNK]
        inter, stf = _gla_step(q, k, v16, b, sums_f[end], b[CHUNK - 1:CHUNK], stf, sm_ref)
        of_ref[0, rows, :] = intra + inter
        cb = n_chunk - 1 - ci
        rows = slice(cb * CHUNK, (cb + 1) * CHUNK)
        q, k, v16 = qb_ref[0, rows, :], kb_ref[0, rows, :], vb_ref[0, rows, :]
        sums = _gate_sums(cms2b_ref, gb_ref[0, rows, :])
        b = sums[0:CHUNK]
        inter, stb = _gla_step(q, k, v16, b, sums[CHUNK:2 * CHUNK], b[0:1], stb, sm_ref)
        ob_ref[0, rows, :] = inter
    stf_ref[...] = stf
    stb_ref[...] = stb

    if emit_state:
        @pl.when(ti == nt - 1)
        def _():
            for dst, st_ref in ((sff_ref, stf_ref), (sfb_ref, stb_ref)):
                blocks = st_ref[...].T
                for hd in range(H_A):
                    dst[0, hd] = blocks[hd * DK_A:(hd + 1) * DK_A, hd * DV_A:(hd + 1) * DV_A]


def _gla_scan(q3, k3, v3, gf3, gb3, init, consts, *, emit_state):
    bsz, s, _ = q3.shape
    tt = GLA_TILE
    nt = s // tt
    fmap = lambda b, t: (b, t, 0)
    rmap = lambda b, t: (b, nt - 1 - t, 0)
    bmap = lambda b, t: (b, 0, 0)
    const = lambda shape: pl.BlockSpec(shape, lambda b, t: (0,) * len(shape))
    tile = lambda w, m: pl.BlockSpec((1, tt, w), m)
    state = pl.BlockSpec((1, H_A * DV_A, H_A * DK_A), bmap)
    in_specs = [tile(256, fmap), tile(256, fmap), tile(512, fmap), tile(256, fmap), tile(256, fmap),
                tile(256, rmap), tile(256, rmap), tile(512, rmap), tile(256, rmap)]
    args = [q3, k3, v3, gf3, gb3, q3, k3, v3, gb3]
    if init is not None:
        in_specs += [state, state]
        args += list(init)
    in_specs += [const(c.shape) for c in consts]
    args += list(consts)
    out_specs = [tile(W_A, fmap), tile(W_A, rmap)]
    out_shape = [jax.ShapeDtypeStruct((bsz, s, W_A), F32)] * 2
    if emit_state:
        out_specs += [pl.BlockSpec((1, H_A, DK_A, DV_A), lambda b, t: (b, 0, 0, 0))] * 2
        out_shape += [jax.ShapeDtypeStruct((bsz, H_A, DK_A, DV_A), F32)] * 2
    return pl.pallas_call(
        functools.partial(_gla_kernel, has_init=init is not None, emit_state=emit_state),
        grid=(bsz, nt),
        in_specs=in_specs,
        out_specs=out_specs,
        out_shape=out_shape,
        scratch_shapes=[pltpu.VMEM((H_A * DV_A, H_A * DK_A), F32)] * 2,
        compiler_params=pltpu.CompilerParams(dimension_semantics=("arbitrary", "arbitrary"),
                                             vmem_limit_bytes=VMEM_LIMIT),
        name="gla_lat" if init is not None else "gla_ctx",
    )(*args)


def _state_to_blocks(s0):
    st = jnp.swapaxes(s0, -1, -2)
    eye = jnp.eye(H_A, dtype=s0.dtype)
    full = st[:, :, :, None, :] * eye[None, :, None, :, None]
    return full.reshape(s0.shape[0], H_A * DV_A, H_A * DK_A)


def _online_step(carry, q, kt, vt):
    m, acc = carry
    s = _dot_nt(q, kt)
    m_new = jnp.maximum(m, jnp.max(s, axis=-1, keepdims=True))
    p = jnp.exp2(s - m_new)
    acc = jnp.exp2(m - m_new) * acc + _dot(p.astype(BF16), vt)
    return m_new, acc


def _key_rounds(n_keys, n_rounds, align):
    units = n_keys // align
    assert units * align == n_keys
    n_rounds = 1 if n_keys <= 2 * MXU_WIDTH else min(n_rounds, units)
    sizes = [(units // n_rounds + (1 if r < units % n_rounds else 0)) * align for r in range(n_rounds)]
    starts = np.concatenate([[0], np.cumsum(sizes)[:-1]])
    return [(int(s), int(z)) for s, z in zip(starts, sizes)]


def _attend(qs, k_ref, v_ref, ksls, vsls, n_rounds, align):
    tq = qs[0].shape[0]
    n = len(qs)
    width = vsls[0].stop - vsls[0].start
    init = (jnp.full((tq, 1), NEG_BIG, F32), jnp.zeros((tq, width), F32))
    carries = [init for _ in range(n)]
    for start, size in _key_rounds(k_ref.shape[1], n_rounds, align):
        rows = slice(start, start + size)
        carries = [_online_step(carries[i], qs[i], k_ref[0, rows, ksls[i]], v_ref[0, rows, vsls[i]])
                   for i in range(n)]
    return [c[1] for c in carries]


def _heads_per_step(n_heads, n_keys, at_least):
    return n_heads if n_keys <= 2 * MXU_WIDTH else at_least


def _mla_kernel(q_ref, k_ref, v_ref, o_ref):
    n_heads = q_ref.shape[2] // HEAD_PAD
    sls = [slice(hd * HEAD_PAD, (hd + 1) * HEAD_PAD) for hd in range(n_heads)]
    accs = _attend([q_ref[0, :, sl] for sl in sls], k_ref, v_ref, sls, sls, MLA_ROUNDS, MXU_WIDTH)
    outs = [acc / acc[:, V_B:V_B + 1] for acc in accs]
    lane = _lane_iota(outs[0].shape)
    pairs = [jnp.where(lane < V_B, outs[2 * p], pltpu.roll(outs[2 * p + 1], V_B, 1)) for p in range(n_heads // 2)]
    o_ref[0] = jnp.concatenate(pairs, axis=1).astype(o_ref.dtype)


def _mla_attention(q3, k3, v3):
    bsz, s, _ = q3.shape
    keys = k3.shape[1]
    tq = min(Q_TILE, s)
    nh = _heads_per_step(H_B, keys, 2)
    pair = nh * HEAD_PAD
    return pl.pallas_call(
        _mla_kernel,
        grid=(bsz, H_B // nh, s // tq),
        in_specs=[
            pl.BlockSpec((1, tq, pair), lambda b, h, i: (b, i, h)),
            pl.BlockSpec((1, keys, pair), lambda b, h, i: (b, 0, h)),
            pl.BlockSpec((1, keys, pair), lambda b, h, i: (b, 0, h)),
        ],
        out_specs=pl.BlockSpec((1, tq, nh * V_B), lambda b, h, i: (b, i, h)),
        out_shape=jax.ShapeDtypeStruct((bsz, s, W_B), BF16),
        compiler_params=pltpu.CompilerParams(dimension_semantics=("arbitrary",) * 3,
                                             vmem_limit_bytes=VMEM_LIMIT),
        name="mla_lat" if keys > s else "mla_ctx",
    )(q3, k3, v3)


def _diff_kernel(lam_ref, q_ref, k_ref, v_ref, o_ref, *, lam_init):
    lq = lam_ref[...]
    lam = (jnp.exp(jnp.sum(lq[0:1] * lq[1:2], axis=-1, keepdims=True))
           - jnp.exp(jnp.sum(lq[2:3] * lq[3:4], axis=-1, keepdims=True)) + lam_init)
    n_heads = q_ref.shape[2] // LANES
    lane = _lane_iota((q_ref.shape[1], LANES))
    qcs, ksls, vsls = [], [], []
    for hd in range(n_heads):
        q = q_ref[0, :, hd * LANES:(hd + 1) * LANES].astype(F32)
        qcs += [jnp.where(lane < D_C, q, 0.0).astype(BF16), jnp.where(lane >= D_C, q, 0.0).astype(BF16)]
        ksls += [slice(hd * LANES, (hd + 1) * LANES)] * 2
        vsls += [slice(2 * hd * LANES, 2 * (hd + 1) * LANES)] * 2
    accs = _attend(qcs, k_ref, v_ref, ksls, vsls, DIFF_ROUNDS, LANES)
    res = [acc[:, 0:LANES] / acc[:, LANES:LANES + 1] for acc in accs]
    o_ref[0] = jnp.concatenate([res[2 * hd] - lam * res[2 * hd + 1] for hd in range(n_heads)], axis=1)


def _diff_attention(lam_qk, q3, k3, v3, lam_init):
    bsz, s, _ = q3.shape
    keys = k3.shape[1]
    tq = min(Q_TILE, s)
    nh = _heads_per_step(H_C, keys, 1)
    return pl.pallas_call(
        functools.partial(_diff_kernel, lam_init=lam_init),
        grid=(bsz, H_C // nh, s // tq),
        in_specs=[
            pl.BlockSpec((4, D_C), lambda b, h, i: (0, 0)),
            pl.BlockSpec((1, tq, nh * LANES), lambda b, h, i: (b, i, h)),
            pl.BlockSpec((1, keys, nh * LANES), lambda b, h, i: (b, 0, h)),
            pl.BlockSpec((1, keys, 2 * nh * LANES), lambda b, h, i: (b, 0, h)),
        ],
        out_specs=pl.BlockSpec((1, tq, nh * LANES), lambda b, h, i: (b, i, h)),
        out_shape=jax.ShapeDtypeStruct((bsz, s, W_C), F32),
        compiler_params=pltpu.CompilerParams(dimension_semantics=("arbitrary",) * 3,
                                             vmem_limit_bytes=VMEM_LIMIT),
        name="diff_lat" if keys > s else "diff_ctx",
    )(lam_qk, q3, k3, v3)


def _merge_kernel(x_ref, mod_ref, g1_ref, wg_ref, of_ref, ob_ref, sr_ref, yb_ref, yc_ref, ggla_ref, gsub_ref,
                  woa_ref, wob_ref, woc_ref, wout_ref, g2_ref, w1_ref, w2_ref, o_ref, *, lam_init):
    x = x_ref[...]
    m = mod_ref[0]
    h = _rms(x, g1_ref[...]) * (1.0 + m[1:2]) + m[0:1]
    gates = _sigmoid(_dot(h.astype(BF16), wg_ref[...]))
    o = of_ref[...] + ob_ref[...]
    sr = sr_ref[...]
    yc = yc_ref[...]
    ya_t, yc_t = [], []
    for hd in range(H_A):
        sl = slice(hd * LANES, (hd + 1) * LANES)
        ya_t.append(_rms(o[:, sl], ggla_ref[...]) * sr[:, sl])
        yc_t.append(_rms(yc[:, sl], gsub_ref[...]) * (1.0 - lam_init))
    ya = jnp.concatenate(ya_t, axis=1).astype(BF16)
    ycn = jnp.concatenate(yc_t, axis=1).astype(BF16)
    merged = (gates[:, 0:D_MODEL] * _dot(ya, woa_ref[...])
              + gates[:, D_MODEL:2 * D_MODEL] * _dot(yb_ref[...], wob_ref[...])
              + gates[:, 2 * D_MODEL:3 * D_MODEL] * _dot(ycn, woc_ref[...]))
    x1 = x + m[2:3] * _dot(merged.astype(BF16), wout_ref[...])

    h2 = _rms(x1, g2_ref[...]) * (1.0 + m[4:5]) + m[3:4]
    h2b = h2.astype(BF16)
    mlp = None
    for part in range(FF_PARTS):
        sl = slice(part * (D_FF // FF_PARTS), (part + 1) * (D_FF // FF_PARTS))
        a = jnp.maximum(_dot(h2b, w1_ref[:, sl]), 0.0)
        term = _dot((a * a).astype(BF16), w2_ref[sl, :])
        mlp = term if mlp is None else mlp + term
    o_ref[...] = x1 + m[5:6] * mlp


def _merge_mlp(x2, mod8, group0, lw, of2, ob2, sr2, yb2, yc2, lam_init, per_seq_groups):
    t = x2.shape[0]
    tm = MERGE_TILE
    tiles_per_seq = LAT_SEQ // tm
    const = lambda shape: pl.BlockSpec(shape, lambda i: (0,) * len(shape), pipeline_mode=pl.Buffered(1))
    row = lambda w: pl.BlockSpec((tm, w), lambda i: (i, 0))
    if per_seq_groups:
        mod_spec = pl.BlockSpec((1, 8, D_MODEL), lambda i: (group0 + i // tiles_per_seq, 0, 0))
    else:
        mod_spec = pl.BlockSpec((1, 8, D_MODEL), lambda i: (group0, 0, 0))
    return pl.pallas_call(
        functools.partial(_merge_kernel, lam_init=lam_init),
        grid=(t // tm,),
        in_specs=[row(D_MODEL), mod_spec, const((1, D_MODEL)), const((D_MODEL, GATE_COLS)),
                  row(W_A), row(W_A), row(W_A), row(W_B), row(W_C), const((1, LANES)), const((1, LANES)),
                  const((W_A, D_MODEL)), const((W_B, D_MODEL)), const((W_C, D_MODEL)),
                  const((D_MODEL, D_MODEL)), const((1, D_MODEL)), const((D_MODEL, D_FF)), const((D_FF, D_MODEL))],
        out_specs=row(D_MODEL),
        out_shape=jax.ShapeDtypeStruct((t, D_MODEL), F32),
        compiler_params=pltpu.CompilerParams(dimension_semantics=("arbitrary",), vmem_limit_bytes=VMEM_LIMIT),
        name="merge_mlp",
    )(x2, mod8, lw["g1"], lw["w_gates"], of2, ob2, sr2, yb2, yc2, lw["ggla"], lw["gsub"],
      lw["woa"], lw["wob"], lw["woc"], lw["wout"], lw["g2"], lw["w1"], lw["w2"])


def _pad_heads(w, real, pad):
    k = w.shape[0]
    w = w.reshape(k, -1, real)
    return jnp.pad(w, ((0, 0), (0, 0), (0, pad - real))).reshape(k, -1)


def _layer_weights(l, p):
    widths = (H_A * DK_A, H_A * DK_A, W_A, W_A, 2 * GATE_RANK, Q_RANK, KV_RANK, ROPE_B, W_C, W_C, W_C, GATE_COLS)
    offs = np.concatenate([[0], np.cumsum(widths)])
    w_in = p["w_in"][l]
    aq, ak, av, ar, aa, qd, kvd, kr, dq, dk, dv, gates = (w_in[:, offs[i]:offs[i + 1]] for i in range(12))
    z = lambda n: jnp.zeros((D_MODEL, n), F32)
    misc = jnp.concatenate([aa, kr, z(LANES - 2 * GATE_RANK - ROPE_B)], axis=1)
    krp = jnp.concatenate([z(NOPE_B), kr, z(LANES - QK_B)], axis=1)
    part = np.arange(ROPE_B) ^ (ROPE_B // 4)
    krq = jnp.concatenate([z(NOPE_B), kr[:, part], z(LANES - QK_B)], axis=1)
    w_in_a = jnp.concatenate([aq, ak, av, ar, qd, kvd, dq, dk, dv, misc, krp, krq], axis=1).astype(BF16)
    uq = p["w_mla_uq"][l].reshape(Q_RANK, H_B, QK_B)
    uq_part = jnp.concatenate([jnp.zeros((Q_RANK, H_B, NOPE_B), F32), uq[:, :, NOPE_B + part]], axis=2)
    gain_part = lambda g: jnp.pad(g[NOPE_B + part], (NOPE_B, LANES - QK_B)).reshape(1, LANES)
    wa2 = jnp.zeros((LANES, 512), F32)
    wa2 = wa2.at[0:GATE_RANK, 0:256].set(p["w_gla_a2"][l, 0])
    wa2 = wa2.at[GATE_RANK:2 * GATE_RANK, 256:512].set(p["w_gla_a2"][l, 1])
    pad_gain = lambda g: jnp.pad(g, (0, HEAD_PAD - QK_B)).reshape(1, HEAD_PAD)
    twice = lambda g: jnp.concatenate([g, g]).reshape(1, LANES)
    return dict(
        g1=p["g_norm1"][l].reshape(1, D_MODEL), g2=p["g_norm2"][l].reshape(1, D_MODEL),
        w_in_a=w_in_a, w_gates=gates.astype(BF16),
        wa2=wa2.astype(BF16), ba=p["b_gla_a"][l].reshape(1, 512),
        gqa=p["g_mla_qa"][l].reshape(1, Q_RANK), gkva=p["g_mla_kva"][l].reshape(1, KV_RANK),
        wuq=_pad_heads(p["w_mla_uq"][l], QK_B, HEAD_PAD).astype(BF16),
        wuk=_pad_heads(p["w_mla_uk"][l], NOPE_B, HEAD_PAD).astype(BF16),
        wuv=_pad_heads(p["w_mla_uv"][l], V_B, HEAD_PAD).astype(BF16),
        gq=pad_gain(p["g_mla_q"][l]), gk=pad_gain(p["g_mla_k"][l]),
        wuq_part=_pad_heads(uq_part.reshape(Q_RANK, H_B * QK_B), QK_B, HEAD_PAD).astype(BF16),
        gq_part=gain_part(p["g_mla_q"][l]), gk_part=gain_part(p["g_mla_k"][l]),
        gdq=twice(p["g_diff_q"][l]), gdk=twice(p["g_diff_k"][l]),
        ggla=p["g_gla_out"][l].reshape(1, DV_A), gsub=p["g_diff_sub"][l].reshape(1, 2 * D_C),
        woa=p["w_o_gla"][l].astype(BF16), wob=p["w_o_mla"][l].astype(BF16), woc=p["w_o_diff"][l].astype(BF16),
        wout=p["w_out"][l].astype(BF16), w1=p["w_mlp1"][l].astype(BF16), w2=p["w_mlp2"][l].astype(BF16),
        lam_qk=p["lam_qk"][l],
    )


def _layer(x2, bsz, seq, mod8, group0, lw, lam_init, consts, tabs, cached):
    is_latent = cached is not None
    cache = None
    if is_latent:
        s0, ckv_c, krope_c, dk_c, dv_c = cached
        past = ckv_c.shape[1]
        krp_c = jnp.pad(krope_c, ((0, 0), (0, 0), (NOPE_B, LANES - QK_B)))
        cache = (ckv_c, krp_c, dk_c.reshape(bsz, past, W_C), dv_c.reshape(bsz, past, W_C))
    outs = _inproj(x2, bsz, mod8, group0, lw, tabs, cache, rope=is_latent, emit_state=not is_latent)
    gq, gk, gv, sr, gf, gb, mq, mk, mv, dq, dk, dv = outs[:12]
    r3 = lambda a: a.reshape(bsz, seq, a.shape[-1])

    init = (_state_to_blocks(s0[:, 0]), _state_to_blocks(s0[:, 1])) if is_latent else None
    gla = _gla_scan(r3(gq), r3(gk), r3(gv), r3(gf), r3(gb), init, consts, emit_state=not is_latent)
    of, ob = gla[0], gla[1]

    yb = _mla_attention(r3(mq), mk, mv)
    yc = _diff_attention(lw["lam_qk"], r3(dq), dk, dv, lam_init)

    x_out = _merge_mlp(x2, mod8, group0, lw, of.reshape(-1, W_A), ob.reshape(-1, W_A), sr, yb.reshape(-1, W_B),
                       yc.reshape(-1, W_C), lam_init, is_latent)

    new_state = None
    if not is_latent:
        ckv, misc, dk32, dv32 = outs[12:]
        gla_state = jnp.stack([gla[2], gla[3]], axis=1)
        new_state = (gla_state, ckv.reshape(bsz, seq, KV_RANK),
                     misc[:, 2 * GATE_RANK:2 * GATE_RANK + ROPE_B].reshape(bsz, seq, ROPE_B),
                     dk32.reshape(bsz, seq, H_C, 2, D_C), dv32.reshape(bsz, seq, H_C, 2 * D_C))
    return x_out, new_state


def kernel(x_prompt, x_sample, state_gla, cache_mla_ckv, cache_mla_krope, cache_diff_k, cache_diff_v, c, c_ctx, w_mod, b_mod, g_norm1, g_norm2, w_in, w_gla_a2, b_gla_a, g_gla_out, g_mla_qa, g_mla_kva, w_mla_uq, w_mla_uk, w_mla_uv, g_mla_q, g_mla_k, g_diff_q, g_diff_k, lam_qk, g_diff_sub, w_o_gla, w_o_mla, w_o_diff, w_out, w_mlp1, w_mlp2):
    params = dict(w_in=w_in, g_norm1=g_norm1, g_norm2=g_norm2, w_gla_a2=w_gla_a2, b_gla_a=b_gla_a,
                  g_gla_out=g_gla_out, g_mla_qa=g_mla_qa, g_mla_kva=g_mla_kva, w_mla_uq=w_mla_uq,
                  w_mla_uk=w_mla_uk, w_mla_uv=w_mla_uv, g_mla_q=g_mla_q, g_mla_k=g_mla_k, g_diff_q=g_diff_q,
                  g_diff_k=g_diff_k, lam_qk=lam_qk, g_diff_sub=g_diff_sub, w_o_gla=w_o_gla, w_o_mla=w_o_mla,
                  w_o_diff=w_o_diff, w_out=w_out, w_mlp1=w_mlp1, w_mlp2=w_mlp2)
    nb, ns, _ = x_prompt.shape
    db, ds, _ = x_sample.shape
    assert ds == LAT_SEQ and db + 1 <= 8

    cond8 = jnp.concatenate([c_ctx[None, :], c, jnp.zeros((8 - 1 - db, D_MODEL), F32)], axis=0)
    mod = _modulation(cond8, w_mod, b_mod).reshape(DEPTH, 8, 6, D_MODEL)
    mod = jnp.pad(mod, ((0, 0), (0, 0), (0, 2), (0, 0)))

    consts = _gla_constants()
    tabs = (_rope_tables(ROPE_B, NOPE_B, LANES), _rope_tables(D_C, 0, D_C))

    yp = x_prompt.reshape(nb * ns, D_MODEL)
    ys = x_sample.reshape(db * ds, D_MODEL)
    states = []
    for l in range(DEPTH):
        lw = _layer_weights(l, params)
        lam_init = 0.8 - 0.6 * math.exp(-0.3 * l)
        yp, st = _layer(yp, nb, ns, mod[l], 0, lw, lam_init, consts, tabs, None)
        states.append(st)
        cached = (state_gla[:, l], cache_mla_ckv[:, l], cache_mla_krope[:, l], cache_diff_k[:, l],
                  cache_diff_v[:, l])
        ys, _ = _layer(ys, db, ds, mod[l], 1, lw, lam_init, consts, tabs, cached)
    stack = lambda i: jnp.stack([s[i] for s in states], axis=1)
    return (yp.reshape(nb, ns, D_MODEL), ys.reshape(db, ds, D_MODEL), stack(0), stack(1), stack(2), stack(3),
            stack(4))
```
